```python
import math
import jax, jax.numpy as jnp
from jax import lax
import numpy as np

D_MODEL = 1024
BATCH = 16
SEQ = 256
DEPTH = 2
DEC_BATCH = 8
DEC_SEQ = 2048
PAST_LEN = 512

GRID_W = 64
POS_BASE = 10000.0
N_EVEN = (DEPTH + 1) // 2
N_ODD = DEPTH // 2
N_DIR = 2
A_HEADS = 12
A_HEAD_DIM = 64
D_A = A_HEADS * A_HEAD_DIM
DECAY_LORA = 64
AAA_LORA = 64
GATE_LORA = 128
B_GROUPS = 4
B_GROUP_DIM = 64
D_B = B_GROUPS * B_GROUP_DIM
D_C = 512
CONV_WIDTH = 31
D_HEADS = 4
D_HEAD_DIM = 128
D_D = D_HEADS * D_HEAD_DIM
CHUNK = 128
N_EXPERTS = 16
D_EXPERT = 1024
EC_FACTOR = 2
RMS_EPS = 1e-6
LN_EPS = 1e-5
GN_EPS = 64e-5

kernel_name = 'hybrid_rwkv7_fnet_conformer_gmlp_ec_moe_diffusion_step'


def rmsnorm(x, g):
    xf = x.astype(jnp.float32)
    y = xf * lax.rsqrt(jnp.mean(xf * xf, axis=-1, keepdims=True) + RMS_EPS)
    return (y * g.astype(jnp.float32)).astype(x.dtype)


def layernorm(x, g, b):
    xf = x.astype(jnp.float32)
    mu = jnp.mean(xf, axis=-1, keepdims=True)
    var = jnp.mean(jnp.square(xf - mu), axis=-1, keepdims=True)
    y = (xf - mu) * lax.rsqrt(var + LN_EPS) * g.astype(jnp.float32) + b.astype(jnp.float32)
    return y.astype(x.dtype)


def centred_shift(x):
    xp = jnp.pad(x, ((0, 0), (1, 1), (0, 0)))
    return 0.5 * (xp[:, :-2] + xp[:, 2:])


def modulation(cvec, w_mod, b_mod):
    m = jax.nn.silu(cvec) @ w_mod + b_mod
    return jnp.split(m[:, None, :], 6, axis=-1)


def grid_posemb(n_tokens, dtype):
    rows = n_tokens // GRID_W
    row = jnp.repeat(jnp.arange(rows, dtype=jnp.float32), GRID_W)
    col = jnp.tile(jnp.arange(GRID_W, dtype=jnp.float32), rows)
    quarter = D_MODEL // 4
    freq = 1.0 / (POS_BASE ** (jnp.arange(quarter, dtype=jnp.float32) / quarter))
    ar = row[:, None] * freq[None, :]
    ac = col[:, None] * freq[None, :]
    return jnp.concatenate([jnp.sin(ar), jnp.cos(ar), jnp.sin(ac), jnp.cos(ac)], axis=-1).astype(dtype)


def wkv7_scan(r, w, k, v, kk, a, s0):
    def step(S, inp):
        r_t, w_t, k_t, v_t, kk_t, a_t = inp
        s_kk = jnp.einsum('zbhvk,zbhk->zbhv', S, kk_t)
        S = (S * w_t[..., None, :] - s_kk[..., None] * (kk_t * a_t)[..., None, :]
             + v_t[..., :, None] * k_t[..., None, :])
        return S, jnp.einsum('zbhvk,zbhk->zbhv', S, r_t)
    xs = tuple(jnp.moveaxis(z, 2, 0) for z in (r, w, k, v, kk, a))
    s_fin, ys = lax.scan(step, s0, xs)
    return jnp.moveaxis(ys, 0, 2), s_fin


def rwkv7_mixer(h, r, k, v, s0, mu_rkv, mu_wag, w0, w1, w2, a0, a1, a2, g1, g2,
                k_k, k_a, r_k, gn_w, gn_b):
    B_, L, _ = h.shape
    f32 = jnp.float32
    heads = lambda z: z.reshape(z.shape[:-1] + (A_HEADS, A_HEAD_DIM))
    r = r + (centred_shift(r) - r) * mu_rkv[0]
    k = k + (centred_shift(k) - k) * mu_rkv[1]
    v = v + (centred_shift(v) - v) * mu_rkv[2]
    dh = centred_shift(h) - h
    xw = h + dh * mu_wag[0]
    xa = h + dh * mu_wag[1]
    xg = h + dh * mu_wag[2]
    w_pre = w0[:, None, None, :] + jnp.einsum('zblr,zrc->zblc', jnp.tanh(jnp.einsum('bld,zdr->zblr', xw, w1)), w2)
    decay = jnp.exp(-jnp.exp(-jax.nn.softplus(-w_pre.astype(f32)) - 0.5))
    icl = jax.nn.sigmoid((a0[:, None, None, :] + jnp.einsum('zblr,zrc->zblc', jnp.einsum('bld,zdr->zblr', xa, a1), a2)).astype(f32))
    gate = jax.nn.sigmoid(xg @ g1) @ g2
    kf = k.astype(f32)
    kk = heads(kf * k_k)
    kk = kk / jnp.maximum(jnp.linalg.norm(kk, axis=-1, keepdims=True), 1e-12)
    k_dir = heads(kf[None] * (1.0 + (icl - 1.0) * k_a))
    rh = heads(r.astype(f32))
    vh = heads(v.astype(f32))
    both = lambda z: jnp.stack([z, jnp.flip(z, axis=1)])
    align = lambda z: jnp.stack([z[0], jnp.flip(z[1], axis=1)])
    ys, s_fin = wkv7_scan(both(rh), align(heads(decay)), align(k_dir), both(vh), both(kk),
                          align(heads(icl)), jnp.swapaxes(s0.astype(f32), 0, 1))
    y = ys[0] + jnp.flip(ys[1], axis=1)
    mu = jnp.mean(y, axis=-1, keepdims=True)
    var = jnp.mean(jnp.square(y - mu), axis=-1, keepdims=True)
    y = (y - mu) * lax.rsqrt(var + GN_EPS) * heads(gn_w.astype(f32)) + heads(gn_b.astype(f32))
    bonus = jnp.sum(rh * heads(kf) * r_k, axis=-1, keepdims=True) * vh
    out = ((y + bonus).reshape(B_, L, D_A) * gate).astype(h.dtype)
    return out, jnp.swapaxes(s_fin, 0, 1)


def fourier_mixer(u):
    B_, L, _ = u.shape
    ug = u.reshape(B_, L, B_GROUPS, B_GROUP_DIM).astype(jnp.float32)
    f = jnp.fft.fft2(ug, axes=(1, 3), norm='ortho').real
    return f.reshape(B_, L, D_B).astype(u.dtype)


def conformer_conv(p, conv_w, conv_b, ln_g, ln_b):
    a, b = jnp.split(p, 2, axis=-1)
    x = a * jax.nn.sigmoid(b)
    pad = CONV_WIDTH // 2
    y = lax.conv_general_dilated(x, conv_w[:, None, :].astype(x.dtype), window_strides=(1,),
                                 padding=[(pad, pad)], dimension_numbers=('NWC', 'WIO', 'NWC'),
                                 feature_group_count=D_C) + conv_b
    return jax.nn.silu(layernorm(y, ln_g, ln_b))


def chunk_gating(p, ln_g, ln_b, w_s, b_s):
    p = jax.nn.gelu(p)
    u, v = jnp.split(p, 2, axis=-1)
    v = layernorm(v, ln_g, ln_b)
    B_, L, _ = v.shape
    vc = v.reshape(B_, L // CHUNK, CHUNK, D_HEADS, D_HEAD_DIM)
    mixed = jnp.einsum('hij,bnjhc->bnihc', w_s, vc) + b_s.T[None, None, :, :, None]
    return u * mixed.reshape(B_, L, D_D)


def expert_choice_ffn(h, w_router, b_router, w_gate, w_up, w_down):
    B_, L, D = h.shape
    cap = EC_FACTOR * L // N_EXPERTS
    logits = (h @ w_router).astype(jnp.float32) + b_router.astype(jnp.float32)
    aff = jax.nn.softmax(logits, axis=-1)
    vals, idx = lax.top_k(jnp.swapaxes(aff, 1, 2), cap)
    xs = jax.vmap(lambda hb, ib: hb[ib])(h, idx)
    hid = jax.nn.silu(jnp.einsum('becd,edf->becf', xs, w_gate)) * jnp.einsum('becd,edf->becf', xs, w_up)
    ys = (jnp.einsum('becf,efd->becd', hid, w_down) * vals[..., None]).astype(h.dtype)
    return jax.vmap(lambda ib, yb: jnp.zeros((L, D), yb.dtype).at[ib.reshape(-1)].add(yb.reshape(-1, D)))(idx, ys)


def run_trunk(x, cvec, s_init, P):
    states = []
    for l in range(DEPTH):
        j = l // 2
        sh1, sc1, gt1, sh2, sc2, gt2 = modulation(cvec, P['mod_w'][l], P['mod_b'][l])
        h = rmsnorm(x, P['norm_mix'][l]) * (1.0 + sc1) + sh1
        if l % 2 == 0:
            proj = h @ P['ev_w_in'][j]
            r, k, v, u = jnp.split(proj, [D_A, 2 * D_A, 3 * D_A], axis=-1)
            o_a, s_fin = rwkv7_mixer(h, r, k, v, s_init[:, j], P['ev_mu_rkv'][j], P['ev_mu_wag'][j],
                                     P['ev_w0'][j], P['ev_w1'][j], P['ev_w2'][j],
                                     P['ev_a0'][j], P['ev_a1'][j], P['ev_a2'][j],
                                     P['ev_g1'][j], P['ev_g2'][j], P['ev_k_k'][j], P['ev_k_a'][j],
                                     P['ev_r_k'][j], P['ev_gn_w'][j], P['ev_gn_b'][j])
            o_b = fourier_mixer(u)
            mixed = jnp.concatenate([o_a, o_b.astype(o_a.dtype)], axis=-1) @ P['ev_w_out'][j]
            states.append(s_fin)
        else:
            proj = h @ P['od_w_in'][j]
            pc, pd = jnp.split(proj, [2 * D_C], axis=-1)
            o_c = conformer_conv(pc, P['od_conv_w'][j], P['od_conv_b'][j], P['od_cln_g'][j], P['od_cln_b'][j])
            o_d = chunk_gating(pd, P['od_vln_g'][j], P['od_vln_b'][j], P['od_w_s'][j], P['od_b_s'][j])
            mixed = jnp.concatenate([o_c, o_d], axis=-1) @ P['od_w_out'][j]
        x = x + gt1 * mixed
        h2 = rmsnorm(x, P['norm_ffn'][l]) * (1.0 + sc2) + sh2
        x = x + gt2 * expert_choice_ffn(h2, P['moe_router'][l], P['moe_router_b'][l],
                                        P['moe_w_gate'][l], P['moe_w_up'][l], P['moe_w_down'][l])
    return rmsnorm(x, P['final_norm']), jnp.stack(states, axis=1)


def setup_inputs(seed: int = 0) -> dict:
    key = jax.random.key(seed)
    keys = jax.random.split(key, 48)

    def nrm(i, shape, scale):
        return jax.random.normal(keys[i], shape, jnp.float32) * scale

    def uni(i, shape, lo, hi):
        return jax.random.uniform(keys[i], shape, jnp.float32, lo, hi)

    D = D_MODEL
    return {
        'x_prompt': nrm(0, (BATCH, SEQ, D), 1.0),
        'x_sample': nrm(1, (DEC_BATCH, DEC_SEQ, D), 1.0),
        'state_wkv': nrm(2, (DEC_BATCH, N_EVEN, N_DIR, A_HEADS, A_HEAD_DIM, A_HEAD_DIM), 1.0),
        'c': nrm(3, (DEC_BATCH, D), 1.0),
        'c_ctx': nrm(4, (D,), 1.0),
        'mod_w': nrm(5, (DEPTH, D, 6 * D), 0.5 * D ** -0.5),
        'mod_b': nrm(6, (DEPTH, 6 * D), 0.02),
        'norm_mix': 1.0 + nrm(7, (DEPTH, D), 0.05),
        'norm_ffn': 1.0 + nrm(8, (DEPTH, D), 0.05),
        'final_norm': 1.0 + nrm(9, (D,), 0.05),
        'ev_w_in': nrm(10, (N_EVEN, D, 3 * D_A + D_B), D ** -0.5),
        'ev_w_out': nrm(11, (N_EVEN, D_A + D_B, D), (D_A + D_B) ** -0.5),
        'ev_mu_rkv': uni(12, (N_EVEN, 3, D_A), 0.0, 1.0),
        'ev_mu_wag': uni(13, (N_EVEN, 3, D), 0.0, 1.0),
        'ev_w0': uni(14, (N_EVEN, N_DIR, D_A), -5.0, 1.0),
        'ev_w1': nrm(15, (N_EVEN, N_DIR, D, DECAY_LORA), D ** -0.5),
        'ev_w2': nrm(16, (N_EVEN, N_DIR, DECAY_LORA, D_A), 0.5 * DECAY_LORA ** -0.5),
        'ev_a0': nrm(17, (N_EVEN, N_DIR, D_A), 0.3),
        'ev_a1': nrm(18, (N_EVEN, N_DIR, D, AAA_LORA), D ** -0.5),
        'ev_a2': nrm(19, (N_EVEN, N_DIR, AAA_LORA, D_A), 0.5 * AAA_LORA ** -0.5),
        'ev_g1': nrm(20, (N_EVEN, D, GATE_LORA), D ** -0.5),
        'ev_g2': nrm(21, (N_EVEN, GATE_LORA, D_A), GATE_LORA ** -0.5),
        'ev_k_k': 0.85 + nrm(22, (N_EVEN, D_A), 0.05),
        'ev_k_a': 1.0 + nrm(23, (N_EVEN, D_A), 0.05),
        'ev_r_k': nrm(24, (N_EVEN, A_HEADS, A_HEAD_DIM), 0.1),
        'ev_gn_w': 1.0 + nrm(25, (N_EVEN, D_A), 0.05),
        'ev_gn_b': nrm(26, (N_EVEN, D_A), 0.02),
        'od_w_in': nrm(27, (N_ODD, D, 2 * D_C + 2 * D_D), D ** -0.5),
        'od_w_out': nrm(28, (N_ODD, D_C + D_D, D), (D_C + D_D) ** -0.5),
        'od_conv_w': nrm(29, (N_ODD, CONV_WIDTH, D_C), CONV_WIDTH ** -0.5),
        'od_conv_b': nrm(30, (N_ODD, D_C), 0.02),
        'od_cln_g': 1.0 + nrm(31, (N_ODD, D_C), 0.05),
        'od_cln_b': nrm(32, (N_ODD, D_C), 0.02),
        'od_vln_g': 1.0 + nrm(33, (N_ODD, D_D), 0.05),
        'od_vln_b': nrm(34, (N_ODD, D_D), 0.02),
        'od_w_s': nrm(35, (N_ODD, D_HEADS, CHUNK, CHUNK), CHUNK ** -0.5),
        'od_b_s': 1.0 + nrm(36, (N_ODD, D_HEADS, CHUNK), 0.1),
        'moe_router': nrm(37, (DEPTH, D, N_EXPERTS), D ** -0.5),
        'moe_router_b': nrm(38, (DEPTH, N_EXPERTS), 0.01),
        'moe_w_gate': nrm(39, (DEPTH, N_EXPERTS, D, D_EXPERT), D ** -0.5),
        'moe_w_up': nrm(40, (DEPTH, N_EXPERTS, D, D_EXPERT), D ** -0.5),
        'moe_w_down': nrm(41, (DEPTH, N_EXPERTS, D_EXPERT, D), D_EXPERT ** -0.5),
    }


def reference(x_prompt, x_sample, state_wkv, c, c_ctx, mod_w, mod_b, norm_mix, norm_ffn, final_norm,
              ev_w_in, ev_w_out, ev_mu_rkv, ev_mu_wag, ev_w0, ev_w1, ev_w2, ev_a0, ev_a1, ev_a2,
              ev_g1, ev_g2, ev_k_k, ev_k_a, ev_r_k, ev_gn_w, ev_gn_b,
              od_w_in, od_w_out, od_conv_w, od_conv_b, od_cln_g, od_cln_b, od_vln_g, od_vln_b,
              od_w_s, od_b_s, moe_router, moe_router_b, moe_w_gate, moe_w_up, moe_w_down):
    P = dict(mod_w=mod_w, mod_b=mod_b, norm_mix=norm_mix, norm_ffn=norm_ffn, final_norm=final_norm,
             ev_w_in=ev_w_in, ev_w_out=ev_w_out, ev_mu_rkv=ev_mu_rkv, ev_mu_wag=ev_mu_wag,
             ev_w0=ev_w0, ev_w1=ev_w1, ev_w2=ev_w2, ev_a0=ev_a0, ev_a1=ev_a1, ev_a2=ev_a2,
             ev_g1=ev_g1, ev_g2=ev_g2, ev_k_k=ev_k_k, ev_k_a=ev_k_a, ev_r_k=ev_r_k,
             ev_gn_w=ev_gn_w, ev_gn_b=ev_gn_b,
             od_w_in=od_w_in, od_w_out=od_w_out, od_conv_w=od_conv_w, od_conv_b=od_conv_b,
             od_cln_g=od_cln_g, od_cln_b=od_cln_b, od_vln_g=od_vln_g, od_vln_b=od_vln_b,
             od_w_s=od_w_s, od_b_s=od_b_s, moe_router=moe_router, moe_router_b=moe_router_b,
             moe_w_gate=moe_w_gate, moe_w_up=moe_w_up, moe_w_down=moe_w_down)
    s_zero = jnp.zeros((x_prompt.shape[0], N_EVEN, N_DIR, A_HEADS, A_HEAD_DIM, A_HEAD_DIM), jnp.float32)
    y_prompt, new_state_wkv = run_trunk(x_prompt, c_ctx[None, :], s_zero, P)
    x_lat = x_sample + grid_posemb(x_sample.shape[1], x_sample.dtype)[None]
    y_sample, _ = run_trunk(x_lat, c, state_wkv, P)
    return (y_prompt, y_sample, new_state_wkv)
```

```python
import functools
import math

import jax
import jax.numpy as jnp
from jax import lax
from jax.experimental import pallas as pl
from jax.experimental.pallas import tpu as pltpu

D_MODEL = 1024
DEPTH = 2
GRID_W = 64
POS_BASE = 10000.0
A_HEADS = 12
A_HEAD_DIM = 64
D_A = A_HEADS * A_HEAD_DIM
B_GROUPS = 4
B_GROUP_DIM = 64
D_B = B_GROUPS * B_GROUP_DIM
D_C = 512
CONV_WIDTH = 31
D_HEADS = 4
D_HEAD_DIM = 128
D_D = D_HEADS * D_HEAD_DIM
CHUNK = 128
N_EXPERTS = 16
EC_FACTOR = 2
RMS_EPS = 1e-6
LN_EPS = 1e-5
GN_EPS = 64e-5

LANES = 128
SCAN_CHUNK = 64
HEADS_PER_TILE = LANES // A_HEAD_DIM
N_HEAD_PAIRS = A_HEADS // HEADS_PER_TILE

F32 = jnp.float32
BF16 = jnp.bfloat16


def _mm(a, b, dims, exact=False):
    if exact:
        return lax.dot_general(a, b, (dims, ((), ())), precision=lax.Precision.HIGHEST,
                               preferred_element_type=F32)
    return lax.dot_general(a.astype(BF16), b.astype(BF16), (dims, ((), ())),
                           preferred_element_type=F32)


_NN = ((1,), (0,))
_NT = ((1,), (1,))
_TN = ((0,), (0,))


def _unit_triangular_inverse(a, same16, same32):
    c = a.shape[0]
    eye = (lax.broadcasted_iota(jnp.int32, (c, c), 0) == lax.broadcasted_iota(jnp.int32, (c, c), 1)).astype(F32)
    n = jnp.where(same16, -a, 0.0)
    t = eye + n
    pw = n
    for _ in range(3):
        pw = _mm(pw, pw, _NN, exact=True)
        t = t + _mm(t, pw, _NN, exact=True)
    e32 = jnp.where(jnp.logical_and(same32, jnp.logical_not(same16)), a, 0.0)
    t = t - _mm(_mm(t, e32, _NN, exact=True), t, _NN, exact=True)
    e64 = jnp.where(same32, 0.0, a)
    t = t - _mm(_mm(t, e64, _NN, exact=True), t, _NN, exact=True)
    return t


def _wkv_chunk(r, v, kk, lw, k, a, s, reverse):
    c = r.shape[0]
    ti = lax.broadcasted_iota(jnp.int32, (c, c), 0)
    si = lax.broadcasted_iota(jnp.int32, (c, c), 1)
    if reverse:
        incl, strict = ti <= si, ti < si
    else:
        incl, strict = ti >= si, ti > si
    same16 = (ti >> 4) == (si >> 4)
    same32 = (ti >> 5) == (si >> 5)
    cs = _mm(incl.astype(F32), lw, _NN, exact=True)
    cs_end = cs[0:1] if reverse else cs[c - 1:c]
    b = kk * a
    g_inv = jnp.exp(-cs)
    g_tail = jnp.exp(cs_end - cs)
    kt = kk * jnp.exp(cs - lw)
    rt = r * jnp.exp(cs)
    kh = k * g_inv
    bh = b * g_inv
    lhs = jnp.concatenate([kt, rt], axis=0)
    p = _mm(lhs, s, _NT)
    lane = lax.broadcasted_iota(jnp.int32, (1, LANES), 1)
    ti2 = lax.broadcasted_iota(jnp.int32, (c, 2 * c), 0)
    si2 = lax.broadcasted_iota(jnp.int32, (c, 2 * c), 1) & (c - 1)
    incl2 = (ti2 <= si2) if reverse else (ti2 >= si2)
    g_mats, u_parts = [], []
    for h in range(HEADS_PER_TILE):
        m = jnp.logical_and(lane >= h * A_HEAD_DIM, lane < (h + 1) * A_HEAD_DIM)
        kh_m = jnp.where(m, kh, 0.0)
        bh_m = jnp.where(m, bh, 0.0)
        g = _mm(lhs, jnp.concatenate([kh_m, bh_m], axis=0), _NT)
        a_kb = jnp.where(strict, _mm(kt, bh_m, _NT), 0.0)
        a_kk = jnp.where(strict, g[0:c, 0:c], 0.0)
        t = _unit_triangular_inverse(a_kb, same16, same32)
        w = p[0:c] + _mm(a_kk, v, _NN)
        u_parts.append(_mm(t, w, _NN))
        g_mats.append(jnp.where(incl2, g[c:2 * c], 0.0))
    u = jnp.where(lane < A_HEAD_DIM, u_parts[0], u_parts[1])
    vu = jnp.concatenate([v, -u], axis=0)
    y_parts = [p[c:2 * c] + _mm(g_mats[h], vu, _NN) for h in range(HEADS_PER_TILE)]
    y = jnp.where(lane < A_HEAD_DIM, y_parts[0], y_parts[1])
    kb_tail = jnp.concatenate([k * g_tail, b * g_tail], axis=0)
    ds = _mm(vu, kb_tail, _TN)
    head_shift = A_HEAD_DIM.bit_length() - 1
    ri = lax.broadcasted_iota(jnp.int32, (LANES, LANES), 0) >> head_shift
    ci = lax.broadcasted_iota(jnp.int32, (LANES, LANES), 1) >> head_shift
    s_new = s * jnp.exp(cs_end) + jnp.where(ri == ci, ds, 0.0)
    return y, s_new


def _wkv_kernel(rf, vf, kkf, lwf, kf, af, rb, vb, kkb, lwb, kb, ab, s0_ref,
                yf_ref, yb_ref, s_ref):
    ci = pl.program_id(2)

    @pl.when(ci == 0)
    def _():
        s_ref[...] = s0_ref[...]

    y, s_new = _wkv_chunk(rf[0], vf[0], kkf[0], lwf[0, 0], kf[0, 0], af[0, 0], s_ref[0, 0, 0], False)
    yf_ref[0] = y
    s_ref[0, 0, 0] = s_new
    y, s_new = _wkv_chunk(rb[0], vb[0], kkb[0], lwb[0, 0], kb[0, 0], ab[0, 0], s_ref[0, 1, 0], True)
    yb_ref[0] = y
    s_ref[0, 1, 0] = s_new


def wkv7_scan_pallas(r, v, kk, lw, k, a, s0):
    bsz, seq, _ = r.shape
    c = SCAN_CHUNK
    nc = seq // c
    n = A_HEAD_DIM
    s0p = s0.reshape(bsz, 2, N_HEAD_PAIRS, HEADS_PER_TILE, n, n)
    z = jnp.zeros_like(s0p[:, :, :, 0])
    s0bd = jnp.concatenate([jnp.concatenate([s0p[:, :, :, 0], z], axis=-1),
                            jnp.concatenate([z, s0p[:, :, :, 1]], axis=-1)], axis=-2)

    shared_f = pl.BlockSpec((1, c, LANES), lambda b, p, i: (b, i, p))
    shared_b = pl.BlockSpec((1, c, LANES), lambda b, p, i: (b, nc - 1 - i, p))
    dir_f = pl.BlockSpec((1, 1, c, LANES), lambda b, p, i: (0, b, i, p))
    dir_b = pl.BlockSpec((1, 1, c, LANES), lambda b, p, i: (1, b, nc - 1 - i, p))
    st = pl.BlockSpec((1, 2, 1, LANES, LANES), lambda b, p, i: (b, 0, p, 0, 0))
    yf, yb, sfin = pl.pallas_call(
        _wkv_kernel,
        grid=(bsz, N_HEAD_PAIRS, nc),
        in_specs=[shared_f, shared_f, shared_f, dir_f, dir_f, dir_f,
                  shared_b, shared_b, shared_b, dir_b, dir_b, dir_b, st],
        out_specs=[shared_f, shared_b, st],
        out_shape=[jax.ShapeDtypeStruct((bsz, seq, D_A), F32),
                   jax.ShapeDtypeStruct((bsz, seq, D_A), F32),
                   jax.ShapeDtypeStruct((bsz, 2, N_HEAD_PAIRS, LANES, LANES), F32)],
        compiler_params=pltpu.CompilerParams(
            dimension_semantics=("arbitrary", "arbitrary", "arbitrary")),
        name="wkv7_scan",
    )(r, v, kk, lw, k, a, r, v, kk, lw, k, a, s0bd)
    s_fin = jnp.stack([sfin[..., :n, :n], sfin[..., n:, n:]], axis=3)
    return yf + yb, s_fin.reshape(bsz, 2, A_HEADS, n, n)


def rmsnorm(x, g):
    y = x * lax.rsqrt(jnp.mean(x * x, axis=-1, keepdims=True) + RMS_EPS)
    return y * g


def layernorm(x, g, b):
    mu = jnp.mean(x, axis=-1, keepdims=True)
    var = jnp.mean(jnp.square(x - mu), axis=-1, keepdims=True)
    return (x - mu) * lax.rsqrt(var + LN_EPS) * g + b


def centred_shift(x):
    xp = jnp.pad(x, ((0, 0), (1, 1), (0, 0)))
    return 0.5 * (xp[:, :-2] + xp[:, 2:])


def modulation(cvec, w_mod, b_mod):
    m = jax.nn.silu(cvec) @ w_mod + b_mod
    return jnp.split(m[:, None, :], 6, axis=-1)


def grid_posemb(n_tokens, dtype):
    rows = n_tokens // GRID_W
    row = jnp.repeat(jnp.arange(rows, dtype=F32), GRID_W)
    col = jnp.tile(jnp.arange(GRID_W, dtype=F32), rows)
    quarter = D_MODEL // 4
    freq = 1.0 / (POS_BASE ** (jnp.arange(quarter, dtype=F32) / quarter))
    ar = row[:, None] * freq[None, :]
    ac = col[:, None] * freq[None, :]
    return jnp.concatenate([jnp.sin(ar), jnp.cos(ar), jnp.sin(ac), jnp.cos(ac)], axis=-1).astype(dtype)


def rwkv7_mixer(h, r, k, v, s0, mu_rkv, mu_wag, w0, w1, w2, a0, a1, a2, g1, g2,
                k_k, k_a, r_k, gn_w, gn_b):
    B_, L, _ = h.shape
    heads = lambda z: z.reshape(z.shape[:-1] + (A_HEADS, A_HEAD_DIM))
    r = r + (centred_shift(r) - r) * mu_rkv[0]
    k = k + (centred_shift(k) - k) * mu_rkv[1]
    v = v + (centred_shift(v) - v) * mu_rkv[2]
    dh = centred_shift(h) - h
    xw = h + dh * mu_wag[0]
    xa = h + dh * mu_wag[1]
    xg = h + dh * mu_wag[2]
    w_pre = w0[:, None, None, :] + jnp.einsum('zblr,zrc->zblc', jnp.tanh(jnp.einsum('bld,zdr->zblr', xw, w1)), w2)
    lw = -jnp.exp(-jax.nn.softplus(-w_pre) - 0.5)
    icl = jax.nn.sigmoid(a0[:, None, None, :] + jnp.einsum('zblr,zrc->zblc', jnp.einsum('bld,zdr->zblr', xa, a1), a2))
    gate = jax.nn.sigmoid(xg @ g1) @ g2
    kkf = k * k_k
    kkh = heads(kkf)
    kkh = kkh / jnp.maximum(jnp.linalg.norm(kkh, axis=-1, keepdims=True), 1e-12)
    kk = kkh.reshape(B_, L, D_A)
    k_dir = k[None] * (1.0 + (icl - 1.0) * k_a)
    y, s_fin = wkv7_scan_pallas(r, v, kk, lw, k_dir, icl, s0)
    y = heads(y)
    rh, vh = heads(r), heads(v)
    mu = jnp.mean(y, axis=-1, keepdims=True)
    var = jnp.mean(jnp.square(y - mu), axis=-1, keepdims=True)
    y = (y - mu) * lax.rsqrt(var + GN_EPS) * heads(gn_w) + heads(gn_b)
    bonus = jnp.sum(rh * heads(k) * r_k, axis=-1, keepdims=True) * vh
    out = (y + bonus).reshape(B_, L, D_A) * gate
    return out, s_fin


def fourier_mixer(u):
    B_, L, _ = u.shape
    ug = u.reshape(B_, L, B_GROUPS, B_GROUP_DIM)
    f = jnp.fft.fft2(ug, axes=(1, 3), norm='ortho').real
    return f.reshape(B_, L, D_B)


def conformer_conv(p, conv_w, conv_b, ln_g, ln_b):
    a, b = jnp.split(p, 2, axis=-1)
    x = a * jax.nn.sigmoid(b)
    pad = CONV_WIDTH // 2
    y = lax.conv_general_dilated(x, conv_w[:, None, :], window_strides=(1,),
                                 padding=[(pad, pad)], dimension_numbers=('NWC', 'WIO', 'NWC'),
                                 feature_group_count=D_C) + conv_b
    return jax.nn.silu(layernorm(y, ln_g, ln_b))


def chunk_gating(p, ln_g, ln_b, w_s, b_s):
    p = jax.nn.gelu(p)
    u, v = jnp.split(p, 2, axis=-1)
    v = layernorm(v, ln_g, ln_b)
    B_, L, _ = v.shape
    vc = v.reshape(B_, L // CHUNK, CHUNK, D_HEADS, D_HEAD_DIM)
    mixed = jnp.einsum('hij,bnjhc->bnihc', w_s, vc) + b_s.T[None, None, :, :, None]
    return u * mixed.reshape(B_, L, D_D)


def expert_choice_ffn(h, w_router, b_router, w_gate, w_up, w_down):
    B_, L, D = h.shape
    cap = EC_FACTOR * L // N_EXPERTS
    logits = h @ w_router + b_router
    aff = jax.nn.softmax(logits, axis=-1)
    vals, idx = lax.top_k(jnp.swapaxes(aff, 1, 2), cap)
    xs = jax.vmap(lambda hb, ib: hb[ib])(h, idx)
    hid = jax.nn.silu(jnp.einsum('becd,edf->becf', xs, w_gate)) * jnp.einsum('becd,edf->becf', xs, w_up)
    ys = jnp.einsum('becf,efd->becd', hid, w_down) * vals[..., None]
    return jax.vmap(lambda ib, yb: jnp.zeros((L, D), yb.dtype).at[ib.reshape(-1)].add(yb.reshape(-1, D)))(idx, ys)


def run_trunk(x, cvec, s_init, P):
    states = []
    for l in range(DEPTH):
        j = l // 2
        sh1, sc1, gt1, sh2, sc2, gt2 = modulation(cvec, P['mod_w'][l], P['mod_b'][l])
        h = rmsnorm(x, P['norm_mix'][l]) * (1.0 + sc1) + sh1
        if l % 2 == 0:
            proj = h @ P['ev_w_in'][j]
            r, k, v, u = jnp.split(proj, [D_A, 2 * D_A, 3 * D_A], axis=-1)
            o_a, s_fin = rwkv7_mixer(h, r, k, v, s_init[:, j], P['ev_mu_rkv'][j], P['ev_mu_wag'][j],
                                     P['ev_w0'][j], P['ev_w1'][j], P['ev_w2'][j],
                                     P['ev_a0'][j], P['ev_a1'][j], P['ev_a2'][j],
                                     P['ev_g1'][j], P['ev_g2'][j], P['ev_k_k'][j], P['ev_k_a'][j],
                                     P['ev_r_k'][j], P['ev_gn_w'][j], P['ev_gn_b'][j])
            o_b = fourier_mixer(u)
            mixed = jnp.concatenate([o_a, o_b], axis=-1) @ P['ev_w_out'][j]
            states.append(s_fin)
        else:
            proj = h @ P['od_w_in'][j]
            pc, pd = jnp.split(proj, [2 * D_C], axis=-1)
            o_c = conformer_conv(pc, P['od_conv_w'][j], P['od_conv_b'][j], P['od_cln_g'][j], P['od_cln_b'][j])
            o_d = chunk_gating(pd, P['od_vln_g'][j], P['od_vln_b'][j], P['od_w_s'][j], P['od_b_s'][j])
            mixed = jnp.concatenate([o_c, o_d], axis=-1) @ P['od_w_out'][j]
        x = x + gt1 * mixed
        h2 = rmsnorm(x, P['norm_ffn'][l]) * (1.0 + sc2) + sh2
        x = x + gt2 * expert_choice_ffn(h2, P['moe_router'][l], P['moe_router_b'][l],
                                        P['moe_w_gate'][l], P['moe_w_up'][l], P['moe_w_down'][l])
    return rmsnorm(x, P['final_norm']), jnp.stack(states, axis=1)


def kernel(x_prompt, x_sample, state_wkv, c, c_ctx, mod_w, mod_b, norm_mix, norm_ffn, final_norm,
           ev_w_in, ev_w_out, ev_mu_rkv, ev_mu_wag, ev_w0, ev_w1, ev_w2, ev_a0, ev_a1, ev_a2,
           ev_g1, ev_g2, ev_k_k, ev_k_a, ev_r_k, ev_gn_w, ev_gn_b,
           od_w_in, od_w_out, od_conv_w, od_conv_b, od_cln_g, od_cln_b, od_vln_g, od_vln_b,
           od_w_s, od_b_s, moe_router, moe_router_b, moe_w_gate, moe_w_up, moe_w_down):
    P = dict(mod_w=mod_w, mod_b=mod_b, norm_mix=norm_mix, norm_ffn=norm_ffn, final_norm=final_norm,
             ev_w_in=ev_w_in, ev_w_out=ev_w_out, ev_mu_rkv=ev_mu_rkv, ev_mu_wag=ev_mu_wag,
             ev_w0=ev_w0, ev_w1=ev_w1, ev_w2=ev_w2, ev_a0=ev_a0, ev_a1=ev_a1, ev_a2=ev_a2,
             ev_g1=ev_g1, ev_g2=ev_g2, ev_k_k=ev_k_k, ev_k_a=ev_k_a, ev_r_k=ev_r_k,
             ev_gn_w=ev_gn_w, ev_gn_b=ev_gn_b,
             od_w_in=od_w_in, od_w_out=od_w_out, od_conv_w=od_conv_w, od_conv_b=od_conv_b,
             od_cln_g=od_cln_g, od_cln_b=od_cln_b, od_vln_g=od_vln_g, od_vln_b=od_vln_b,
             od_w_s=od_w_s, od_b_s=od_b_s, moe_router=moe_router, moe_router_b=moe_router_b,
             moe_w_gate=moe_w_gate, moe_w_up=moe_w_up, moe_w_down=moe_w_down)
    n_even = state_wkv.shape[1]
    s_zero = jnp.zeros((x_prompt.shape[0], n_even, 2, A_HEADS, A_HEAD_DIM, A_HEAD_DIM), F32)
    y_prompt, new_state_wkv = run_trunk(x_prompt, c_ctx[None, :], s_zero, P)
    x_lat = x_sample + grid_posemb(x_sample.shape[1], x_sample.dtype)[None]
    y_sample, _ = run_trunk(x_lat, c, state_wkv, P)
    return (y_prompt, y_sample, new_state_wkv)
```

```python
import functools
import math

import jax
import jax.numpy as jnp
from jax import lax
from jax.experimental import pallas as pl
from jax.experimental.pallas import tpu as pltpu

D_MODEL = 1024
DEPTH = 2
GRID_W = 64
POS_BASE = 10000.0
A_HEADS = 12
A_HEAD_DIM = 64
D_A = A_HEADS * A_HEAD_DIM
B_GROUPS = 4
B_GROUP_DIM = 64
D_B = B_GROUPS * B_GROUP_DIM
D_C = 512
CONV_WIDTH = 31
D_HEADS = 4
D_HEAD_DIM = 128
D_D = D_HEADS * D_HEAD_DIM
CHUNK = 128
N_EXPERTS = 16
EC_FACTOR = 2
RMS_EPS = 1e-6
LN_EPS = 1e-5
GN_EPS = 64e-5

LANES = 128
SCAN_CHUNK = 64
HEADS_PER_TILE = LANES // A_HEAD_DIM
N_HEAD_PAIRS = A_HEADS // HEADS_PER_TILE
PAIRS_PER_STEP = 6

F32 = jnp.float32
BF16 = jnp.bfloat16


def _mm(a, b, dims):
    return lax.dot_general(a.astype(BF16), b.astype(BF16), (dims, ((), ())),
                           preferred_element_type=F32)


_NN = ((1,), (0,))
_NT = ((1,), (1,))
_TN = ((0,), (0,))


def _cumsum_rows(tri, x):
    x1 = x.astype(BF16)
    r1 = x - x1.astype(F32)
    x2 = r1.astype(BF16)
    x3 = (r1 - x2.astype(F32)).astype(BF16)
    dot = lambda y: lax.dot_general(tri, y, (_NN, ((), ())), preferred_element_type=F32)
    return dot(x1) + dot(x2) + dot(x3)


def _unit_triangular_inverses(mats, same16, same32):
    c = mats[0].shape[0]
    eye = (lax.broadcasted_iota(jnp.int32, (c, c), 0) == lax.broadcasted_iota(jnp.int32, (c, c), 1)).astype(F32)
    pws = [jnp.where(same16, -a, 0.0) for a in mats]
    ts = [eye + n for n in pws]
    for _ in range(3):
        pws = [_mm(pw, pw, _NN) for pw in pws]
        ts = [t + _mm(t, pw, _NN) for t, pw in zip(ts, pws)]
    in32 = jnp.logical_and(same32, jnp.logical_not(same16))
    tmp = [_mm(t, jnp.where(in32, a, 0.0), _NN) for t, a in zip(ts, mats)]
    ts = [t - _mm(x, t, _NN) for t, x in zip(ts, tmp)]
    tmp = [_mm(t, jnp.where(same32, 0.0, a), _NN) for t, a in zip(ts, mats)]
    ts = [t - _mm(x, t, _NN) for t, x in zip(ts, tmp)]
    return ts


def _wkv_chunks(items):
    c = items[0][0].shape[0]
    ti = lax.broadcasted_iota(jnp.int32, (c, c), 0)
    si = lax.broadcasted_iota(jnp.int32, (c, c), 1)
    ti2 = lax.broadcasted_iota(jnp.int32, (c, 2 * c), 0)
    si2 = lax.broadcasted_iota(jnp.int32, (c, 2 * c), 1) & (c - 1)
    masks = {False: (ti >= si, ti > si, ti2 >= si2), True: (ti <= si, ti < si, ti2 <= si2)}
    tri = {rev: masks[rev][0].astype(F32).astype(BF16) for rev in (False, True)}
    same16 = (ti >> 4) == (si >> 4)
    same32 = (ti >> 5) == (si >> 5)
    lane = lax.broadcasted_iota(jnp.int32, (1, LANES), 1)
    head_masks = [jnp.logical_and(lane >= h * A_HEAD_DIM, lane < (h + 1) * A_HEAD_DIM)
                  for h in range(HEADS_PER_TILE)]
    first_head = lane < A_HEAD_DIM
    head_shift = A_HEAD_DIM.bit_length() - 1
    block_diag = ((lax.broadcasted_iota(jnp.int32, (LANES, LANES), 0) >> head_shift)
                  == (lax.broadcasted_iota(jnp.int32, (LANES, LANES), 1) >> head_shift))

    cs_all = [_cumsum_rows(tri[rev], lw) for (_, _, _, lw, _, _, _, rev) in items]
    prep = []
    for (r, v, kk, lw, k, a, s, rev), cs in zip(items, cs_all):
        cs_end = cs[0:1] if rev else cs[c - 1:c]
        b = kk * a
        g_inv = jnp.exp(-cs)
        g_tail = jnp.exp(cs_end - cs)
        kt = kk * jnp.exp(cs - lw)
        rt = r * jnp.exp(cs)
        lhs = jnp.concatenate([kt, rt], axis=0)
        kb_tail = jnp.concatenate([k * g_tail, b * g_tail], axis=0)
        prep.append((lhs, kt, k * g_inv, b * g_inv, kb_tail, jnp.exp(cs_end)))
    ps = [_mm(pr[0], it[6], _NT) for pr, it in zip(prep, items)]
    gs, a_kbs = [], []
    for (lhs, kt, kh, bh, _, _), it in zip(prep, items):
        for m in head_masks:
            bh_m = jnp.where(m, bh, 0.0)
            gs.append(_mm(lhs, jnp.concatenate([jnp.where(m, kh, 0.0), bh_m], axis=0), _NT))
            a_kbs.append(jnp.where(masks[it[7]][1], _mm(kt, bh_m, _NT), 0.0))
    ts = _unit_triangular_inverses(a_kbs, same16, same32)
    ws = []
    for i, it in enumerate(items):
        for h in range(HEADS_PER_TILE):
            a_kk = jnp.where(masks[it[7]][1], gs[HEADS_PER_TILE * i + h][0:c, 0:c], 0.0)
            ws.append(ps[i][0:c] + _mm(a_kk, it[1], _NN))
    us = [_mm(t, w, _NN) for t, w in zip(ts, ws)]
    vus = [jnp.concatenate([it[1], -jnp.where(first_head, us[HEADS_PER_TILE * i], us[HEADS_PER_TILE * i + 1])], axis=0)
           for i, it in enumerate(items)]
    ys = []
    for i, it in enumerate(items):
        parts = [ps[i][c:2 * c] + _mm(jnp.where(masks[it[7]][2], gs[HEADS_PER_TILE * i + h][c:2 * c], 0.0), vus[i], _NN)
                 for h in range(HEADS_PER_TILE)]
        ys.append(jnp.where(first_head, parts[0], parts[1]))
    dss = [_mm(vu, pr[4], _TN) for vu, pr in zip(vus, prep)]
    return [(y, it[6] * pr[5] + jnp.where(block_diag, ds, 0.0))
            for y, it, pr, ds in zip(ys, items, prep, dss)]


def _wkv_kernel(rf, vf, kkf, lwf, kf, af, rb, vb, kkb, lwb, kb, ab, s0_ref,
                yf_ref, yb_ref, s_ref):
    @pl.when(pl.program_id(2) == 0)
    def _():
        s_ref[...] = s0_ref[...]

    items = []
    for p in range(PAIRS_PER_STEP):
        ln = slice(p * LANES, (p + 1) * LANES)
        items.append((rf[0, :, ln], vf[0, :, ln], kkf[0, :, ln], lwf[0, 0, :, ln], kf[0, 0, :, ln],
                      af[0, 0, :, ln], s_ref[0, 0, p], False))
        items.append((rb[0, :, ln], vb[0, :, ln], kkb[0, :, ln], lwb[0, 0, :, ln], kb[0, 0, :, ln],
                      ab[0, 0, :, ln], s_ref[0, 1, p], True))
    out = _wkv_chunks(items)
    for p in range(PAIRS_PER_STEP):
        ln = slice(p * LANES, (p + 1) * LANES)
        yf_ref[0, :, ln], s_ref[0, 0, p] = out[2 * p]
        yb_ref[0, :, ln], s_ref[0, 1, p] = out[2 * p + 1]


def wkv7_scan_pallas(r, v, kk, lw, k, a, s0):
    bsz, seq, _ = r.shape
    c = SCAN_CHUNK
    nc = seq // c
    n = A_HEAD_DIM
    s0p = s0.reshape(bsz, 2, N_HEAD_PAIRS, HEADS_PER_TILE, n, n)
    z = jnp.zeros_like(s0p[:, :, :, 0])
    s0bd = jnp.concatenate([jnp.concatenate([s0p[:, :, :, 0], z], axis=-1),
                            jnp.concatenate([z, s0p[:, :, :, 1]], axis=-1)], axis=-2)

    wd = PAIRS_PER_STEP * LANES
    shared_f = pl.BlockSpec((1, c, wd), lambda b, p, i: (b, i, p))
    shared_b = pl.BlockSpec((1, c, wd), lambda b, p, i: (b, nc - 1 - i, p))
    dir_f = pl.BlockSpec((1, 1, c, wd), lambda b, p, i: (0, b, i, p))
    dir_b = pl.BlockSpec((1, 1, c, wd), lambda b, p, i: (1, b, nc - 1 - i, p))
    st = pl.BlockSpec((1, 2, PAIRS_PER_STEP, LANES, LANES), lambda b, p, i: (b, 0, p, 0, 0))
    yf, yb, sfin = pl.pallas_call(
        _wkv_kernel,
        grid=(bsz, N_HEAD_PAIRS // PAIRS_PER_STEP, nc),
        in_specs=[shared_f, shared_f, shared_f, dir_f, dir_f, dir_f,
                  shared_b, shared_b, shared_b, dir_b, dir_b, dir_b, st],
        out_specs=[shared_f, shared_b, st],
        out_shape=[jax.ShapeDtypeStruct((bsz, seq, D_A), F32),
                   jax.ShapeDtypeStruct((bsz, seq, D_A), F32),
                   jax.ShapeDtypeStruct((bsz, 2, N_HEAD_PAIRS, LANES, LANES), F32)],
        compiler_params=pltpu.CompilerParams(
            dimension_semantics=("arbitrary", "arbitrary", "arbitrary")),
        name="wkv7_scan",
    )(r, v, kk, lw, k, a, r, v, kk, lw, k, a, s0bd)
    s_fin = jnp.stack([sfin[..., :n, :n], sfin[..., n:, n:]], axis=3)
    return yf + yb, s_fin.reshape(bsz, 2, A_HEADS, n, n)


def rmsnorm(x, g):
    y = x * lax.rsqrt(jnp.mean(x * x, axis=-1, keepdims=True) + RMS_EPS)
    return y * g


def layernorm(x, g, b):
    mu = jnp.mean(x, axis=-1, keepdims=True)
    var = jnp.mean(jnp.square(x - mu), axis=-1, keepdims=True)
    return (x - mu) * lax.rsqrt(var + LN_EPS) * g + b


def centred_shift(x):
    xp = jnp.pad(x, ((0, 0), (1, 1), (0, 0)))
    return 0.5 * (xp[:, :-2] + xp[:, 2:])


def modulation(cvec, w_mod, b_mod):
    m = jax.nn.silu(cvec) @ w_mod + b_mod
    return jnp.split(m[:, None, :], 6, axis=-1)


def grid_posemb(n_tokens, dtype):
    rows = n_tokens // GRID_W
    row = jnp.repeat(jnp.arange(rows, dtype=F32), GRID_W)
    col = jnp.tile(jnp.arange(GRID_W, dtype=F32), rows)
    quarter = D_MODEL // 4
    freq = 1.0 / (POS_BASE ** (jnp.arange(quarter, dtype=F32) / quarter))
    ar = row[:, None] * freq[None, :]
    ac = col[:, None] * freq[None, :]
    return jnp.concatenate([jnp.sin(ar), jnp.cos(ar), jnp.sin(ac), jnp.cos(ac)], axis=-1).astype(dtype)


def rwkv7_mixer(h, r, k, v, s0, mu_rkv, mu_wag, w0, w1, w2, a0, a1, a2, g1, g2,
                k_k, k_a, r_k, gn_w, gn_b):
    B_, L, _ = h.shape
    heads = lambda z: z.reshape(z.shape[:-1] + (A_HEADS, A_HEAD_DIM))
    r = r + (centred_shift(r) - r) * mu_rkv[0]
    k = k + (centred_shift(k) - k) * mu_rkv[1]
    v = v + (centred_shift(v) - v) * mu_rkv[2]
    dh = centred_shift(h) - h
    xw = h + dh * mu_wag[0]
    xa = h + dh * mu_wag[1]
    xg = h + dh * mu_wag[2]
    w_pre = w0[:, None, None, :] + jnp.einsum('zblr,zrc->zblc', jnp.tanh(jnp.einsum('bld,zdr->zblr', xw, w1)), w2)
    lw = -jnp.exp(-jax.nn.softplus(-w_pre) - 0.5)
    icl = jax.nn.sigmoid(a0[:, None, None, :] + jnp.einsum('zblr,zrc->zblc', jnp.einsum('bld,zdr->zblr', xa, a1), a2))
    gate = jax.nn.sigmoid(xg @ g1) @ g2
    kkf = k * k_k
    kkh = heads(kkf)
    kkh = kkh / jnp.maximum(jnp.linalg.norm(kkh, axis=-1, keepdims=True), 1e-12)
    kk = kkh.reshape(B_, L, D_A)
    k_dir = k[None] * (1.0 + (icl - 1.0) * k_a)
    y, s_fin = wkv7_scan_pallas(r, v, kk, lw, k_dir, icl, s0)
    y = heads(y)
    rh, vh = heads(r), heads(v)
    mu = jnp.mean(y, axis=-1, keepdims=True)
    var = jnp.mean(jnp.square(y - mu), axis=-1, keepdims=True)
    y = (y - mu) * lax.rsqrt(var + GN_EPS) * heads(gn_w) + heads(gn_b)
    bonus = jnp.sum(rh * heads(k) * r_k, axis=-1, keepdims=True) * vh
    out = (y + bonus).reshape(B_, L, D_A) * gate
    return out, s_fin


def fourier_mixer(u):
    B_, L, _ = u.shape
    ug = u.reshape(B_, L, B_GROUPS, B_GROUP_DIM)
    f = jnp.fft.fft2(ug, axes=(1, 3), norm='ortho').real
    return f.reshape(B_, L, D_B)


def conformer_conv(p, conv_w, conv_b, ln_g, ln_b):
    a, b = jnp.split(p, 2, axis=-1)
    x = a * jax.nn.sigmoid(b)
    pad = CONV_WIDTH // 2
    y = lax.conv_general_dilated(x, conv_w[:, None, :], window_strides=(1,),
                                 padding=[(pad, pad)], dimension_numbers=('NWC', 'WIO', 'NWC'),
                                 feature_group_count=D_C) + conv_b
    return jax.nn.silu(layernorm(y, ln_g, ln_b))


def chunk_gating(p, ln_g, ln_b, w_s, b_s):
    p = jax.nn.gelu(p)
    u, v = jnp.split(p, 2, axis=-1)
    v = layernorm(v, ln_g, ln_b)
    B_, L, _ = v.shape
    vc = v.reshape(B_, L // CHUNK, CHUNK, D_HEADS, D_HEAD_DIM)
    mixed = jnp.einsum('hij,bnjhc->bnihc', w_s, vc) + b_s.T[None, None, :, :, None]
    return u * mixed.reshape(B_, L, D_D)


def expert_choice_ffn(h, w_router, b_router, w_gate, w_up, w_down):
    B_, L, D = h.shape
    cap = EC_FACTOR * L // N_EXPERTS
    logits = h @ w_router + b_router
    aff = jax.nn.softmax(logits, axis=-1)
    vals, idx = lax.top_k(jnp.swapaxes(aff, 1, 2), cap)
    xs = jax.vmap(lambda hb, ib: hb[ib])(h, idx)
    hid = jax.nn.silu(jnp.einsum('becd,edf->becf', xs, w_gate)) * jnp.einsum('becd,edf->becf', xs, w_up)
    ys = jnp.einsum('becf,efd->becd', hid, w_down) * vals[..., None]
    return jax.vmap(lambda ib, yb: jnp.zeros((L, D), yb.dtype).at[ib.reshape(-1)].add(yb.reshape(-1, D)))(idx, ys)


def run_trunk(x, cvec, s_init, P):
    states = []
    for l in range(DEPTH):
        j = l // 2
        sh1, sc1, gt1, sh2, sc2, gt2 = modulation(cvec, P['mod_w'][l], P['mod_b'][l])
        h = rmsnorm(x, P['norm_mix'][l]) * (1.0 + sc1) + sh1
        if l % 2 == 0:
            proj = h @ P['ev_w_in'][j]
            r, k, v, u = jnp.split(proj, [D_A, 2 * D_A, 3 * D_A], axis=-1)
            o_a, s_fin = rwkv7_mixer(h, r, k, v, s_init[:, j], P['ev_mu_rkv'][j], P['ev_mu_wag'][j],
                                     P['ev_w0'][j], P['ev_w1'][j], P['ev_w2'][j],
                                     P['ev_a0'][j], P['ev_a1'][j], P['ev_a2'][j],
                                     P['ev_g1'][j], P['ev_g2'][j], P['ev_k_k'][j], P['ev_k_a'][j],
                                     P['ev_r_k'][j], P['ev_gn_w'][j], P['ev_gn_b'][j])
            o_b = fourier_mixer(u)
            mixed = jnp.concatenate([o_a, o_b], axis=-1) @ P['ev_w_out'][j]
            states.append(s_fin)
        else:
            proj = h @ P['od_w_in'][j]
            pc, pd = jnp.split(proj, [2 * D_C], axis=-1)
            o_c = conformer_conv(pc, P['od_conv_w'][j], P['od_conv_b'][j], P['od_cln_g'][j], P['od_cln_b'][j])
            o_d = chunk_gating(pd, P['od_vln_g'][j], P['od_vln_b'][j], P['od_w_s'][j], P['od_b_s'][j])
            mixed = jnp.concatenate([o_c, o_d], axis=-1) @ P['od_w_out'][j]
        x = x + gt1 * mixed
        h2 = rmsnorm(x, P['norm_ffn'][l]) * (1.0 + sc2) + sh2
        x = x + gt2 * expert_choice_ffn(h2, P['moe_router'][l], P['moe_router_b'][l],
                                        P['moe_w_gate'][l], P['moe_w_up'][l], P['moe_w_down'][l])
    return rmsnorm(x, P['final_norm']), jnp.stack(states, axis=1)


def kernel(x_prompt, x_sample, state_wkv, c, c_ctx, mod_w, mod_b, norm_mix, norm_ffn, final_norm,
           ev_w_in, ev_w_out, ev_mu_rkv, ev_mu_wag, ev_w0, ev_w1, ev_w2, ev_a0, ev_a1, ev_a2,
           ev_g1, ev_g2, ev_k_k, ev_k_a, ev_r_k, ev_gn_w, ev_gn_b,
           od_w_in, od_w_out, od_conv_w, od_conv_b, od_cln_g, od_cln_b, od_vln_g, od_vln_b,
           od_w_s, od_b_s, moe_router, moe_router_b, moe_w_gate, moe_w_up, moe_w_down):
    P = dict(mod_w=mod_w, mod_b=mod_b, norm_mix=norm_mix, norm_ffn=norm_ffn, final_norm=final_norm,
             ev_w_in=ev_w_in, ev_w_out=ev_w_out, ev_mu_rkv=ev_mu_rkv, ev_mu_wag=ev_mu_wag,
             ev_w0=ev_w0, ev_w1=ev_w1, ev_w2=ev_w2, ev_a0=ev_a0, ev_a1=ev_a1, ev_a2=ev_a2,
             ev_g1=ev_g1, ev_g2=ev_g2, ev_k_k=ev_k_k, ev_k_a=ev_k_a, ev_r_k=ev_r_k,
             ev_gn_w=ev_gn_w, ev_gn_b=ev_gn_b,
             od_w_in=od_w_in, od_w_out=od_w_out, od_conv_w=od_conv_w, od_conv_b=od_conv_b,
             od_cln_g=od_cln_g, od_cln_b=od_cln_b, od_vln_g=od_vln_g, od_vln_b=od_vln_b,
             od_w_s=od_w_s, od_b_s=od_b_s, moe_router=moe_router, moe_router_b=moe_router_b,
             moe_w_gate=moe_w_gate, moe_w_up=moe_w_up, moe_w_down=moe_w_down)
    n_even = state_wkv.shape[1]
    s_zero = jnp.zeros((x_prompt.shape[0], n_even, 2, A_HEADS, A_HEAD_DIM, A_HEAD_DIM), F32)
    y_prompt, new_state_wkv = run_trunk(x_prompt, c_ctx[None, :], s_zero, P)
    x_lat = x_sample + grid_posemb(x_sample.shape[1], x_sample.dtype)[None]
    y_sample, _ = run_trunk(x_lat, c, state_wkv, P)
    return (y_prompt, y_sample, new_state_wkv)
```

```python
import functools
import math

import jax
import jax.numpy as jnp
from jax import lax
from jax.experimental import pallas as pl
from jax.experimental.pallas import tpu as pltpu

D_MODEL = 1024
DEPTH = 2
GRID_W = 64
POS_BASE = 10000.0
A_HEADS = 12
A_HEAD_DIM = 64
D_A = A_HEADS * A_HEAD_DIM
B_GROUPS = 4
B_GROUP_DIM = 64
D_B = B_GROUPS * B_GROUP_DIM
D_C = 512
CONV_WIDTH = 31
D_HEADS = 4
D_HEAD_DIM = 128
D_D = D_HEADS * D_HEAD_DIM
CHUNK = 128
N_EXPERTS = 16
EC_FACTOR = 2
RMS_EPS = 1e-6
LN_EPS = 1e-5
GN_EPS = 64e-5

LANES = 128
SCAN_CHUNK = 64
HEADS_PER_TILE = LANES // A_HEAD_DIM
N_HEAD_PAIRS = A_HEADS // HEADS_PER_TILE
PAIRS_PER_STEP = 6

F32 = jnp.float32
BF16 = jnp.bfloat16


def _mm(a, b, dims):
    return lax.dot_general(a.astype(BF16), b.astype(BF16), (dims, ((), ())),
                           preferred_element_type=F32)


_NN = ((1,), (0,))
_NT = ((1,), (1,))
_TN = ((0,), (0,))


def _cumsum_rows(tri, x):
    x1 = x.astype(BF16)
    r1 = x - x1.astype(F32)
    x2 = r1.astype(BF16)
    x3 = (r1 - x2.astype(F32)).astype(BF16)
    dot = lambda y: lax.dot_general(tri, y, (_NN, ((), ())), preferred_element_type=F32)
    return dot(x1) + dot(x2) + dot(x3)


def _unit_triangular_inverses(mats, same16, same32):
    c = mats[0].shape[0]
    eye = (lax.broadcasted_iota(jnp.int32, (c, c), 0) == lax.broadcasted_iota(jnp.int32, (c, c), 1)).astype(F32)
    pws = [jnp.where(same16, -a, 0.0) for a in mats]
    ts = [eye + n for n in pws]
    for _ in range(3):
        pws = [_mm(pw, pw, _NN) for pw in pws]
        ts = [t + _mm(t, pw, _NN) for t, pw in zip(ts, pws)]
    in32 = jnp.logical_and(same32, jnp.logical_not(same16))
    tmp = [_mm(t, jnp.where(in32, a, 0.0), _NN) for t, a in zip(ts, mats)]
    ts = [t - _mm(x, t, _NN) for t, x in zip(ts, tmp)]
    tmp = [_mm(t, jnp.where(same32, 0.0, a), _NN) for t, a in zip(ts, mats)]
    ts = [t - _mm(x, t, _NN) for t, x in zip(ts, tmp)]
    return ts


def _wkv_chunks(items):
    c = items[0][0].shape[0]
    ti = lax.broadcasted_iota(jnp.int32, (c, c), 0)
    si = lax.broadcasted_iota(jnp.int32, (c, c), 1)
    ti2 = lax.broadcasted_iota(jnp.int32, (c, 2 * c), 0)
    si2 = lax.broadcasted_iota(jnp.int32, (c, 2 * c), 1) & (c - 1)
    masks = {False: (ti >= si, ti > si, ti2 >= si2), True: (ti <= si, ti < si, ti2 <= si2)}
    tri = {rev: masks[rev][0].astype(F32).astype(BF16) for rev in (False, True)}
    same16 = (ti >> 4) == (si >> 4)
    same32 = (ti >> 5) == (si >> 5)
    lane = lax.broadcasted_iota(jnp.int32, (1, LANES), 1)
    head_masks = [jnp.logical_and(lane >= h * A_HEAD_DIM, lane < (h + 1) * A_HEAD_DIM)
                  for h in range(HEADS_PER_TILE)]
    first_head = lane < A_HEAD_DIM
    head_shift = A_HEAD_DIM.bit_length() - 1
    block_diag = ((lax.broadcasted_iota(jnp.int32, (LANES, LANES), 0) >> head_shift)
                  == (lax.broadcasted_iota(jnp.int32, (LANES, LANES), 1) >> head_shift))

    cs_all = [_cumsum_rows(tri[rev], lw) for (_, _, _, lw, _, _, _, rev) in items]
    prep = []
    for (r, v, kk, lw, k, a, s, rev), cs in zip(items, cs_all):
        cs_end = cs[0:1] if rev else cs[c - 1:c]
        b = kk * a
        g_inv = jnp.exp(-cs)
        g_tail = jnp.exp(cs_end - cs)
        kt = kk * jnp.exp(cs - lw)
        rt = r * jnp.exp(cs)
        lhs = jnp.concatenate([kt, rt], axis=0)
        kb_tail = jnp.concatenate([k * g_tail, b * g_tail], axis=0)
        prep.append((lhs, kt, k * g_inv, b * g_inv, kb_tail, jnp.exp(cs_end)))
    ps = [_mm(pr[0], it[6], _NT) for pr, it in zip(prep, items)]
    gs, a_kbs = [], []
    for (lhs, kt, kh, bh, _, _), it in zip(prep, items):
        for m in head_masks:
            bh_m = jnp.where(m, bh, 0.0)
            gs.append(_mm(lhs, jnp.concatenate([jnp.where(m, kh, 0.0), bh_m], axis=0), _NT))
            a_kbs.append(jnp.where(masks[it[7]][1], _mm(kt, bh_m, _NT), 0.0))
    ts = _unit_triangular_inverses(a_kbs, same16, same32)
    ws = []
    for i, it in enumerate(items):
        for h in range(HEADS_PER_TILE):
            a_kk = jnp.where(masks[it[7]][1], gs[HEADS_PER_TILE * i + h][0:c, 0:c], 0.0)
            ws.append(ps[i][0:c] + _mm(a_kk, it[1], _NN))
    us = [_mm(t, w, _NN) for t, w in zip(ts, ws)]
    vus = [jnp.concatenate([it[1], -jnp.where(first_head, us[HEADS_PER_TILE * i], us[HEADS_PER_TILE * i + 1])], axis=0)
           for i, it in enumerate(items)]
    ys = []
    for i, it in enumerate(items):
        parts = [ps[i][c:2 * c] + _mm(jnp.where(masks[it[7]][2], gs[HEADS_PER_TILE * i + h][c:2 * c], 0.0), vus[i], _NN)
                 for h in range(HEADS_PER_TILE)]
        ys.append(jnp.where(first_head, parts[0], parts[1]))
    dss = [_mm(vu, pr[4], _TN) for vu, pr in zip(vus, prep)]
    return [(y, it[6] * pr[5] + jnp.where(block_diag, ds, 0.0))
            for y, it, pr, ds in zip(ys, items, prep, dss)]


def _wkv_kernel(rf, vf, kkf, lwf, kf, af, rb, vb, kkb, lwb, kb, ab, s0_ref,
                yf_ref, yb_ref, s_ref):
    @pl.when(pl.program_id(2) == 0)
    def _():
        s_ref[...] = s0_ref[...]

    items = []
    for p in range(PAIRS_PER_STEP):
        ln = slice(p * LANES, (p + 1) * LANES)
        items.append((rf[0, :, ln], vf[0, :, ln].astype(F32), kkf[0, :, ln], lwf[0, 0, :, ln], kf[0, 0, :, ln],
                      af[0, 0, :, ln], s_ref[0, 0, p], False))
        items.append((rb[0, :, ln], vb[0, :, ln].astype(F32), kkb[0, :, ln], lwb[0, 0, :, ln], kb[0, 0, :, ln],
                      ab[0, 0, :, ln], s_ref[0, 1, p], True))
    out = _wkv_chunks(items)
    for p in range(PAIRS_PER_STEP):
        ln = slice(p * LANES, (p + 1) * LANES)
        yf_ref[0, :, ln], s_ref[0, 0, p] = out[2 * p]
        yb_ref[0, :, ln], s_ref[0, 1, p] = out[2 * p + 1]


def wkv7_scan_pallas(r, v, kk, lw, k, a, s0):
    bsz, seq, _ = r.shape
    c = SCAN_CHUNK
    nc = seq // c
    n = A_HEAD_DIM
    s0p = s0.reshape(bsz, 2, N_HEAD_PAIRS, HEADS_PER_TILE, n, n)
    z = jnp.zeros_like(s0p[:, :, :, 0])
    s0bd = jnp.concatenate([jnp.concatenate([s0p[:, :, :, 0], z], axis=-1),
                            jnp.concatenate([z, s0p[:, :, :, 1]], axis=-1)], axis=-2)

    wd = PAIRS_PER_STEP * LANES
    shared_f = pl.BlockSpec((1, c, wd), lambda b, p, i: (b, i, p))
    shared_b = pl.BlockSpec((1, c, wd), lambda b, p, i: (b, nc - 1 - i, p))
    dir_f = pl.BlockSpec((1, 1, c, wd), lambda b, p, i: (0, b, i, p))
    dir_b = pl.BlockSpec((1, 1, c, wd), lambda b, p, i: (1, b, nc - 1 - i, p))
    st = pl.BlockSpec((1, 2, PAIRS_PER_STEP, LANES, LANES), lambda b, p, i: (b, 0, p, 0, 0))
    yf, yb, sfin = pl.pallas_call(
        _wkv_kernel,
        grid=(bsz, N_HEAD_PAIRS // PAIRS_PER_STEP, nc),
        in_specs=[shared_f, shared_f, shared_f, dir_f, dir_f, dir_f,
                  shared_b, shared_b, shared_b, dir_b, dir_b, dir_b, st],
        out_specs=[shared_f, shared_b, st],
        out_shape=[jax.ShapeDtypeStruct((bsz, seq, D_A), F32),
                   jax.ShapeDtypeStruct((bsz, seq, D_A), F32),
                   jax.ShapeDtypeStruct((bsz, 2, N_HEAD_PAIRS, LANES, LANES), F32)],
        compiler_params=pltpu.CompilerParams(
            dimension_semantics=("arbitrary", "arbitrary", "arbitrary")),
        name="wkv7_scan",
    )(r, v, kk, lw, k, a, r, v, kk, lw, k, a, s0bd)
    s_fin = jnp.stack([sfin[..., :n, :n], sfin[..., n:, n:]], axis=3)
    return yf, yb, s_fin.reshape(bsz, 2, A_HEADS, n, n)


def rmsnorm(x, g):
    y = x * lax.rsqrt(jnp.mean(x * x, axis=-1, keepdims=True) + RMS_EPS)
    return y * g


def modulation(cvec, w_mod, b_mod):
    return jax.nn.silu(cvec) @ w_mod + b_mod


def grid_posemb(n_tokens, dtype):
    rows = n_tokens // GRID_W
    row = jnp.repeat(jnp.arange(rows, dtype=F32), GRID_W)
    col = jnp.tile(jnp.arange(GRID_W, dtype=F32), rows)
    quarter = D_MODEL // 4
    freq = 1.0 / (POS_BASE ** (jnp.arange(quarter, dtype=F32) / quarter))
    ar = row[:, None] * freq[None, :]
    ac = col[:, None] * freq[None, :]
    return jnp.concatenate([jnp.sin(ar), jnp.cos(ar), jnp.sin(ac), jnp.cos(ac)], axis=-1).astype(dtype)


def _expert_kernel(idx_ref, val_ref, h_ref, wg_ref, wu_ref, wd_ref, y_ref, wg_s, wu_s, wd_s):
    @pl.when(pl.program_id(1) == 0)
    def _():
        wg_s[...] = wg_ref[0].astype(BF16)
        wu_s[...] = wu_ref[0].astype(BF16)
        wd_s[...] = wd_ref[0].astype(BF16)

    cap = idx_ref.shape[2]
    seq = h_ref.shape[1]
    onehot = jnp.where(lax.broadcasted_iota(jnp.int32, (cap, seq), 1) == idx_ref[0, 0], 1.0, 0.0).astype(BF16)
    xs = jnp.dot(onehot, h_ref[0], preferred_element_type=F32).astype(BF16)
    g = jnp.dot(xs, wg_s[...], preferred_element_type=F32)
    u = jnp.dot(xs, wu_s[...], preferred_element_type=F32)
    hid = (g * jax.nn.sigmoid(g) * u).astype(BF16)
    y = jnp.dot(hid, wd_s[...], preferred_element_type=F32) * val_ref[0, 0]
    y_ref[0, 0] = y.astype(y_ref.dtype)


def _combine_kernel(idx_ref, ys_ref, x_ref, gate_ref, o_ref):
    rows = x_ref.shape[1]
    slots = idx_ref.shape[2]
    tok = pl.program_id(1) * rows + lax.broadcasted_iota(jnp.int32, (rows, slots), 0)
    onehot = jnp.where(tok == idx_ref[0], 1.0, 0.0).astype(BF16)
    moe = jnp.dot(onehot, ys_ref[0], preferred_element_type=F32)
    o_ref[0] = x_ref[0] + gate_ref[0] * moe


MOE_VMEM_BYTES = 52 * 1024 * 1024
COMBINE_ROWS = 512


def expert_choice_ffn(x, gate, h, aff, w_gate, w_up, w_down):
    bsz, seq, d = h.shape
    n_exp, _, f = w_gate.shape
    cap = EC_FACTOR * seq // n_exp
    vals, idx = lax.top_k(jnp.swapaxes(aff, 1, 2), cap)
    ys = pl.pallas_call(
        _expert_kernel,
        grid=(n_exp, bsz),
        in_specs=[pl.BlockSpec((1, 1, cap, 1), lambda e, b: (b, e, 0, 0)),
                  pl.BlockSpec((1, 1, cap, 1), lambda e, b: (b, e, 0, 0)),
                  pl.BlockSpec((1, seq, d), lambda e, b: (b, 0, 0)),
                  pl.BlockSpec((1, d, f), lambda e, b: (e, 0, 0)),
                  pl.BlockSpec((1, d, f), lambda e, b: (e, 0, 0)),
                  pl.BlockSpec((1, f, d), lambda e, b: (e, 0, 0))],
        out_specs=pl.BlockSpec((1, 1, cap, d), lambda e, b: (b, e, 0, 0)),
        out_shape=jax.ShapeDtypeStruct((bsz, n_exp, cap, d), BF16),
        scratch_shapes=[pltpu.VMEM((d, f), BF16), pltpu.VMEM((d, f), BF16), pltpu.VMEM((f, d), BF16)],
        compiler_params=pltpu.CompilerParams(dimension_semantics=("arbitrary", "arbitrary"),
                                             vmem_limit_bytes=MOE_VMEM_BYTES),
        name="moe_experts",
    )(idx[..., None], vals[..., None], h.astype(BF16), w_gate, w_up, w_down)
    rows = min(COMBINE_ROWS, seq)
    slots = n_exp * cap
    return pl.pallas_call(
        _combine_kernel,
        grid=(bsz, seq // rows),
        in_specs=[pl.BlockSpec((1, 1, slots), lambda b, i: (b, 0, 0)),
                  pl.BlockSpec((1, slots, d), lambda b, i: (b, 0, 0)),
                  pl.BlockSpec((1, rows, d), lambda b, i: (b, i, 0)),
                  pl.BlockSpec((1, 1, d), lambda b, i: (b, 0, 0))],
        out_specs=pl.BlockSpec((1, rows, d), lambda b, i: (b, i, 0)),
        out_shape=jax.ShapeDtypeStruct((bsz, seq, d), F32),
        compiler_params=pltpu.CompilerParams(dimension_semantics=("arbitrary", "arbitrary"),
                                             vmem_limit_bytes=MOE_VMEM_BYTES),
        name="moe_combine",
    )(idx.reshape(bsz, 1, slots), ys.reshape(bsz, slots, d), x, gate)


TOKEN_TILE = 512
HALO = 16
LAYER_VMEM_BYTES = 48 * 1024 * 1024


def _rms(x, g):
    return x * lax.rsqrt(jnp.mean(x * x, axis=-1, keepdims=True) + RMS_EPS) * g


def _ln(x, g, b):
    mu = jnp.mean(x, axis=-1, keepdims=True)
    xc = x - mu
    return xc * lax.rsqrt(jnp.mean(xc * xc, axis=-1, keepdims=True) + LN_EPS) * g + b


def _split_dot(x, w_hi, w_lo):
    x_hi = x.astype(BF16)
    x_lo = (x - x_hi.astype(F32)).astype(BF16)
    dot = lambda a, b: jnp.dot(a, b, preferred_element_type=F32)
    return dot(x_hi, w_hi) + dot(x_lo, w_hi) + dot(x_hi, w_lo)


def _with_halo(x_ref, prev_ref, next_ref):
    return jnp.concatenate([prev_ref[0], x_ref[0], next_ref[0]], axis=0)


def _inside_mask(rows):
    i, n = pl.program_id(1), pl.num_programs(1)
    r = lax.broadcasted_iota(jnp.int32, (rows + 2 * HALO, 1), 0)
    return jnp.logical_and(jnp.logical_or(i > 0, r >= HALO), jnp.logical_or(i < n - 1, r < rows + HALO))


def _router(x1, mod_ref, nf_ref, wr_hi_ref, wr_lo_ref, br_ref, h2_ref, aff_ref):
    h2 = _rms(x1, nf_ref[...]) * (1.0 + mod_ref[0, 4:5]) + mod_ref[0, 3:4]
    h2_ref[0] = h2.astype(h2_ref.dtype)
    logits = _split_dot(h2, wr_hi_ref[...], wr_lo_ref[...]) + br_ref[...]
    e = jnp.exp(logits - jnp.max(logits, axis=-1, keepdims=True))
    aff_ref[0] = e / jnp.sum(e, axis=-1, keepdims=True)


def _odd_kernel(x_ref, xp_ref, xn_ref, mod_ref, nm_ref, nf_ref, win_ref, cw_ref, cb_ref, cg_ref, cbb_ref,
                vg_ref, vb_ref, ws_ref, bs_ref, wout_ref, wr_hi_ref, wr_lo_ref, br_ref,
                x1_ref, h2_ref, aff_ref, glu_s):
    rows = x_ref.shape[1]
    inside = _inside_mask(rows)
    xa = _with_halo(x_ref, xp_ref, xn_ref)
    h = _rms(xa, nm_ref[...]) * (1.0 + mod_ref[0, 1:2]) + mod_ref[0, 0:1]
    h = jnp.where(inside, h, 0.0).astype(BF16)
    pc = jnp.dot(h, win_ref[:, 0:2 * D_C], preferred_element_type=F32)
    glu_s[...] = pc[:, 0:D_C] * jax.nn.sigmoid(pc[:, D_C:2 * D_C])
    acc = jnp.zeros((rows, D_C), F32)
    for j in range(CONV_WIDTH):
        off = HALO - CONV_WIDTH // 2 + j
        acc = acc + cw_ref[j:j + 1, :] * glu_s[off:off + rows, :]
    o_c = _ln(acc + cb_ref[...], cg_ref[...], cbb_ref[...])
    o_c = o_c * jax.nn.sigmoid(o_c)
    pd = jax.nn.gelu(jnp.dot(h[HALO:HALO + rows], win_ref[:, 2 * D_C:], preferred_element_type=F32))
    u = pd[:, 0:D_D]
    v = _ln(pd[:, D_D:], vg_ref[...], vb_ref[...]).astype(BF16)
    chunks = []
    for ck in range(rows // CHUNK):
        vc = v[ck * CHUNK:(ck + 1) * CHUNK]
        chunks.append(jnp.concatenate(
            [jnp.dot(ws_ref[hd], vc[:, hd * D_HEAD_DIM:(hd + 1) * D_HEAD_DIM], preferred_element_type=F32)
             for hd in range(D_HEADS)], axis=1) + bs_ref[...])
    o_d = u * jnp.concatenate(chunks, axis=0)
    mixed = jnp.dot(jnp.concatenate([o_c, o_d], axis=1).astype(BF16), wout_ref[...], preferred_element_type=F32)
    x1 = x_ref[0] + mod_ref[0, 2:3] * mixed
    x1_ref[0] = x1
    _router(x1, mod_ref, nf_ref, wr_hi_ref, wr_lo_ref, br_ref, h2_ref, aff_ref)


def _row(v):
    return v.reshape(1, -1)


def _hi_lo(w):
    hi = w.astype(BF16)
    return hi, (w - hi.astype(F32)).astype(BF16)


def _tile_specs(rows, seq, d):
    per = rows // HALO
    last = seq // HALO - 1
    return [pl.BlockSpec((1, rows, d), lambda b, i: (b, i, 0)),
            pl.BlockSpec((1, HALO, d), lambda b, i: (b, jnp.maximum(i * per - 1, 0), 0)),
            pl.BlockSpec((1, HALO, d), lambda b, i: (b, jnp.minimum((i + 1) * per, last), 0))]


def _full(a):
    return pl.BlockSpec(a.shape, lambda b, i: (0,) * a.ndim)


def odd_layer(x, mod, norm_mix, norm_ffn, w_in, conv_w, conv_b, cln_g, cln_b, vln_g, vln_b, w_s, b_s, w_out,
              w_router, b_router):
    bsz, seq, d = x.shape
    rows = min(TOKEN_TILE, seq)
    n_exp = w_router.shape[1]
    wr_hi, wr_lo = _hi_lo(w_router)
    bs_full = jnp.repeat(b_s.T, D_HEAD_DIM, axis=1)
    consts = [_row(norm_mix), _row(norm_ffn), w_in.astype(BF16), conv_w, _row(conv_b), _row(cln_g), _row(cln_b),
              _row(vln_g), _row(vln_b), w_s.astype(BF16), bs_full, w_out.astype(BF16), wr_hi, wr_lo,
              _row(b_router)]
    tile = lambda w: pl.BlockSpec((1, rows, w), lambda b, i: (b, i, 0))
    return pl.pallas_call(
        _odd_kernel,
        grid=(bsz, seq // rows),
        in_specs=_tile_specs(rows, seq, d) + [pl.BlockSpec((1, 6, d), lambda b, i: (b, 0, 0))]
        + [_full(a) for a in consts],
        out_specs=[tile(d), tile(d), tile(n_exp)],
        out_shape=[jax.ShapeDtypeStruct((bsz, seq, d), F32), jax.ShapeDtypeStruct((bsz, seq, d), BF16),
                   jax.ShapeDtypeStruct((bsz, seq, n_exp), F32)],
        scratch_shapes=[pltpu.VMEM((rows + 2 * HALO, D_C), F32)],
        compiler_params=pltpu.CompilerParams(dimension_semantics=("arbitrary", "arbitrary"),
                                             vmem_limit_bytes=LAYER_VMEM_BYTES),
        name="odd_layer",
    )(x, x, x, mod, *consts)


EVEN_TILE = 256
DECAY_SCALE = math.exp(-0.5)


def _head_sums(x, ones_bd):
    hi = x.astype(BF16)
    lo = (x - hi.astype(F32)).astype(BF16)
    dot = lambda a: jnp.dot(a, ones_bd, preferred_element_type=F32)
    return jnp.concatenate([dot(hi[:, g * LANES:(g + 1) * LANES]) + dot(lo[:, g * LANES:(g + 1) * LANES])
                            for g in range(x.shape[1] // LANES)], axis=1)


def _head_ones():
    head_shift = A_HEAD_DIM.bit_length() - 1
    same = ((lax.broadcasted_iota(jnp.int32, (LANES, LANES), 0) >> head_shift)
            == (lax.broadcasted_iota(jnp.int32, (LANES, LANES), 1) >> head_shift))
    return jnp.where(same, 1.0, 0.0).astype(BF16)


def _even_pre_kernel(x_ref, xp_ref, xn_ref, mod_ref, nm_ref, win_ref, mu_rkv_ref, mu_wag_ref,
                     w1_ref, w2_ref, w0_ref, a1_ref, a2_ref, a0_ref, g1_ref, g2_ref, kk_ref, ka_ref, rk_ref,
                     r_out, v_out, kk_out, lw_out, k_out, a_out, gate_out, bonus_out, u_out, proj_s, h_s):
    rows = x_ref.shape[1]
    inside = _inside_mask(rows)
    xa = _with_halo(x_ref, xp_ref, xn_ref)
    h = _rms(xa, nm_ref[...]) * (1.0 + mod_ref[0, 1:2]) + mod_ref[0, 0:1]
    h = jnp.where(inside, h, 0.0)
    h_s[...] = h
    proj = jnp.dot(h.astype(BF16), win_ref[...], preferred_element_type=F32)
    proj_s[...] = proj[:, 0:3 * D_A]
    u_out[0] = proj[HALO:HALO + rows, 3 * D_A:].astype(u_out.dtype)

    def shifted(ref, lo, hi):
        cur = ref[HALO:HALO + rows, lo:hi]
        return cur, 0.5 * (ref[HALO - 1:HALO - 1 + rows, lo:hi] + ref[HALO + 1:HALO + 1 + rows, lo:hi]) - cur

    r, dr = shifted(proj_s, 0, D_A)
    k, dk = shifted(proj_s, D_A, 2 * D_A)
    v, dv = shifted(proj_s, 2 * D_A, 3 * D_A)
    r = r + dr * mu_rkv_ref[0:1]
    k = k + dk * mu_rkv_ref[1:2]
    v = v + dv * mu_rkv_ref[2:3]
    hc, dh = shifted(h_s, 0, D_MODEL)
    xw = (hc + dh * mu_wag_ref[0:1]).astype(BF16)
    xa_ = (hc + dh * mu_wag_ref[1:2]).astype(BF16)
    xg = (hc + dh * mu_wag_ref[2:3]).astype(BF16)
    dot = lambda a, b: jnp.dot(a, b, preferred_element_type=F32)
    w_pre = w0_ref[...] + dot(jnp.tanh(dot(xw, w1_ref[...])).astype(BF16), w2_ref[...])
    icl = jax.nn.sigmoid(a0_ref[...] + dot(dot(xa_, a1_ref[...]).astype(BF16), a2_ref[...]))
    gate_out[0] = dot(jax.nn.sigmoid(dot(xg, g1_ref[...])).astype(BF16), g2_ref[...])
    ones_bd = _head_ones()
    kk = k * kk_ref[...]
    kk = kk / jnp.maximum(jnp.sqrt(_head_sums(kk * kk, ones_bd)), 1e-12)
    r_out[0] = r
    v_out[0] = v.astype(v_out.dtype)
    kk_out[0] = kk
    bonus_out[0] = _head_sums(r * k * rk_ref[...], ones_bd) * v
    for z in range(2):
        a_z = icl[:, z * D_A:(z + 1) * D_A]
        lw_out[z, 0] = -DECAY_SCALE * jax.nn.sigmoid(w_pre[:, z * D_A:(z + 1) * D_A])
        a_out[z, 0] = a_z
        k_out[z, 0] = k * (1.0 + (a_z - 1.0) * ka_ref[...])


def _block_diag2(w):
    z = jnp.zeros_like(w[0])
    return jnp.concatenate([jnp.concatenate([w[0], z], axis=1), jnp.concatenate([z, w[1]], axis=1)], axis=0)


def even_pre(x, mod, norm_mix, w_in, mu_rkv, mu_wag, w0, w1, w2, a0, a1, a2, g1, g2, k_k, k_a, r_k):
    bsz, seq, d = x.shape
    rows = min(EVEN_TILE, seq)
    cat = lambda w: jnp.concatenate([w[0], w[1]], axis=1)
    consts = [_row(norm_mix), w_in.astype(BF16), mu_rkv, mu_wag,
              cat(w1).astype(BF16), _block_diag2(w2).astype(BF16), _row(w0),
              cat(a1).astype(BF16), _block_diag2(a2).astype(BF16), _row(a0),
              g1.astype(BF16), g2.astype(BF16), _row(k_k), _row(k_a), _row(r_k)]
    tile = lambda w: pl.BlockSpec((1, rows, w), lambda b, i: (b, i, 0))
    tile2 = pl.BlockSpec((2, 1, rows, D_A), lambda b, i: (0, b, i, 0))
    sds = lambda w, dt=F32: jax.ShapeDtypeStruct((bsz, seq, w), dt)
    sds2 = jax.ShapeDtypeStruct((2, bsz, seq, D_A), F32)
    return pl.pallas_call(
        _even_pre_kernel,
        grid=(bsz, seq // rows),
        in_specs=_tile_specs(rows, seq, d) + [pl.BlockSpec((1, 6, d), lambda b, i: (b, 0, 0))]
        + [_full(a) for a in consts],
        out_specs=[tile(D_A), tile(D_A), tile(D_A), tile2, tile2, tile2, tile(D_A), tile(D_A), tile(D_B)],
        out_shape=[sds(D_A), sds(D_A, BF16), sds(D_A), sds2, sds2, sds2, sds(D_A), sds(D_A), sds(D_B, BF16)],
        scratch_shapes=[pltpu.VMEM((rows + 2 * HALO, 3 * D_A), F32), pltpu.VMEM((rows + 2 * HALO, d), F32)],
        compiler_params=pltpu.CompilerParams(dimension_semantics=("arbitrary", "arbitrary"),
                                             vmem_limit_bytes=LAYER_VMEM_BYTES),
        name="even_pre",
    )(x, x, x, mod, *consts)


FOURIER_ROWS = 512


def _fourier_kernel(u_ref, f64_ref, fl_ref, o_ref, ucs_s):
    seq = u_ref.shape[1]

    @pl.when(pl.program_id(1) == 0)
    def _():
        ucs = jnp.dot(u_ref[0], f64_ref[...], preferred_element_type=F32)
        ucs_s[0:seq] = ucs[:, 0:D_B].astype(BF16)
        ucs_s[seq:2 * seq] = ucs[:, D_B:].astype(BF16)

    scale = 1.0 / math.sqrt(seq * B_GROUP_DIM)
    o_ref[0] = (jnp.dot(fl_ref[...], ucs_s[...], preferred_element_type=F32) * scale).astype(o_ref.dtype)


def _dft_tables(seq):
    def cs(n):
        i = jnp.arange(n, dtype=jnp.int32)
        ang = ((i[:, None] * i[None, :]) % n).astype(F32) * (2.0 * math.pi / n)
        return jnp.cos(ang), jnp.sin(ang)
    c64, s64 = cs(B_GROUP_DIM)
    eye = jnp.eye(B_GROUPS, dtype=F32)
    f64 = jnp.concatenate([jnp.kron(eye, c64), jnp.kron(eye, s64)], axis=1)
    cl, sl = cs(seq)
    return f64.astype(BF16), jnp.concatenate([cl, -sl], axis=1).astype(BF16)


def fourier_mixer(u):
    bsz, seq, _ = u.shape
    rows = min(FOURIER_ROWS, seq)
    f64, fl = _dft_tables(seq)
    return pl.pallas_call(
        _fourier_kernel,
        grid=(bsz, seq // rows),
        in_specs=[pl.BlockSpec((1, seq, D_B), lambda b, i: (b, 0, 0)),
                  pl.BlockSpec(f64.shape, lambda b, i: (0, 0)),
                  pl.BlockSpec((rows, 2 * seq), lambda b, i: (i, 0))],
        out_specs=pl.BlockSpec((1, rows, D_B), lambda b, i: (b, i, 0)),
        out_shape=jax.ShapeDtypeStruct((bsz, seq, D_B), BF16),
        scratch_shapes=[pltpu.VMEM((2 * seq, D_B), BF16)],
        compiler_params=pltpu.CompilerParams(dimension_semantics=("arbitrary", "arbitrary"),
                                             vmem_limit_bytes=LAYER_VMEM_BYTES),
        name="fourier_mixer",
    )(u, f64, fl)


def _even_post_kernel(yf_ref, yb_ref, bonus_ref, gate_ref, ob_ref, x_ref, mod_ref, gnw_ref, gnb_ref, wout_ref,
                      nf_ref, wr_hi_ref, wr_lo_ref, br_ref, x1_ref, h2_ref, aff_ref):
    ones_bd = _head_ones()
    y = yf_ref[0] + yb_ref[0]
    mu = _head_sums(y, ones_bd) * (1.0 / A_HEAD_DIM)
    yc = y - mu
    var = _head_sums(yc * yc, ones_bd) * (1.0 / A_HEAD_DIM)
    o_a = (yc * lax.rsqrt(var + GN_EPS) * gnw_ref[...] + gnb_ref[...] + bonus_ref[0]) * gate_ref[0]
    mixed = (jnp.dot(o_a.astype(BF16), wout_ref[0:D_A], preferred_element_type=F32)
             + jnp.dot(ob_ref[0], wout_ref[D_A:], preferred_element_type=F32))
    x1 = x_ref[0] + mod_ref[0, 2:3] * mixed
    x1_ref[0] = x1
    _router(x1, mod_ref, nf_ref, wr_hi_ref, wr_lo_ref, br_ref, h2_ref, aff_ref)


def even_post(yf, yb, bonus, gate, o_b, x, mod, gn_w, gn_b, w_out, norm_ffn, w_router, b_router):
    bsz, seq, d = x.shape
    rows = min(TOKEN_TILE, seq)
    n_exp = w_router.shape[1]
    wr_hi, wr_lo = _hi_lo(w_router)
    consts = [_row(gn_w), _row(gn_b), w_out.astype(BF16), _row(norm_ffn), wr_hi, wr_lo, _row(b_router)]
    tile = lambda w: pl.BlockSpec((1, rows, w), lambda b, i: (b, i, 0))
    return pl.pallas_call(
        _even_post_kernel,
        grid=(bsz, seq // rows),
        in_specs=[tile(D_A), tile(D_A), tile(D_A), tile(D_A), tile(D_B), tile(d),
                  pl.BlockSpec((1, 6, d), lambda b, i: (b, 0, 0))] + [_full(a) for a in consts],
        out_specs=[tile(d), tile(d), tile(n_exp)],
        out_shape=[jax.ShapeDtypeStruct((bsz, seq, d), F32), jax.ShapeDtypeStruct((bsz, seq, d), BF16),
                   jax.ShapeDtypeStruct((bsz, seq, n_exp), F32)],
        compiler_params=pltpu.CompilerParams(dimension_semantics=("arbitrary", "arbitrary"),
                                             vmem_limit_bytes=LAYER_VMEM_BYTES),
        name="even_post",
    )(yf, yb, bonus, gate, o_b, x, mod, *consts)


def run_trunk(x, cvec, s_init, P):
    states = []
    for l in range(DEPTH):
        j = l // 2
        mod = modulation(cvec, P['mod_w'][l], P['mod_b'][l])
        mod = jnp.broadcast_to(mod.reshape(-1, 6, x.shape[2]), (x.shape[0], 6, x.shape[2]))
        if l % 2 == 0:
            r, v, kk, lw, kd, a, gate, bonus, u = even_pre(
                x, mod, P['norm_mix'][l], P['ev_w_in'][j], P['ev_mu_rkv'][j], P['ev_mu_wag'][j],
                P['ev_w0'][j], P['ev_w1'][j], P['ev_w2'][j], P['ev_a0'][j], P['ev_a1'][j], P['ev_a2'][j],
                P['ev_g1'][j], P['ev_g2'][j], P['ev_k_k'][j], P['ev_k_a'][j], P['ev_r_k'][j])
            yf, yb, s_fin = wkv7_scan_pallas(r, v, kk, lw, kd, a, s_init[:, j])
            states.append(s_fin)
            x1, h2, aff = even_post(yf, yb, bonus, gate, fourier_mixer(u), x, mod, P['ev_gn_w'][j], P['ev_gn_b'][j],
                                    P['ev_w_out'][j], P['norm_ffn'][l], P['moe_router'][l], P['moe_router_b'][l])
        else:
            x1, h2, aff = odd_layer(x, mod, P['norm_mix'][l], P['norm_ffn'][l], P['od_w_in'][j], P['od_conv_w'][j],
                                    P['od_conv_b'][j], P['od_cln_g'][j], P['od_cln_b'][j], P['od_vln_g'][j],
                                    P['od_vln_b'][j], P['od_w_s'][j], P['od_b_s'][j], P['od_w_out'][j],
                                    P['moe_router'][l], P['moe_router_b'][l])
        x = expert_choice_ffn(x1, mod[:, 5:6], h2, aff, P['moe_w_gate'][l], P['moe_w_up'][l], P['moe_w_down'][l])
    return rmsnorm(x, P['final_norm']), jnp.stack(states, axis=1)


def kernel(x_prompt, x_sample, state_wkv, c, c_ctx, mod_w, mod_b, norm_mix, norm_ffn, final_norm,
           ev_w_in, ev_w_out, ev_mu_rkv, ev_mu_wag, ev_w0, ev_w1, ev_w2, ev_a0, ev_a1, ev_a2,
           ev_g1, ev_g2, ev_k_k, ev_k_a, ev_r_k, ev_gn_w, ev_gn_b,
           od_w_in, od_w_out, od_conv_w, od_conv_b, od_cln_g, od_cln_b, od_vln_g, od_vln_b,
           od_w_s, od_b_s, moe_router, moe_router_b, moe_w_gate, moe_w_up, moe_w_down):
    P = dict(mod_w=mod_w, mod_b=mod_b, norm_mix=norm_mix, norm_ffn=norm_ffn, final_norm=final_norm,
             ev_w_in=ev_w_in, ev_w_out=ev_w_out, ev_mu_rkv=ev_mu_rkv, ev_mu_wag=ev_mu_wag,
             ev_w0=ev_w0, ev_w1=ev_w1, ev_w2=ev_w2, ev_a0=ev_a0, ev_a1=ev_a1, ev_a2=ev_a2,
             ev_g1=ev_g1, ev_g2=ev_g2, ev_k_k=ev_k_k, ev_k_a=ev_k_a, ev_r_k=ev_r_k,
             ev_gn_w=ev_gn_w, ev_gn_b=ev_gn_b,
             od_w_in=od_w_in, od_w_out=od_w_out, od_conv_w=od_conv_w, od_conv_b=od_conv_b,
             od_cln_g=od_cln_g, od_cln_b=od_cln_b, od_vln_g=od_vln_g, od_vln_b=od_vln_b,
             od_w_s=od_w_s, od_b_s=od_b_s, moe_router=moe_router, moe_router_b=moe_router_b,
             moe_w_gate=moe_w_gate, moe_w_up=moe_w_up, moe_w_down=moe_w_down)
    n_even = state_wkv.shape[1]
    s_zero = jnp.zeros((x_prompt.shape[0], n_even, 2, A_HEADS, A_HEAD_DIM, A_HEAD_DIM), F32)
    y_prompt, new_state_wkv = run_trunk(x_prompt, c_ctx[None, :], s_zero, P)
    x_lat = x_sample + grid_posemb(x_sample.shape[1], x_sample.dtype)[None]
    y_sample, _ = run_trunk(x_lat, c, state_wkv, P)
    return (y_prompt, y_sample, new_state_wkv)
```

```python
import functools
import math

import jax
import jax.numpy as jnp
from jax import lax
from jax.experimental import pallas as pl
from jax.experimental.pallas import tpu as pltpu

D_MODEL = 1024
DEPTH = 2
GRID_W = 64
POS_BASE = 10000.0
A_HEADS = 12
A_HEAD_DIM = 64
D_A = A_HEADS * A_HEAD_DIM
B_GROUPS = 4
B_GROUP_DIM = 64
D_B = B_GROUPS * B_GROUP_DIM
D_C = 512
CONV_WIDTH = 31
D_HEADS = 4
D_HEAD_DIM = 128
D_D = D_HEADS * D_HEAD_DIM
CHUNK = 128
N_EXPERTS = 16
EC_FACTOR = 2
RMS_EPS = 1e-6
LN_EPS = 1e-5
GN_EPS = 64e-5

LANES = 128
SUBLANES = 8
SCAN_CHUNK = 64
HEADS_PER_TILE = LANES // A_HEAD_DIM
N_HEAD_PAIRS = A_HEADS // HEADS_PER_TILE
PAIRS_PER_STEP = 6

F32 = jnp.float32
BF16 = jnp.bfloat16


def _mm(a, b, dims):
    return lax.dot_general(a.astype(BF16), b.astype(BF16), (dims, ((), ())),
                           preferred_element_type=F32)


_NN = ((1,), (0,))
_NT = ((1,), (1,))
_TN = ((0,), (0,))


def _cumsum_rows(tri, x):
    x1 = x.astype(BF16)
    r1 = x - x1.astype(F32)
    x2 = r1.astype(BF16)
    x3 = (r1 - x2.astype(F32)).astype(BF16)
    dot = lambda y: lax.dot_general(tri, y, (_NN, ((), ())), preferred_element_type=F32)
    return dot(x1) + dot(x2) + dot(x3)


def _unit_triangular_inverses(mats, same16, same32):
    c = mats[0].shape[0]
    eye = (lax.broadcasted_iota(jnp.int32, (c, c), 0) == lax.broadcasted_iota(jnp.int32, (c, c), 1)).astype(F32)
    pws = [jnp.where(same16, -a, 0.0) for a in mats]
    ts = [eye + n for n in pws]
    for _ in range(3):
        pws = [_mm(pw, pw, _NN) for pw in pws]
        ts = [t + _mm(t, pw, _NN) for t, pw in zip(ts, pws)]
    in32 = jnp.logical_and(same32, jnp.logical_not(same16))
    tmp = [_mm(t, jnp.where(in32, a, 0.0), _NN) for t, a in zip(ts, mats)]
    ts = [t - _mm(x, t, _NN) for t, x in zip(ts, tmp)]
    tmp = [_mm(t, jnp.where(same32, 0.0, a), _NN) for t, a in zip(ts, mats)]
    ts = [t - _mm(x, t, _NN) for t, x in zip(ts, tmp)]
    return ts


def _wkv_chunks(items):
    c = items[0][0].shape[0]
    ti = lax.broadcasted_iota(jnp.int32, (c, c), 0)
    si = lax.broadcasted_iota(jnp.int32, (c, c), 1)
    ti2 = lax.broadcasted_iota(jnp.int32, (c, 2 * c), 0)
    si2 = lax.broadcasted_iota(jnp.int32, (c, 2 * c), 1) & (c - 1)
    masks = {False: (ti >= si, ti > si, ti2 >= si2), True: (ti <= si, ti < si, ti2 <= si2)}
    tri = {rev: masks[rev][0].astype(F32).astype(BF16) for rev in (False, True)}
    same16 = (ti >> 4) == (si >> 4)
    same32 = (ti >> 5) == (si >> 5)
    lane = lax.broadcasted_iota(jnp.int32, (1, LANES), 1)
    head_masks = [jnp.logical_and(lane >= h * A_HEAD_DIM, lane < (h + 1) * A_HEAD_DIM)
                  for h in range(HEADS_PER_TILE)]
    first_head = lane < A_HEAD_DIM
    head_shift = A_HEAD_DIM.bit_length() - 1
    block_diag = ((lax.broadcasted_iota(jnp.int32, (LANES, LANES), 0) >> head_shift)
                  == (lax.broadcasted_iota(jnp.int32, (LANES, LANES), 1) >> head_shift))

    cs_all = [_cumsum_rows(tri[rev], lw) for (_, _, _, lw, _, _, _, rev) in items]
    prep = []
    for (r, v, kk, lw, k, a, s, rev), cs in zip(items, cs_all):
        cs_end = cs[0:1] if rev else cs[c - 1:c]
        b = kk * a
        g_inv = jnp.exp(-cs)
        g_tail = jnp.exp(cs_end - cs)
        kt = kk * jnp.exp(cs - lw)
        rt = r * jnp.exp(cs)
        lhs = jnp.concatenate([kt, rt], axis=0)
        kb_tail = jnp.concatenate([k * g_tail, b * g_tail], axis=0)
        prep.append((lhs, kt, k * g_inv, b * g_inv, kb_tail, jnp.exp(cs_end)))
    ps = [_mm(pr[0], it[6], _NT) for pr, it in zip(prep, items)]
    gs, a_kbs = [], []
    for (lhs, kt, kh, bh, _, _), it in zip(prep, items):
        for m in head_masks:
            bh_m = jnp.where(m, bh, 0.0)
            gs.append(_mm(lhs, jnp.concatenate([jnp.where(m, kh, 0.0), bh_m], axis=0), _NT))
            a_kbs.append(jnp.where(masks[it[7]][1], _mm(kt, bh_m, _NT), 0.0))
    ts = _unit_triangular_inverses(a_kbs, same16, same32)
    ws = []
    for i, it in enumerate(items):
        for h in range(HEADS_PER_TILE):
            a_kk = jnp.where(masks[it[7]][1], gs[HEADS_PER_TILE * i + h][0:c, 0:c], 0.0)
            ws.append(ps[i][0:c] + _mm(a_kk, it[1], _NN))
    us = [_mm(t, w, _NN) for t, w in zip(ts, ws)]
    vus = [jnp.concatenate([it[1], -jnp.where(first_head, us[HEADS_PER_TILE * i], us[HEADS_PER_TILE * i + 1])], axis=0)
           for i, it in enumerate(items)]
    ys = []
    for i, it in enumerate(items):
        parts = [ps[i][c:2 * c] + _mm(jnp.where(masks[it[7]][2], gs[HEADS_PER_TILE * i + h][c:2 * c], 0.0), vus[i], _NN)
                 for h in range(HEADS_PER_TILE)]
        ys.append(jnp.where(first_head, parts[0], parts[1]))
    dss = [_mm(vu, pr[4], _TN) for vu, pr in zip(vus, prep)]
    return [(y, it[6] * pr[5] + jnp.where(block_diag, ds, 0.0))
            for y, it, pr, ds in zip(ys, items, prep, dss)]


def _wkv_kernel(rf, vf, kkf, lwf, kf, af, rb, vb, kkb, lwb, kb, ab, s0_ref,
                yf_ref, yb_ref, s_ref):
    @pl.when(pl.program_id(2) == 0)
    def _():
        s_ref[...] = s0_ref[...]

    items = []
    for p in range(PAIRS_PER_STEP):
        ln = slice(p * LANES, (p + 1) * LANES)
        items.append((rf[0, :, ln], vf[0, :, ln].astype(F32), kkf[0, :, ln], lwf[0, 0, :, ln], kf[0, 0, :, ln],
                      af[0, 0, :, ln], s_ref[0, 0, p], False))
        items.append((rb[0, :, ln], vb[0, :, ln].astype(F32), kkb[0, :, ln], lwb[0, 0, :, ln], kb[0, 0, :, ln],
                      ab[0, 0, :, ln], s_ref[0, 1, p], True))
    out = _wkv_chunks(items)
    for p in range(PAIRS_PER_STEP):
        ln = slice(p * LANES, (p + 1) * LANES)
        yf_ref[0, :, ln], s_ref[0, 0, p] = out[2 * p]
        yb_ref[0, :, ln], s_ref[0, 1, p] = out[2 * p + 1]


def wkv7_scan_pallas(r, v, kk, lw, k, a, s0):
    bsz, seq, _ = r.shape
    c = SCAN_CHUNK
    nc = seq // c
    n = A_HEAD_DIM
    s0p = s0.reshape(bsz, 2, N_HEAD_PAIRS, HEADS_PER_TILE, n, n)
    z = jnp.zeros_like(s0p[:, :, :, 0])
    s0bd = jnp.concatenate([jnp.concatenate([s0p[:, :, :, 0], z], axis=-1),
                            jnp.concatenate([z, s0p[:, :, :, 1]], axis=-1)], axis=-2)

    wd = PAIRS_PER_STEP * LANES
    shared_f = pl.BlockSpec((1, c, wd), lambda b, p, i: (b, i, p))
    shared_b = pl.BlockSpec((1, c, wd), lambda b, p, i: (b, nc - 1 - i, p))
    dir_f = pl.BlockSpec((1, 1, c, wd), lambda b, p, i: (0, b, i, p))
    dir_b = pl.BlockSpec((1, 1, c, wd), lambda b, p, i: (1, b, nc - 1 - i, p))
    st = pl.BlockSpec((1, 2, PAIRS_PER_STEP, LANES, LANES), lambda b, p, i: (b, 0, p, 0, 0))
    yf, yb, sfin = pl.pallas_call(
        _wkv_kernel,
        grid=(bsz, N_HEAD_PAIRS // PAIRS_PER_STEP, nc),
        in_specs=[shared_f, shared_f, shared_f, dir_f, dir_f, dir_f,
                  shared_b, shared_b, shared_b, dir_b, dir_b, dir_b, st],
        out_specs=[shared_f, shared_b, st],
        out_shape=[jax.ShapeDtypeStruct((bsz, seq, D_A), F32),
                   jax.ShapeDtypeStruct((bsz, seq, D_A), F32),
                   jax.ShapeDtypeStruct((bsz, 2, N_HEAD_PAIRS, LANES, LANES), F32)],
        compiler_params=pltpu.CompilerParams(
            dimension_semantics=("arbitrary", "arbitrary", "arbitrary")),
        name="wkv7_scan",
    )(r, v, kk, lw, k, a, r, v, kk, lw, k, a, s0bd)
    s_fin = jnp.stack([sfin[..., :n, :n], sfin[..., n:, n:]], axis=3)
    return yf, yb, s_fin.reshape(bsz, 2, A_HEADS, n, n)


def grid_posemb(n_tokens, dtype):
    rows = n_tokens // GRID_W
    row = jnp.repeat(jnp.arange(rows, dtype=F32), GRID_W)
    col = jnp.tile(jnp.arange(GRID_W, dtype=F32), rows)
    quarter = D_MODEL // 4
    freq = 1.0 / (POS_BASE ** (jnp.arange(quarter, dtype=F32) / quarter))
    ar = row[:, None] * freq[None, :]
    ac = col[:, None] * freq[None, :]
    return jnp.concatenate([jnp.sin(ar), jnp.cos(ar), jnp.sin(ac), jnp.cos(ac)], axis=-1).astype(dtype)


def _exclusive_prefix(x):
    rows, seq = x.shape
    tri = jnp.where(lax.broadcasted_iota(jnp.int32, (LANES, LANES), 0)
                    <= lax.broadcasted_iota(jnp.int32, (LANES, LANES), 1), 1.0, 0.0).astype(BF16)
    carry = jnp.zeros((rows, 1), F32)
    out = []
    for blk in range(seq // LANES):
        xb = x[:, blk * LANES:(blk + 1) * LANES]
        inc = jnp.dot(xb.astype(BF16), tri, preferred_element_type=F32)
        out.append(inc - xb + carry)
        carry = carry + inc[:, LANES - 1:LANES]
    return jnp.concatenate(out, axis=1)


SELECT_STEPS = 80


def _select_kernel(aff_ref, slot_ref, *, cap):
    bsz, n_exp, seq = aff_ref.shape
    aff = aff_ref[...].reshape(bsz * n_exp, seq)
    count = lambda t: jnp.sum(jnp.where(aff >= t, 1.0, 0.0), axis=1, keepdims=True)
    lo = jnp.min(aff, axis=1, keepdims=True)
    hi = 2.0 * jnp.max(aff, axis=1, keepdims=True) + 1e-30

    def halve(_, bracket):
        lo, hi = bracket
        mid = lo + 0.5 * (hi - lo)
        ok = count(mid) >= cap
        return jnp.where(ok, mid, lo), jnp.where(ok, hi, mid)

    lo, hi = lax.fori_loop(0, SELECT_STEPS, halve, (lo, hi))
    above = jnp.where(aff >= hi, 1.0, 0.0)
    tied = jnp.where(aff >= lo, 1.0, 0.0) - above
    need = cap - jnp.sum(above, axis=1, keepdims=True)
    keep = above + tied * jnp.where(_exclusive_prefix(tied) < need, 1.0, 0.0)
    slot = _exclusive_prefix(keep)
    slot_ref[...] = jnp.where(keep > 0.5, slot, -1.0).astype(jnp.int32).reshape(bsz, n_exp, seq)


def _expert_kernel(slot_ref, aff_ref, h_ref, wg_ref, wu_ref, wd_ref, y_ref, wg_s, wu_s, wd_s, *, cap):
    @pl.when(pl.program_id(1) == 0)
    def _():
        wg_s[...] = wg_ref[0, 0].astype(BF16)
        wu_s[...] = wu_ref[0, 0].astype(BF16)
        wd_s[...] = wd_ref[0, 0].astype(BF16)

    group, seq, _ = h_ref.shape
    slot_iota = lax.broadcasted_iota(jnp.int32, (cap, seq), 0)
    xs, vals = [], []
    for rb in range(group):
        hit = slot_iota == slot_ref[rb, 0]
        xs.append(jnp.dot(jnp.where(hit, 1.0, 0.0).astype(BF16), h_ref[rb], preferred_element_type=F32))
        vals.append(jnp.sum(jnp.where(hit, aff_ref[rb, 0], 0.0), axis=1, keepdims=True))
    xs = jnp.concatenate(xs, axis=0).astype(BF16)
    g = jnp.dot(xs, wg_s[...], preferred_element_type=F32)
    u = jnp.dot(xs, wu_s[...], preferred_element_type=F32)
    hid = (g * jax.nn.sigmoid(g) * u).astype(BF16)
    y = jnp.dot(hid, wd_s[...], preferred_element_type=F32) * jnp.concatenate(vals, axis=0)
    for rb in range(group):
        y_ref[rb, 0] = y[rb * cap:(rb + 1) * cap].astype(y_ref.dtype)


def _combine_kernel(slot_ref, ys_ref, x_ref, gate_ref, fg_ref, o_ref, *, cap, final):
    rows = x_ref.shape[1]
    n_exp = slot_ref.shape[2]
    slot_iota = lax.broadcasted_iota(jnp.int32, (rows, cap), 1)
    moe = jnp.zeros(x_ref.shape[1:], F32)
    for e in range(n_exp):
        onehot = jnp.where(slot_iota == slot_ref[0, :, e:e + 1], 1.0, 0.0).astype(BF16)
        moe = moe + jnp.dot(onehot, ys_ref[0, e * cap:(e + 1) * cap], preferred_element_type=F32)
    out = x_ref[0] + gate_ref[0] * moe
    o_ref[0] = _rms(out, fg_ref[...]) if final else out


MOE_VMEM_BYTES = 56 * 1024 * 1024
COMBINE_ROWS = 512
EXPERT_ROWS = 512


def expert_choice_ffn(x, gate, h, aff_t, layer, w_gate, w_up, w_down, final_g, final):
    bsz, seq, d = h.shape
    _, n_exp, _, f = w_gate.shape
    cap = EC_FACTOR * seq // n_exp
    slot = pl.pallas_call(
        functools.partial(_select_kernel, cap=float(cap)),
        out_shape=jax.ShapeDtypeStruct((bsz, n_exp, seq), jnp.int32),
        compiler_params=pltpu.CompilerParams(vmem_limit_bytes=MOE_VMEM_BYTES),
        name="moe_select",
    )(aff_t)
    group = min(bsz, max(1, EXPERT_ROWS // cap))
    per_expert = lambda w: pl.BlockSpec((group, 1, 1, w), lambda e, b: (b, e, 0, 0))
    weight = lambda k, n: pl.BlockSpec((1, 1, k, n), lambda e, b: (layer, e, 0, 0))
    ys = pl.pallas_call(
        functools.partial(_expert_kernel, cap=cap),
        grid=(n_exp, bsz // group),
        in_specs=[per_expert(seq), per_expert(seq),
                  pl.BlockSpec((group, seq, d), lambda e, b: (b, 0, 0)),
                  weight(d, f), weight(d, f), weight(f, d)],
        out_specs=pl.BlockSpec((group, 1, cap, d), lambda e, b: (b, e, 0, 0)),
        out_shape=jax.ShapeDtypeStruct((bsz, n_exp, cap, d), BF16),
        scratch_shapes=[pltpu.VMEM((d, f), BF16), pltpu.VMEM((d, f), BF16), pltpu.VMEM((f, d), BF16)],
        compiler_params=pltpu.CompilerParams(dimension_semantics=("arbitrary", "arbitrary"),
                                             vmem_limit_bytes=MOE_VMEM_BYTES),
        name="moe_experts",
    )(slot.reshape(bsz, n_exp, 1, seq), aff_t.reshape(bsz, n_exp, 1, seq), h, w_gate, w_up, w_down)
    rows = min(COMBINE_ROWS, seq)
    return pl.pallas_call(
        functools.partial(_combine_kernel, cap=cap, final=final),
        grid=(bsz, seq // rows),
        in_specs=[pl.BlockSpec((1, rows, n_exp), lambda b, i: (b, i, 0)),
                  pl.BlockSpec((1, n_exp * cap, d), lambda b, i: (b, 0, 0)),
                  pl.BlockSpec((1, rows, d), lambda b, i: (b, i, 0)),
                  pl.BlockSpec((1, 1, d), lambda b, i: (b, 0, 0)),
                  pl.BlockSpec((1, d), lambda b, i: (0, 0))],
        out_specs=pl.BlockSpec((1, rows, d), lambda b, i: (b, i, 0)),
        out_shape=jax.ShapeDtypeStruct((bsz, seq, d), F32),
        compiler_params=pltpu.CompilerParams(dimension_semantics=("arbitrary", "arbitrary"),
                                             vmem_limit_bytes=MOE_VMEM_BYTES),
        name="moe_combine",
    )(jnp.swapaxes(slot, 1, 2), ys.reshape(bsz, n_exp * cap, d), x, gate, _row(final_g))


def _mod_kernel(c_ref, w_ref, b_ref, o_ref):
    c = c_ref[...]
    w_hi, w_lo = _hi_lo(w_ref[0])
    o_ref[0] = _split_dot(c * jax.nn.sigmoid(c), w_hi, w_lo) + b_ref[0]


def modulation(cvecs, w_mod, b_mod):
    n, d = cvecs.shape
    depth = w_mod.shape[0]
    rows = -(-n // 8) * 8
    cpad = jnp.pad(cvecs, ((0, rows - n), (0, 0)))
    out = pl.pallas_call(
        _mod_kernel,
        grid=(depth, 6),
        in_specs=[pl.BlockSpec((rows, d), lambda l, n_: (0, 0)),
                  pl.BlockSpec((1, d, d), lambda l, n_: (l, 0, n_)),
                  pl.BlockSpec((1, 1, d), lambda l, n_: (l, 0, n_))],
        out_specs=pl.BlockSpec((1, rows, d), lambda l, n_: (l, 0, n_)),
        out_shape=jax.ShapeDtypeStruct((depth, rows, 6 * d), F32),
        compiler_params=pltpu.CompilerParams(dimension_semantics=("arbitrary", "arbitrary")),
        name="modulation",
    )(cpad, w_mod, b_mod.reshape(depth, 1, 6 * d))
    return out[:, :n].reshape(depth, n, 6, d)


TOKEN_TILE = 512
HALO = 16
LAYER_VMEM_BYTES = 48 * 1024 * 1024


def _rms(x, g):
    return x * lax.rsqrt(jnp.mean(x * x, axis=-1, keepdims=True) + RMS_EPS) * g


def _ln(x, g, b):
    mu = jnp.mean(x, axis=-1, keepdims=True)
    xc = x - mu
    return xc * lax.rsqrt(jnp.mean(xc * xc, axis=-1, keepdims=True) + LN_EPS) * g + b


def _split_dot(x, w_hi, w_lo):
    x_hi = x.astype(BF16)
    x_lo = (x - x_hi.astype(F32)).astype(BF16)
    dot = lambda a, b: jnp.dot(a, b, preferred_element_type=F32)
    return dot(x_hi, w_hi) + dot(x_lo, w_hi) + dot(x_hi, w_lo)


def _with_halo(x_ref, prev_ref, next_ref):
    return jnp.concatenate([prev_ref[0], x_ref[0], next_ref[0]], axis=0)


def _inside_mask(rows):
    i, n = pl.program_id(1), pl.num_programs(1)
    r = lax.broadcasted_iota(jnp.int32, (rows + 2 * HALO, 1), 0)
    return jnp.logical_and(jnp.logical_or(i > 0, r >= HALO), jnp.logical_or(i < n - 1, r < rows + HALO))


def _router(x1, mod_ref, nf_ref, wr_hi_ref, wr_lo_ref, br_ref, h2_ref, aff_ref):
    h2 = _rms(x1, nf_ref[...]) * (1.0 + mod_ref[0, 4:5]) + mod_ref[0, 3:4]
    h2_ref[0] = h2.astype(h2_ref.dtype)
    h_hi = h2.astype(BF16)
    h_lo = (h2 - h_hi.astype(F32)).astype(BF16)
    dot = lambda w, a: lax.dot_general(w, a, (_NT, ((), ())), preferred_element_type=F32)
    logits = dot(wr_hi_ref[...], h_hi) + dot(wr_hi_ref[...], h_lo) + dot(wr_lo_ref[...], h_hi) + br_ref[...]
    e = jnp.exp(logits - jnp.max(logits, axis=0, keepdims=True))
    aff_ref[0] = e / jnp.sum(e, axis=0, keepdims=True)


def _odd_kernel(x_ref, xp_ref, xn_ref, mod_ref, nm_ref, nf_ref, win_ref, cw_ref, cb_ref, cg_ref, cbb_ref,
                vg_ref, vb_ref, ws_ref, bs_ref, wout_ref, wr_hi_ref, wr_lo_ref, br_ref,
                x1_ref, h2_ref, aff_ref, glu_s):
    rows = x_ref.shape[1]
    inside = _inside_mask(rows)
    xa = _with_halo(x_ref, xp_ref, xn_ref)
    h = _rms(xa, nm_ref[...]) * (1.0 + mod_ref[0, 1:2]) + mod_ref[0, 0:1]
    h = jnp.where(inside, h, 0.0).astype(BF16)
    pc = jnp.dot(h, win_ref[:, 0:2 * D_C], preferred_element_type=F32)
    glu = pc[:, 0:D_C] * jax.nn.sigmoid(pc[:, D_C:2 * D_C])
    span = glu_s.shape[1]
    for r in range(SUBLANES):
        glu_s[r] = glu[r:r + span]
    acc = jnp.zeros((rows, D_C), F32)
    for j in range(CONV_WIDTH):
        off = HALO - CONV_WIDTH // 2 + j
        base = off // SUBLANES * SUBLANES
        acc = acc + cw_ref[j:j + 1, :] * glu_s[off % SUBLANES, base:base + rows, :]
    o_c = _ln(acc + cb_ref[...], cg_ref[...], cbb_ref[...])
    o_c = o_c * jax.nn.sigmoid(o_c)
    pd = jax.nn.gelu(jnp.dot(h[HALO:HALO + rows], win_ref[:, 2 * D_C:], preferred_element_type=F32))
    u = pd[:, 0:D_D]
    v = _ln(pd[:, D_D:], vg_ref[...], vb_ref[...]).astype(BF16)
    chunks = []
    for ck in range(rows // CHUNK):
        vc = v[ck * CHUNK:(ck + 1) * CHUNK]
        chunks.append(jnp.concatenate(
            [jnp.dot(ws_ref[hd], vc[:, hd * D_HEAD_DIM:(hd + 1) * D_HEAD_DIM], preferred_element_type=F32)
             for hd in range(D_HEADS)], axis=1) + bs_ref[...])
    o_d = u * jnp.concatenate(chunks, axis=0)
    mixed = jnp.dot(jnp.concatenate([o_c, o_d], axis=1).astype(BF16), wout_ref[...], preferred_element_type=F32)
    x1 = x_ref[0] + mod_ref[0, 2:3] * mixed
    x1_ref[0] = x1
    _router(x1, mod_ref, nf_ref, wr_hi_ref, wr_lo_ref, br_ref, h2_ref, aff_ref)


def _row(v):
    return v.reshape(1, -1)


def _hi_lo(w):
    hi = w.astype(BF16)
    return hi, (w - hi.astype(F32)).astype(BF16)


def _tile_specs(rows, seq, d):
    per = rows // HALO
    last = seq // HALO - 1
    return [pl.BlockSpec((1, rows, d), lambda b, i: (b, i, 0)),
            pl.BlockSpec((1, HALO, d), lambda b, i: (b, jnp.maximum(i * per - 1, 0), 0)),
            pl.BlockSpec((1, HALO, d), lambda b, i: (b, jnp.minimum((i + 1) * per, last), 0))]


def _shared_tile_specs(rows, seq, d):
    per = rows // HALO
    last = seq // HALO - 1
    return [pl.BlockSpec((rows, d), lambda b, i: (i, 0)),
            pl.BlockSpec((HALO, d), lambda b, i: (jnp.maximum(i * per - 1, 0), 0)),
            pl.BlockSpec((HALO, d), lambda b, i: (jnp.minimum((i + 1) * per, last), 0))]


def _full(a):
    return pl.BlockSpec(a.shape, lambda b, i: (0,) * a.ndim)


def odd_layer(x, mod, norm_mix, norm_ffn, w_in, conv_w, conv_b, cln_g, cln_b, vln_g, vln_b, w_s, b_s, w_out,
              w_router, b_router):
    bsz, seq, d = x.shape
    rows = min(TOKEN_TILE, seq)
    n_exp = w_router.shape[1]
    wr_hi, wr_lo = _hi_lo(w_router.T)
    bs_full = jnp.repeat(b_s.T, D_HEAD_DIM, axis=1)
    consts = [_row(norm_mix), _row(norm_ffn), w_in.astype(BF16), conv_w, _row(conv_b), _row(cln_g), _row(cln_b),
              _row(vln_g), _row(vln_b), w_s.astype(BF16), bs_full, w_out.astype(BF16), wr_hi, wr_lo,
              b_router.reshape(-1, 1)]
    tile = lambda w: pl.BlockSpec((1, rows, w), lambda b, i: (b, i, 0))
    return pl.pallas_call(
        _odd_kernel,
        grid=(bsz, seq // rows),
        in_specs=_tile_specs(rows, seq, d) + [pl.BlockSpec((1, 6, d), lambda b, i: (b, 0, 0))]
        + [_full(a) for a in consts],
        out_specs=[tile(d), tile(d), pl.BlockSpec((1, n_exp, rows), lambda b, i: (b, 0, i))],
        out_shape=[jax.ShapeDtypeStruct((bsz, seq, d), F32), jax.ShapeDtypeStruct((bsz, seq, d), BF16),
                   jax.ShapeDtypeStruct((bsz, n_exp, seq), F32)],
        scratch_shapes=[pltpu.VMEM((SUBLANES, rows + 2 * HALO - SUBLANES, D_C), F32)],
        compiler_params=pltpu.CompilerParams(dimension_semantics=("arbitrary", "arbitrary"),
                                             vmem_limit_bytes=LAYER_VMEM_BYTES),
        name="odd_layer",
    )(x, x, x, mod, *consts)


EVEN_TILE = 256
DECAY_SCALE = math.exp(-0.5)


def _head_sums(x, ones_bd):
    hi = x.astype(BF16)
    lo = (x - hi.astype(F32)).astype(BF16)
    dot = lambda a: jnp.dot(a, ones_bd, preferred_element_type=F32)
    return jnp.concatenate([dot(hi[:, g * LANES:(g + 1) * LANES]) + dot(lo[:, g * LANES:(g + 1) * LANES])
                            for g in range(x.shape[1] // LANES)], axis=1)


def _head_ones():
    head_shift = A_HEAD_DIM.bit_length() - 1
    same = ((lax.broadcasted_iota(jnp.int32, (LANES, LANES), 0) >> head_shift)
            == (lax.broadcasted_iota(jnp.int32, (LANES, LANES), 1) >> head_shift))
    return jnp.where(same, 1.0, 0.0).astype(BF16)


def _even_pre_kernel(x_ref, xp_ref, xn_ref, pos_ref, posp_ref, posn_ref, mod_ref, nm_ref, win_ref, mu_rkv_ref,
                     mu_wag_ref, w1_ref, w2_ref, w0_ref, a1_ref, a2_ref, a0_ref, g1_ref, g2_ref, kk_ref, ka_ref,
                     rk_ref, x_out, r_out, v_out, kk_out, lw_out, k_out, a_out, gate_out, bonus_out, u_out,
                     proj_s, h_s):
    rows = x_ref.shape[1]
    inside = _inside_mask(rows)
    xa = _with_halo(x_ref, xp_ref, xn_ref) + jnp.concatenate([posp_ref[...], pos_ref[...], posn_ref[...]], axis=0)
    x_out[0] = xa[HALO:HALO + rows]
    h = _rms(xa, nm_ref[...]) * (1.0 + mod_ref[0, 1:2]) + mod_ref[0, 0:1]
    h = jnp.where(inside, h, 0.0)
    h_s[...] = h
    proj = jnp.dot(h.astype(BF16), win_ref[...], preferred_element_type=F32)
    proj_s[...] = proj[:, 0:3 * D_A]
    u_out[0] = proj[HALO:HALO + rows, 3 * D_A:].astype(u_out.dtype)

    def shifted(ref, lo, hi):
        cur = ref[HALO:HALO + rows, lo:hi]
        return cur, 0.5 * (ref[HALO - 1:HALO - 1 + rows, lo:hi] + ref[HALO + 1:HALO + 1 + rows, lo:hi]) - cur

    r, dr = shifted(proj_s, 0, D_A)
    k, dk = shifted(proj_s, D_A, 2 * D_A)
    v, dv = shifted(proj_s, 2 * D_A, 3 * D_A)
    r = r + dr * mu_rkv_ref[0:1]
    k = k + dk * mu_rkv_ref[1:2]
    v = v + dv * mu_rkv_ref[2:3]
    hc, dh = shifted(h_s, 0, D_MODEL)
    xw = (hc + dh * mu_wag_ref[0:1]).astype(BF16)
    xa_ = (hc + dh * mu_wag_ref[1:2]).astype(BF16)
    xg = (hc + dh * mu_wag_ref[2:3]).astype(BF16)
    dot = lambda a, b: jnp.dot(a, b, preferred_element_type=F32)
    w_pre = w0_ref[...] + dot(jnp.tanh(dot(xw, w1_ref[...])).astype(BF16), w2_ref[...])
    icl = jax.nn.sigmoid(a0_ref[...] + dot(dot(xa_, a1_ref[...]).astype(BF16), a2_ref[...]))
    gate_out[0] = dot(jax.nn.sigmoid(dot(xg, g1_ref[...])).astype(BF16), g2_ref[...])
    ones_bd = _head_ones()
    kk = k * kk_ref[...]
    kk = kk / jnp.maximum(jnp.sqrt(_head_sums(kk * kk, ones_bd)), 1e-12)
    r_out[0] = r
    v_out[0] = v.astype(v_out.dtype)
    kk_out[0] = kk
    bonus_out[0] = _head_sums(r * k * rk_ref[...], ones_bd) * v
    for z in range(2):
        a_z = icl[:, z * D_A:(z + 1) * D_A]
        lw_out[z, 0] = -DECAY_SCALE * jax.nn.sigmoid(w_pre[:, z * D_A:(z + 1) * D_A])
        a_out[z, 0] = a_z
        k_out[z, 0] = k * (1.0 + (a_z - 1.0) * ka_ref[...])


def _block_diag2(w):
    z = jnp.zeros_like(w[0])
    return jnp.concatenate([jnp.concatenate([w[0], z], axis=1), jnp.concatenate([z, w[1]], axis=1)], axis=0)


def even_pre(x, pos, mod, norm_mix, w_in, mu_rkv, mu_wag, w0, w1, w2, a0, a1, a2, g1, g2, k_k, k_a, r_k):
    bsz, seq, d = x.shape
    rows = min(EVEN_TILE, seq)
    cat = lambda w: jnp.concatenate([w[0], w[1]], axis=1)
    consts = [_row(norm_mix), w_in.astype(BF16), mu_rkv, mu_wag,
              cat(w1).astype(BF16), _block_diag2(w2).astype(BF16), _row(w0),
              cat(a1).astype(BF16), _block_diag2(a2).astype(BF16), _row(a0),
              g1.astype(BF16), g2.astype(BF16), _row(k_k), _row(k_a), _row(r_k)]
    tile = lambda w: pl.BlockSpec((1, rows, w), lambda b, i: (b, i, 0))
    tile2 = pl.BlockSpec((2, 1, rows, D_A), lambda b, i: (0, b, i, 0))
    sds = lambda w, dt=F32: jax.ShapeDtypeStruct((bsz, seq, w), dt)
    sds2 = jax.ShapeDtypeStruct((2, bsz, seq, D_A), F32)
    return pl.pallas_call(
        _even_pre_kernel,
        grid=(bsz, seq // rows),
        in_specs=_tile_specs(rows, seq, d) + _shared_tile_specs(rows, seq, d)
        + [pl.BlockSpec((1, 6, d), lambda b, i: (b, 0, 0))] + [_full(a) for a in consts],
        out_specs=[tile(d), tile(D_A), tile(D_A), tile(D_A), tile2, tile2, tile2, tile(D_A), tile(D_A), tile(D_B)],
        out_shape=[sds(d), sds(D_A), sds(D_A, BF16), sds(D_A), sds2, sds2, sds2, sds(D_A), sds(D_A), sds(D_B, BF16)],
        scratch_shapes=[pltpu.VMEM((rows + 2 * HALO, 3 * D_A), F32), pltpu.VMEM((rows + 2 * HALO, d), F32)],
        compiler_params=pltpu.CompilerParams(dimension_semantics=("arbitrary", "arbitrary"),
                                             vmem_limit_bytes=LAYER_VMEM_BYTES),
        name="even_pre",
    )(x, x, x, pos, pos, pos, mod, *consts)


FOURIER_ROWS = 512
DFT_SPLIT = 64


def _fourier_kernel(u_ref, f64_ref, fl_ref, o_ref, ucs_s):
    seq = u_ref.shape[1]

    @pl.when(pl.program_id(1) == 0)
    def _():
        ucs = jnp.dot(u_ref[0], f64_ref[...], preferred_element_type=F32)
        ucs_s[0:seq] = ucs[:, 0:D_B].astype(BF16)
        ucs_s[seq:2 * seq] = ucs[:, D_B:].astype(BF16)

    scale = 1.0 / math.sqrt(seq * B_GROUP_DIM)
    o_ref[0] = (jnp.dot(fl_ref[...], ucs_s[...], preferred_element_type=F32) * scale).astype(o_ref.dtype)


def _dft_tables(seq):
    def cs(n):
        i = jnp.arange(n, dtype=jnp.int32)
        ang = ((i[:, None] * i[None, :]) % n).astype(F32) * (2.0 * math.pi / n)
        return jnp.cos(ang), jnp.sin(ang)
    c64, s64 = cs(B_GROUP_DIM)
    eye = jnp.eye(B_GROUPS, dtype=F32)
    f64 = jnp.concatenate([jnp.kron(eye, c64), jnp.kron(eye, s64)], axis=1)
    split = min(DFT_SPLIT, seq)
    s = jnp.arange(seq, dtype=jnp.int32)[None, :]
    ang = lambda t: ((t[:, None] * s) % seq).astype(F32) * (2.0 * math.pi / seq)
    ang_a = ang(jnp.arange(seq // split, dtype=jnp.int32) * split)
    ang_b = ang(jnp.arange(split, dtype=jnp.int32))
    ca, sa = jnp.cos(ang_a)[:, None, :], jnp.sin(ang_a)[:, None, :]
    cb, sb = jnp.cos(ang_b)[None, :, :], jnp.sin(ang_b)[None, :, :]
    cl = (ca * cb - sa * sb).reshape(seq, seq)
    sl = (sa * cb + ca * sb).reshape(seq, seq)
    return f64.astype(BF16), jnp.concatenate([cl, -sl], axis=1).astype(BF16)


def fourier_mixer(u):
    bsz, seq, _ = u.shape
    rows = min(FOURIER_ROWS, seq)
    f64, fl = _dft_tables(seq)
    return pl.pallas_call(
        _fourier_kernel,
        grid=(bsz, seq // rows),
        in_specs=[pl.BlockSpec((1, seq, D_B), lambda b, i: (b, 0, 0)),
                  pl.BlockSpec(f64.shape, lambda b, i: (0, 0)),
                  pl.BlockSpec((rows, 2 * seq), lambda b, i: (i, 0))],
        out_specs=pl.BlockSpec((1, rows, D_B), lambda b, i: (b, i, 0)),
        out_shape=jax.ShapeDtypeStruct((bsz, seq, D_B), BF16),
        scratch_shapes=[pltpu.VMEM((2 * seq, D_B), BF16)],
        compiler_params=pltpu.CompilerParams(dimension_semantics=("arbitrary", "arbitrary"),
                                             vmem_limit_bytes=LAYER_VMEM_BYTES),
        name="fourier_mixer",
    )(u, f64, fl)


def _even_post_kernel(yf_ref, yb_ref, bonus_ref, gate_ref, ob_ref, x_ref, mod_ref, gnw_ref, gnb_ref, wout_ref,
                      nf_ref, wr_hi_ref, wr_lo_ref, br_ref, x1_ref, h2_ref, aff_ref):
    ones_bd = _head_ones()
    y = yf_ref[0] + yb_ref[0]
    mu = _head_sums(y, ones_bd) * (1.0 / A_HEAD_DIM)
    yc = y - mu
    var = _head_sums(yc * yc, ones_bd) * (1.0 / A_HEAD_DIM)
    o_a = (yc * lax.rsqrt(var + GN_EPS) * gnw_ref[...] + gnb_ref[...] + bonus_ref[0]) * gate_ref[0]
    mixed = (jnp.dot(o_a.astype(BF16), wout_ref[0:D_A], preferred_element_type=F32)
             + jnp.dot(ob_ref[0], wout_ref[D_A:], preferred_element_type=F32))
    x1 = x_ref[0] + mod_ref[0, 2:3] * mixed
    x1_ref[0] = x1
    _router(x1, mod_ref, nf_ref, wr_hi_ref, wr_lo_ref, br_ref, h2_ref, aff_ref)


def even_post(yf, yb, bonus, gate, o_b, x, mod, gn_w, gn_b, w_out, norm_ffn, w_router, b_router):
    bsz, seq, d = x.shape
    rows = min(TOKEN_TILE, seq)
    n_exp = w_router.shape[1]
    wr_hi, wr_lo = _hi_lo(w_router.T)
    consts = [_row(gn_w), _row(gn_b), w_out.astype(BF16), _row(norm_ffn), wr_hi, wr_lo, b_router.reshape(-1, 1)]
    tile = lambda w: pl.BlockSpec((1, rows, w), lambda b, i: (b, i, 0))
    return pl.pallas_call(
        _even_post_kernel,
        grid=(bsz, seq // rows),
        in_specs=[tile(D_A), tile(D_A), tile(D_A), tile(D_A), tile(D_B), tile(d),
                  pl.BlockSpec((1, 6, d), lambda b, i: (b, 0, 0))] + [_full(a) for a in consts],
        out_specs=[tile(d), tile(d), pl.BlockSpec((1, n_exp, rows), lambda b, i: (b, 0, i))],
        out_shape=[jax.ShapeDtypeStruct((bsz, seq, d), F32), jax.ShapeDtypeStruct((bsz, seq, d), BF16),
                   jax.ShapeDtypeStruct((bsz, n_exp, seq), F32)],
        compiler_params=pltpu.CompilerParams(dimension_semantics=("arbitrary", "arbitrary"),
                                             vmem_limit_bytes=LAYER_VMEM_BYTES),
        name="even_post",
    )(yf, yb, bonus, gate, o_b, x, mod, *consts)


def run_trunk(x, pos, mods, s_init, P):
    states = []
    for l in range(DEPTH):
        j = l // 2
        mod = jnp.broadcast_to(mods[l], (x.shape[0], 6, x.shape[2]))
        if l % 2 == 0:
            x, r, v, kk, lw, kd, a, gate, bonus, u = even_pre(
                x, pos if l == 0 else jnp.zeros_like(pos), mod, P['norm_mix'][l], P['ev_w_in'][j], P['ev_mu_rkv'][j], P['ev_mu_wag'][j],
                P['ev_w0'][j], P['ev_w1'][j], P['ev_w2'][j], P['ev_a0'][j], P['ev_a1'][j], P['ev_a2'][j],
                P['ev_g1'][j], P['ev_g2'][j], P['ev_k_k'][j], P['ev_k_a'][j], P['ev_r_k'][j])
            yf, yb, s_fin = wkv7_scan_pallas(r, v, kk, lw, kd, a, s_init[:, j])
            states.append(s_fin)
            x1, h2, aff_t = even_post(yf, yb, bonus, gate, fourier_mixer(u), x, mod, P['ev_gn_w'][j], P['ev_gn_b'][j],
                                      P['ev_w_out'][j], P['norm_ffn'][l], P['moe_router'][l], P['moe_router_b'][l])
        else:
            x1, h2, aff_t = odd_layer(x, mod, P['norm_mix'][l], P['norm_ffn'][l], P['od_w_in'][j], P['od_conv_w'][j],
                                      P['od_conv_b'][j], P['od_cln_g'][j], P['od_cln_b'][j], P['od_vln_g'][j],
                                      P['od_vln_b'][j], P['od_w_s'][j], P['od_b_s'][j], P['od_w_out'][j],
                                      P['moe_router'][l], P['moe_router_b'][l])
        x = expert_choice_ffn(x1, mod[:, 5:6], h2, aff_t, l, P['moe_w_gate'], P['moe_w_up'], P['moe_w_down'],
                              P['final_norm'], l == DEPTH - 1)
    return x, jnp.stack(states, axis=1)


def kernel(x_prompt, x_sample, state_wkv, c, c_ctx, mod_w, mod_b, norm_mix, norm_ffn, final_norm,
           ev_w_in, ev_w_out, ev_mu_rkv, ev_mu_wag, ev_w0, ev_w1, ev_w2, ev_a0, ev_a1, ev_a2,
           ev_g1, ev_g2, ev_k_k, ev_k_a, ev_r_k, ev_gn_w, ev_gn_b,
           od_w_in, od_w_out, od_conv_w, od_conv_b, od_cln_g, od_cln_b, od_vln_g, od_vln_b,
           od_w_s, od_b_s, moe_router, moe_router_b, moe_w_gate, moe_w_up, moe_w_down):
    P = dict(mod_w=mod_w, mod_b=mod_b, norm_mix=norm_mix, norm_ffn=norm_ffn, final_norm=final_norm,
             ev_w_in=ev_w_in, ev_w_out=ev_w_out, ev_mu_rkv=ev_mu_rkv, ev_mu_wag=ev_mu_wag,
             ev_w0=ev_w0, ev_w1=ev_w1, ev_w2=ev_w2, ev_a0=ev_a0, ev_a1=ev_a1, ev_a2=ev_a2,
             ev_g1=ev_g1, ev_g2=ev_g2, ev_k_k=ev_k_k, ev_k_a=ev_k_a, ev_r_k=ev_r_k,
             ev_gn_w=ev_gn_w, ev_gn_b=ev_gn_b,
             od_w_in=od_w_in, od_w_out=od_w_out, od_conv_w=od_conv_w, od_conv_b=od_conv_b,
             od_cln_g=od_cln_g, od_cln_b=od_cln_b, od_vln_g=od_vln_g, od_vln_b=od_vln_b,
             od_w_s=od_w_s, od_b_s=od_b_s, moe_router=moe_router, moe_router_b=moe_router_b,
             moe_w_gate=moe_w_gate, moe_w_up=moe_w_up, moe_w_down=moe_w_down)
    n_even = state_wkv.shape[1]
    s_zero = jnp.zeros((x_prompt.shape[0], n_even, 2, A_HEADS, A_HEAD_DIM, A_HEAD_DIM), F32)
    mods = modulation(jnp.concatenate([c_ctx[None, :], c], axis=0), mod_w, mod_b)
    no_pos = jnp.zeros(x_prompt.shape[1:], x_prompt.dtype)
    y_prompt, new_state_wkv = run_trunk(x_prompt, no_pos, mods[:, 0:1], s_zero, P)
    y_sample, _ = run_trunk(x_sample, grid_posemb(x_sample.shape[1], x_sample.dtype), mods[:, 1:], state_wkv, P)
    return (y_prompt, y_sample, new_state_wkv)
```

```python
import functools
import math

import jax
import jax.numpy as jnp
from jax import lax
from jax.experimental import pallas as pl
from jax.experimental.pallas import tpu as pltpu

D_MODEL = 1024
DEPTH = 2
GRID_W = 64
POS_BASE = 10000.0
A_HEADS = 12
A_HEAD_DIM = 64
D_A = A_HEADS * A_HEAD_DIM
B_GROUPS = 4
B_GROUP_DIM = 64
D_B = B_GROUPS * B_GROUP_DIM
D_C = 512
CONV_WIDTH = 31
D_HEADS = 4
D_HEAD_DIM = 128
D_D = D_HEADS * D_HEAD_DIM
CHUNK = 128
N_EXPERTS = 16
EC_FACTOR = 2
RMS_EPS = 1e-6
LN_EPS = 1e-5
GN_EPS = 64e-5

LANES = 128
SUBLANES = 8
SCAN_CHUNK = 64
HEADS_PER_TILE = LANES // A_HEAD_DIM
N_HEAD_PAIRS = A_HEADS // HEADS_PER_TILE
PAIRS_PER_STEP = 6

F32 = jnp.float32
BF16 = jnp.bfloat16


def _mm(a, b, dims):
    return lax.dot_general(a.astype(BF16), b.astype(BF16), (dims, ((), ())),
                           preferred_element_type=F32)


_NN = ((1,), (0,))
_NT = ((1,), (1,))
_TN = ((0,), (0,))


def _cumsum_rows(tri, x):
    x1 = x.astype(BF16)
    r1 = x - x1.astype(F32)
    x2 = r1.astype(BF16)
    x3 = (r1 - x2.astype(F32)).astype(BF16)
    dot = lambda y: lax.dot_general(tri, y, (_NN, ((), ())), preferred_element_type=F32)
    return dot(x1) + dot(x2) + dot(x3)


def _unit_triangular_inverses(mats, same16, same32):
    c = mats[0].shape[0]
    eye = (lax.broadcasted_iota(jnp.int32, (c, c), 0) == lax.broadcasted_iota(jnp.int32, (c, c), 1)).astype(F32)
    pws = [jnp.where(same16, -a, 0.0) for a in mats]
    ts = [eye + n for n in pws]
    for _ in range(3):
        pws = [_mm(pw, pw, _NN) for pw in pws]
        ts = [t + _mm(t, pw, _NN) for t, pw in zip(ts, pws)]
    in32 = jnp.logical_and(same32, jnp.logical_not(same16))
    tmp = [_mm(t, jnp.where(in32, a, 0.0), _NN) for t, a in zip(ts, mats)]
    ts = [t - _mm(x, t, _NN) for t, x in zip(ts, tmp)]
    tmp = [_mm(t, jnp.where(same32, 0.0, a), _NN) for t, a in zip(ts, mats)]
    ts = [t - _mm(x, t, _NN) for t, x in zip(ts, tmp)]
    return ts


def _wkv_chunks(items):
    c = items[0][0].shape[0]
    ti = lax.broadcasted_iota(jnp.int32, (c, c), 0)
    si = lax.broadcasted_iota(jnp.int32, (c, c), 1)
    ti2 = lax.broadcasted_iota(jnp.int32, (c, 2 * c), 0)
    si2 = lax.broadcasted_iota(jnp.int32, (c, 2 * c), 1) & (c - 1)
    masks = {False: (ti >= si, ti > si, ti2 >= si2), True: (ti <= si, ti < si, ti2 <= si2)}
    tri = {rev: masks[rev][0].astype(F32).astype(BF16) for rev in (False, True)}
    same16 = (ti >> 4) == (si >> 4)
    same32 = (ti >> 5) == (si >> 5)
    lane = lax.broadcasted_iota(jnp.int32, (1, LANES), 1)
    head_masks = [jnp.logical_and(lane >= h * A_HEAD_DIM, lane < (h + 1) * A_HEAD_DIM)
                  for h in range(HEADS_PER_TILE)]
    first_head = lane < A_HEAD_DIM
    head_shift = A_HEAD_DIM.bit_length() - 1
    block_diag = ((lax.broadcasted_iota(jnp.int32, (LANES, LANES), 0) >> head_shift)
                  == (lax.broadcasted_iota(jnp.int32, (LANES, LANES), 1) >> head_shift))

    cs_all = [_cumsum_rows(tri[rev], lw) for (_, _, _, lw, _, _, _, rev) in items]
    prep = []
    for (r, v, kk, lw, k, a, s, rev), cs in zip(items, cs_all):
        cs_end = cs[0:1] if rev else cs[c - 1:c]
        b = kk * a
        g_inv = jnp.exp(-cs)
        g_tail = jnp.exp(cs_end - cs)
        kt = kk * jnp.exp(cs - lw)
        rt = r * jnp.exp(cs)
        lhs = jnp.concatenate([kt, rt], axis=0)
        kb_tail = jnp.concatenate([k * g_tail, b * g_tail], axis=0)
        prep.append((lhs, kt, k * g_inv, b * g_inv, kb_tail, jnp.exp(cs_end)))
    ps = [_mm(pr[0], it[6], _NT) for pr, it in zip(prep, items)]
    gs, a_kbs = [], []
    for (lhs, kt, kh, bh, _, _), it in zip(prep, items):
        for m in head_masks:
            bh_m = jnp.where(m, bh, 0.0)
            gs.append(_mm(lhs, jnp.concatenate([jnp.where(m, kh, 0.0), bh_m], axis=0), _NT))
            a_kbs.append(jnp.where(masks[it[7]][1], _mm(kt, bh_m, _NT), 0.0))
    ts = _unit_triangular_inverses(a_kbs, same16, same32)
    ws = []
    for i, it in enumerate(items):
        for h in range(HEADS_PER_TILE):
            a_kk = jnp.where(masks[it[7]][1], gs[HEADS_PER_TILE * i + h][0:c, 0:c], 0.0)
            ws.append(ps[i][0:c] + _mm(a_kk, it[1], _NN))
    us = [_mm(t, w, _NN) for t, w in zip(ts, ws)]
    vus = [jnp.concatenate([it[1], -jnp.where(first_head, us[HEADS_PER_TILE * i], us[HEADS_PER_TILE * i + 1])], axis=0)
           for i, it in enumerate(items)]
    ys = []
    for i, it in enumerate(items):
        parts = [ps[i][c:2 * c] + _mm(jnp.where(masks[it[7]][2], gs[HEADS_PER_TILE * i + h][c:2 * c], 0.0), vus[i], _NN)
                 for h in range(HEADS_PER_TILE)]
        ys.append(jnp.where(first_head, parts[0], parts[1]))
    dss = [_mm(vu, pr[4], _TN) for vu, pr in zip(vus, prep)]
    return [(y, it[6] * pr[5] + jnp.where(block_diag, ds, 0.0))
            for y, it, pr, ds in zip(ys, items, prep, dss)]


def _wkv_kernel(rf, vf, kkf, lwf, kf, af, rb, vb, kkb, lwb, kb, ab, s0_ref,
                yf_ref, yb_ref, s_ref):
    @pl.when(pl.program_id(2) == 0)
    def _():
        s_ref[...] = s0_ref[...]

    items = []
    for p in range(PAIRS_PER_STEP):
        ln = slice(p * LANES, (p + 1) * LANES)
        items.append((rf[0, :, ln], vf[0, :, ln].astype(F32), kkf[0, :, ln], lwf[0, 0, :, ln], kf[0, 0, :, ln],
                      af[0, 0, :, ln], s_ref[0, 0, p], False))
        items.append((rb[0, :, ln], vb[0, :, ln].astype(F32), kkb[0, :, ln], lwb[0, 0, :, ln], kb[0, 0, :, ln],
                      ab[0, 0, :, ln], s_ref[0, 1, p], True))
    out = _wkv_chunks(items)
    for p in range(PAIRS_PER_STEP):
        ln = slice(p * LANES, (p + 1) * LANES)
        yf_ref[0, :, ln], s_ref[0, 0, p] = out[2 * p]
        yb_ref[0, :, ln], s_ref[0, 1, p] = out[2 * p + 1]


def wkv7_scan_pallas(r, v, kk, lw, k, a, s0):
    bsz, seq, _ = r.shape
    c = SCAN_CHUNK
    nc = seq // c
    n = A_HEAD_DIM
    s0p = s0.reshape(bsz, 2, N_HEAD_PAIRS, HEADS_PER_TILE, n, n)
    z = jnp.zeros_like(s0p[:, :, :, 0])
    s0bd = jnp.concatenate([jnp.concatenate([s0p[:, :, :, 0], z], axis=-1),
                            jnp.concatenate([z, s0p[:, :, :, 1]], axis=-1)], axis=-2)

    wd = PAIRS_PER_STEP * LANES
    shared_f = pl.BlockSpec((1, c, wd), lambda b, p, i: (b, i, p))
    shared_b = pl.BlockSpec((1, c, wd), lambda b, p, i: (b, nc - 1 - i, p))
    dir_f = pl.BlockSpec((1, 1, c, wd), lambda b, p, i: (0, b, i, p))
    dir_b = pl.BlockSpec((1, 1, c, wd), lambda b, p, i: (1, b, nc - 1 - i, p))
    st = pl.BlockSpec((1, 2, PAIRS_PER_STEP, LANES, LANES), lambda b, p, i: (b, 0, p, 0, 0))
    yf, yb, sfin = pl.pallas_call(
        _wkv_kernel,
        grid=(bsz, N_HEAD_PAIRS // PAIRS_PER_STEP, nc),
        in_specs=[shared_f, shared_f, shared_f, dir_f, dir_f, dir_f,
                  shared_b, shared_b, shared_b, dir_b, dir_b, dir_b, st],
        out_specs=[shared_f, shared_b, st],
        out_shape=[jax.ShapeDtypeStruct((bsz, seq, D_A), F32),
                   jax.ShapeDtypeStruct((bsz, seq, D_A), F32),
                   jax.ShapeDtypeStruct((bsz, 2, N_HEAD_PAIRS, LANES, LANES), F32)],
        compiler_params=pltpu.CompilerParams(
            dimension_semantics=("arbitrary", "arbitrary", "arbitrary")),
        name="wkv7_scan",
    )(r, v, kk, lw, k, a, r, v, kk, lw, k, a, s0bd)
    s_fin = jnp.stack([sfin[..., :n, :n], sfin[..., n:, n:]], axis=3)
    return yf, yb, s_fin.reshape(bsz, 2, A_HEADS, n, n)


def grid_posemb(n_tokens, dtype):
    rows = n_tokens // GRID_W
    row = jnp.repeat(jnp.arange(rows, dtype=F32), GRID_W)
    col = jnp.tile(jnp.arange(GRID_W, dtype=F32), rows)
    quarter = D_MODEL // 4
    freq = 1.0 / (POS_BASE ** (jnp.arange(quarter, dtype=F32) / quarter))
    ar = row[:, None] * freq[None, :]
    ac = col[:, None] * freq[None, :]
    return jnp.concatenate([jnp.sin(ar), jnp.cos(ar), jnp.sin(ac), jnp.cos(ac)], axis=-1).astype(dtype)


def _exclusive_prefix(x):
    rows, seq = x.shape
    tri = jnp.where(lax.broadcasted_iota(jnp.int32, (LANES, LANES), 0)
                    <= lax.broadcasted_iota(jnp.int32, (LANES, LANES), 1), 1.0, 0.0).astype(BF16)
    carry = jnp.zeros((rows, 1), F32)
    out = []
    for blk in range(seq // LANES):
        xb = x[:, blk * LANES:(blk + 1) * LANES]
        inc = jnp.dot(xb.astype(BF16), tri, preferred_element_type=F32)
        out.append(inc - xb + carry)
        carry = carry + inc[:, LANES - 1:LANES]
    return jnp.concatenate(out, axis=1)


SELECT_STEPS = 80


def _select_kernel(aff_ref, slot_ref, *, cap):
    bsz, n_exp, seq = aff_ref.shape
    aff = aff_ref[...].reshape(bsz * n_exp, seq)
    count = lambda t: jnp.sum(jnp.where(aff >= t, 1.0, 0.0), axis=1, keepdims=True)
    lo = jnp.min(aff, axis=1, keepdims=True)
    hi = 2.0 * jnp.max(aff, axis=1, keepdims=True) + 1e-30

    def halve(_, bracket):
        lo, hi = bracket
        mid = lo + 0.5 * (hi - lo)
        ok = count(mid) >= cap
        return jnp.where(ok, mid, lo), jnp.where(ok, hi, mid)

    lo, hi = lax.fori_loop(0, SELECT_STEPS, halve, (lo, hi))
    above = jnp.where(aff >= hi, 1.0, 0.0)
    tied = jnp.where(aff >= lo, 1.0, 0.0) - above
    need = cap - jnp.sum(above, axis=1, keepdims=True)
    keep = above + tied * jnp.where(_exclusive_prefix(tied) < need, 1.0, 0.0)
    slot = _exclusive_prefix(keep)
    slot_ref[...] = jnp.where(keep > 0.5, slot, -1.0).astype(jnp.int32).reshape(bsz, n_exp, seq)


def _band(i, cap, nblk):
    step = cap // nblk
    lo = i * step - step // 2
    if isinstance(i, int):
        return min(max(lo, 0), cap - 2 * step), 2 * step
    return jnp.clip(lo, 0, cap - 2 * step), 2 * step


def _expert_kernel(slot_ref, h_ref, wg_ref, wu_ref, wd_ref, y_ref, wg_s, wu_s, wd_s, xs_s, *, cap, nblk):
    @pl.when(pl.program_id(1) == 0)
    def _():
        wg_s[...] = wg_ref[0, 0].astype(BF16)
        wu_s[...] = wu_ref[0, 0].astype(BF16)
        wd_s[...] = wd_ref[0, 0].astype(BF16)

    group, seq, _ = h_ref.shape
    tb = seq // nblk
    for rb in range(group):
        slot_row = slot_ref[rb, 0]
        base = rb * cap

        def gather_all():
            hit = lax.broadcasted_iota(jnp.int32, (cap, seq), 0) == slot_row
            xs_s[base:base + cap] = jnp.dot(jnp.where(hit, 1.0, 0.0).astype(BF16), h_ref[rb],
                                            preferred_element_type=F32)

        if nblk == 1:
            gather_all()
            continue
        outside = jnp.zeros((1, tb), F32)
        for j in range(nblk):
            lo, win = _band(j, cap, nblk)
            s_j = slot_row[:, j * tb:(j + 1) * tb]
            miss = jnp.logical_and(s_j >= 0, jnp.logical_or(s_j < lo, s_j >= lo + win))
            outside = outside + jnp.where(miss, 1.0, 0.0)
        n_outside = jnp.max(outside)

        @pl.when(n_outside == 0.0)
        def _():
            xs_s[base:base + cap] = jnp.zeros((cap, xs_s.shape[1]), F32)
            for j in range(nblk):
                lo, win = _band(j, cap, nblk)
                hit = (lax.broadcasted_iota(jnp.int32, (win, tb), 0) + lo) == slot_row[:, j * tb:(j + 1) * tb]
                xs_s[base + lo:base + lo + win] += jnp.dot(jnp.where(hit, 1.0, 0.0).astype(BF16),
                                                           h_ref[rb, j * tb:(j + 1) * tb],
                                                           preferred_element_type=F32)

        pl.when(n_outside != 0.0)(gather_all)

    xs = xs_s[...].astype(BF16)
    g = jnp.dot(xs, wg_s[...], preferred_element_type=F32)
    u = jnp.dot(xs, wu_s[...], preferred_element_type=F32)
    hid = (g * jax.nn.sigmoid(g) * u).astype(BF16)
    y = jnp.dot(hid, wd_s[...], preferred_element_type=F32)
    for rb in range(group):
        y_ref[rb, 0] = y[rb * cap:(rb + 1) * cap].astype(y_ref.dtype)


def _combine_kernel(slot_ref, aff_ref, ys_ref, x_ref, gate_ref, fg_ref, o_ref, *, cap, nblk, final):
    rows = x_ref.shape[1]
    n_exp = slot_ref.shape[2]
    slot = slot_ref[0]
    aff = aff_ref[0]

    def scatter(lo, win):
        cols = lax.broadcasted_iota(jnp.int32, (rows, win), 1) + lo
        moe = jnp.zeros(x_ref.shape[1:], F32)
        for e in range(n_exp):
            onehot = jnp.where(cols == slot[:, e:e + 1], 1.0, 0.0).astype(BF16)
            start = e * cap + lo
            start = start if isinstance(start, int) else pl.multiple_of(start, 16)
            moe = moe + aff[:, e:e + 1] * jnp.dot(onehot, ys_ref[0, pl.ds(start, win)], preferred_element_type=F32)
        out = x_ref[0] + gate_ref[0] * moe
        o_ref[0] = _rms(out, fg_ref[...]) if final else out

    if nblk == 1:
        scatter(0, cap)
        return
    lo, win = _band(pl.program_id(1), cap, nblk)
    miss = jnp.logical_and(slot >= 0, jnp.logical_or(slot < lo, slot >= lo + win))
    n_outside = jnp.max(jnp.where(miss, 1.0, 0.0))
    pl.when(n_outside == 0.0)(lambda: scatter(lo, win))
    pl.when(n_outside != 0.0)(lambda: scatter(0, cap))


MOE_VMEM_BYTES = 56 * 1024 * 1024
COMBINE_ROWS = 512
EXPERT_ROWS = 512
BF16_ROWS = 16


def _band_blocks(seq, cap):
    nblk = max(1, seq // COMBINE_ROWS)
    return nblk if nblk > 1 and (cap // nblk // 2) % BF16_ROWS == 0 else 1


def expert_choice_ffn(x, gate, h, aff_t, layer, w_gate, w_up, w_down, final_g, final):
    bsz, seq, d = h.shape
    _, n_exp, _, f = w_gate.shape
    cap = EC_FACTOR * seq // n_exp
    nblk = _band_blocks(seq, cap)
    slot = pl.pallas_call(
        functools.partial(_select_kernel, cap=float(cap)),
        out_shape=jax.ShapeDtypeStruct((bsz, n_exp, seq), jnp.int32),
        compiler_params=pltpu.CompilerParams(vmem_limit_bytes=MOE_VMEM_BYTES),
        name="moe_select",
    )(aff_t)
    group = min(bsz, max(1, EXPERT_ROWS // cap))
    weight = lambda k, n: pl.BlockSpec((1, 1, k, n), lambda e, b: (layer, e, 0, 0))
    ys = pl.pallas_call(
        functools.partial(_expert_kernel, cap=cap, nblk=nblk),
        grid=(n_exp, bsz // group),
        in_specs=[pl.BlockSpec((group, 1, 1, seq), lambda e, b: (b, e, 0, 0)),
                  pl.BlockSpec((group, seq, d), lambda e, b: (b, 0, 0)),
                  weight(d, f), weight(d, f), weight(f, d)],
        out_specs=pl.BlockSpec((group, 1, cap, d), lambda e, b: (b, e, 0, 0)),
        out_shape=jax.ShapeDtypeStruct((bsz, n_exp, cap, d), BF16),
        scratch_shapes=[pltpu.VMEM((d, f), BF16), pltpu.VMEM((d, f), BF16), pltpu.VMEM((f, d), BF16),
                        pltpu.VMEM((group * cap, d), F32)],
        compiler_params=pltpu.CompilerParams(dimension_semantics=("arbitrary", "arbitrary"),
                                             vmem_limit_bytes=MOE_VMEM_BYTES),
        name="moe_experts",
    )(slot.reshape(bsz, n_exp, 1, seq), h, w_gate, w_up, w_down)
    rows = seq // nblk if nblk > 1 else min(COMBINE_ROWS, seq)
    per_token = pl.BlockSpec((1, rows, n_exp), lambda b, i: (b, i, 0))
    return pl.pallas_call(
        functools.partial(_combine_kernel, cap=cap, nblk=nblk, final=final),
        grid=(bsz, seq // rows),
        in_specs=[per_token, per_token,
                  pl.BlockSpec((1, n_exp * cap, d), lambda b, i: (b, 0, 0)),
                  pl.BlockSpec((1, rows, d), lambda b, i: (b, i, 0)),
                  pl.BlockSpec((1, 1, d), lambda b, i: (b, 0, 0)),
                  pl.BlockSpec((1, d), lambda b, i: (0, 0))],
        out_specs=pl.BlockSpec((1, rows, d), lambda b, i: (b, i, 0)),
        out_shape=jax.ShapeDtypeStruct((bsz, seq, d), F32),
        compiler_params=pltpu.CompilerParams(dimension_semantics=("arbitrary", "arbitrary"),
                                             vmem_limit_bytes=MOE_VMEM_BYTES),
        name="moe_combine",
    )(jnp.swapaxes(slot, 1, 2), jnp.swapaxes(aff_t, 1, 2), ys.reshape(bsz, n_exp * cap, d), x, gate, _row(final_g))


def _mod_kernel(c_ref, w_ref, b_ref, o_ref):
    c = c_ref[...]
    w_hi, w_lo = _hi_lo(w_ref[0])
    o_ref[0] = _split_dot(c * jax.nn.sigmoid(c), w_hi, w_lo) + b_ref[0]


def modulation(cvecs, w_mod, b_mod):
    n, d = cvecs.shape
    depth = w_mod.shape[0]
    rows = -(-n // 8) * 8
    cpad = jnp.pad(cvecs, ((0, rows - n), (0, 0)))
    out = pl.pallas_call(
        _mod_kernel,
        grid=(depth, 6),
        in_specs=[pl.BlockSpec((rows, d), lambda l, n_: (0, 0)),
                  pl.BlockSpec((1, d, d), lambda l, n_: (l, 0, n_)),
                  pl.BlockSpec((1, 1, d), lambda l, n_: (l, 0, n_))],
        out_specs=pl.BlockSpec((1, rows, d), lambda l, n_: (l, 0, n_)),
        out_shape=jax.ShapeDtypeStruct((depth, rows, 6 * d), F32),
        compiler_params=pltpu.CompilerParams(dimension_semantics=("arbitrary", "arbitrary")),
        name="modulation",
    )(cpad, w_mod, b_mod.reshape(depth, 1, 6 * d))
    return out[:, :n].reshape(depth, n, 6, d)


TOKEN_TILE = 512
HALO = 16
LAYER_VMEM_BYTES = 48 * 1024 * 1024


def _rms(x, g):
    return x * lax.rsqrt(jnp.mean(x * x, axis=-1, keepdims=True) + RMS_EPS) * g


def _ln(x, g, b):
    mu = jnp.mean(x, axis=-1, keepdims=True)
    xc = x - mu
    return xc * lax.rsqrt(jnp.mean(xc * xc, axis=-1, keepdims=True) + LN_EPS) * g + b


def _split_dot(x, w_hi, w_lo):
    x_hi = x.astype(BF16)
    x_lo = (x - x_hi.astype(F32)).astype(BF16)
    dot = lambda a, b: jnp.dot(a, b, preferred_element_type=F32)
    return dot(x_hi, w_hi) + dot(x_lo, w_hi) + dot(x_hi, w_lo)


def _with_halo(x_ref, prev_ref, next_ref):
    return jnp.concatenate([prev_ref[0], x_ref[0], next_ref[0]], axis=0)


def _inside_mask(rows):
    i, n = pl.program_id(1), pl.num_programs(1)
    r = lax.broadcasted_iota(jnp.int32, (rows + 2 * HALO, 1), 0)
    return jnp.logical_and(jnp.logical_or(i > 0, r >= HALO), jnp.logical_or(i < n - 1, r < rows + HALO))


def _router(x1, mod_ref, nf_ref, wr_hi_ref, wr_lo_ref, br_ref, h2_ref, aff_ref):
    h2 = _rms(x1, nf_ref[...]) * (1.0 + mod_ref[0, 4:5]) + mod_ref[0, 3:4]
    h2_ref[0] = h2.astype(h2_ref.dtype)
    h_hi = h2.astype(BF16)
    h_lo = (h2 - h_hi.astype(F32)).astype(BF16)
    dot = lambda w, a: lax.dot_general(w, a, (_NT, ((), ())), preferred_element_type=F32)
    logits = dot(wr_hi_ref[...], h_hi) + dot(wr_hi_ref[...], h_lo) + dot(wr_lo_ref[...], h_hi) + br_ref[...]
    e = jnp.exp(logits - jnp.max(logits, axis=0, keepdims=True))
    aff_ref[0] = e / jnp.sum(e, axis=0, keepdims=True)


def _odd_kernel(x_ref, xp_ref, xn_ref, mod_ref, nm_ref, nf_ref, win_ref, cw_ref, cb_ref, cg_ref, cbb_ref,
                vg_ref, vb_ref, ws_ref, bs_ref, wout_ref, wr_hi_ref, wr_lo_ref, br_ref,
                x1_ref, h2_ref, aff_ref, glu_s):
    rows = x_ref.shape[1]
    inside = _inside_mask(rows)
    xa = _with_halo(x_ref, xp_ref, xn_ref)
    h = _rms(xa, nm_ref[...]) * (1.0 + mod_ref[0, 1:2]) + mod_ref[0, 0:1]
    h = jnp.where(inside, h, 0.0).astype(BF16)
    pc = jnp.dot(h, win_ref[:, 0:2 * D_C], preferred_element_type=F32)
    glu = pc[:, 0:D_C] * jax.nn.sigmoid(pc[:, D_C:2 * D_C])
    span = glu_s.shape[1]
    for r in range(SUBLANES):
        glu_s[r] = glu[r:r + span]
    acc = jnp.zeros((rows, D_C), F32)
    for j in range(CONV_WIDTH):
        off = HALO - CONV_WIDTH // 2 + j
        base = off // SUBLANES * SUBLANES
        acc = acc + cw_ref[j:j + 1, :] * glu_s[off % SUBLANES, base:base + rows, :]
    o_c = _ln(acc + cb_ref[...], cg_ref[...], cbb_ref[...])
    o_c = o_c * jax.nn.sigmoid(o_c)
    pd = jax.nn.gelu(jnp.dot(h[HALO:HALO + rows], win_ref[:, 2 * D_C:], preferred_element_type=F32))
    u = pd[:, 0:D_D]
    v = _ln(pd[:, D_D:], vg_ref[...], vb_ref[...]).astype(BF16)
    chunks = []
    for ck in range(rows // CHUNK):
        vc = v[ck * CHUNK:(ck + 1) * CHUNK]
        chunks.append(jnp.concatenate(
            [jnp.dot(ws_ref[hd], vc[:, hd * D_HEAD_DIM:(hd + 1) * D_HEAD_DIM], preferred_element_type=F32)
             for hd in range(D_HEADS)], axis=1) + bs_ref[...])
    o_d = u * jnp.concatenate(chunks, axis=0)
    mixed = jnp.dot(jnp.concatenate([o_c, o_d], axis=1).astype(BF16), wout_ref[...], preferred_element_type=F32)
    x1 = x_ref[0] + mod_ref[0, 2:3] * mixed
    x1_ref[0] = x1
    _router(x1, mod_ref, nf_ref, wr_hi_ref, wr_lo_ref, br_ref, h2_ref, aff_ref)


def _row(v):
    return v.reshape(1, -1)


def _hi_lo(w):
    hi = w.astype(BF16)
    return hi, (w - hi.astype(F32)).astype(BF16)


def _tile_specs(rows, seq, d):
    per = rows // HALO
    last = seq // HALO - 1
    return [pl.BlockSpec((1, rows, d), lambda b, i: (b, i, 0)),
            pl.BlockSpec((1, HALO, d), lambda b, i: (b, jnp.maximum(i * per - 1, 0), 0)),
            pl.BlockSpec((1, HALO, d), lambda b, i: (b, jnp.minimum((i + 1) * per, last), 0))]


def _shared_tile_specs(rows, seq, d):
    per = rows // HALO
    last = seq // HALO - 1
    return [pl.BlockSpec((rows, d), lambda b, i: (i, 0)),
            pl.BlockSpec((HALO, d), lambda b, i: (jnp.maximum(i * per - 1, 0), 0)),
            pl.BlockSpec((HALO, d), lambda b, i: (jnp.minimum((i + 1) * per, last), 0))]


def _full(a):
    return pl.BlockSpec(a.shape, lambda b, i: (0,) * a.ndim)


def odd_layer(x, mod, norm_mix, norm_ffn, w_in, conv_w, conv_b, cln_g, cln_b, vln_g, vln_b, w_s, b_s, w_out,
              w_router, b_router):
    bsz, seq, d = x.shape
    rows = min(TOKEN_TILE, seq)
    n_exp = w_router.shape[1]
    wr_hi, wr_lo = _hi_lo(w_router.T)
    bs_full = jnp.repeat(b_s.T, D_HEAD_DIM, axis=1)
    consts = [_row(norm_mix), _row(norm_ffn), w_in.astype(BF16), conv_w, _row(conv_b), _row(cln_g), _row(cln_b),
              _row(vln_g), _row(vln_b), w_s.astype(BF16), bs_full, w_out.astype(BF16), wr_hi, wr_lo,
              b_router.reshape(-1, 1)]
    tile = lambda w: pl.BlockSpec((1, rows, w), lambda b, i: (b, i, 0))
    return pl.pallas_call(
        _odd_kernel,
        grid=(bsz, seq // rows),
        in_specs=_tile_specs(rows, seq, d) + [pl.BlockSpec((1, 6, d), lambda b, i: (b, 0, 0))]
        + [_full(a) for a in consts],
        out_specs=[tile(d), tile(d), pl.BlockSpec((1, n_exp, rows), lambda b, i: (b, 0, i))],
        out_shape=[jax.ShapeDtypeStruct((bsz, seq, d), F32), jax.ShapeDtypeStruct((bsz, seq, d), BF16),
                   jax.ShapeDtypeStruct((bsz, n_exp, seq), F32)],
        scratch_shapes=[pltpu.VMEM((SUBLANES, rows + 2 * HALO - SUBLANES, D_C), F32)],
        compiler_params=pltpu.CompilerParams(dimension_semantics=("arbitrary", "arbitrary"),
                                             vmem_limit_bytes=LAYER_VMEM_BYTES),
        name="odd_layer",
    )(x, x, x, mod, *consts)


EVEN_TILE = 256
DECAY_SCALE = math.exp(-0.5)


def _head_sums(x, ones_bd):
    hi = x.astype(BF16)
    lo = (x - hi.astype(F32)).astype(BF16)
    dot = lambda a: jnp.dot(a, ones_bd, preferred_element_type=F32)
    return jnp.concatenate([dot(hi[:, g * LANES:(g + 1) * LANES]) + dot(lo[:, g * LANES:(g + 1) * LANES])
                            for g in range(x.shape[1] // LANES)], axis=1)


def _head_ones():
    head_shift = A_HEAD_DIM.bit_length() - 1
    same = ((lax.broadcasted_iota(jnp.int32, (LANES, LANES), 0) >> head_shift)
            == (lax.broadcasted_iota(jnp.int32, (LANES, LANES), 1) >> head_shift))
    return jnp.where(same, 1.0, 0.0).astype(BF16)


def _even_pre_kernel(x_ref, xp_ref, xn_ref, pos_ref, posp_ref, posn_ref, mod_ref, nm_ref, win_ref, mu_rkv_ref,
                     mu_wag_ref, w1_ref, w2_ref, w0_ref, a1_ref, a2_ref, a0_ref, g1_ref, g2_ref, kk_ref, ka_ref,
                     rk_ref, x_out, r_out, v_out, kk_out, lw_out, k_out, a_out, gate_out, bonus_out, u_out,
                     proj_s, h_s):
    rows = x_ref.shape[1]
    inside = _inside_mask(rows)
    xa = _with_halo(x_ref, xp_ref, xn_ref) + jnp.concatenate([posp_ref[...], pos_ref[...], posn_ref[...]], axis=0)
    x_out[0] = xa[HALO:HALO + rows]
    h = _rms(xa, nm_ref[...]) * (1.0 + mod_ref[0, 1:2]) + mod_ref[0, 0:1]
    h = jnp.where(inside, h, 0.0)
    h_s[...] = h
    proj = jnp.dot(h.astype(BF16), win_ref[...], preferred_element_type=F32)
    proj_s[...] = proj[:, 0:3 * D_A]
    u_out[0] = proj[HALO:HALO + rows, 3 * D_A:].astype(u_out.dtype)

    def shifted(ref, lo, hi):
        cur = ref[HALO:HALO + rows, lo:hi]
        return cur, 0.5 * (ref[HALO - 1:HALO - 1 + rows, lo:hi] + ref[HALO + 1:HALO + 1 + rows, lo:hi]) - cur

    r, dr = shifted(proj_s, 0, D_A)
    k, dk = shifted(proj_s, D_A, 2 * D_A)
    v, dv = shifted(proj_s, 2 * D_A, 3 * D_A)
    r = r + dr * mu_rkv_ref[0:1]
    k = k + dk * mu_rkv_ref[1:2]
    v = v + dv * mu_rkv_ref[2:3]
    hc, dh = shifted(h_s, 0, D_MODEL)
    xw = (hc + dh * mu_wag_ref[0:1]).astype(BF16)
    xa_ = (hc + dh * mu_wag_ref[1:2]).astype(BF16)
    xg = (hc + dh * mu_wag_ref[2:3]).astype(BF16)
    dot = lambda a, b: jnp.dot(a, b, preferred_element_type=F32)
    w_pre = w0_ref[...] + dot(jnp.tanh(dot(xw, w1_ref[...])).astype(BF16), w2_ref[...])
    icl = jax.nn.sigmoid(a0_ref[...] + dot(dot(xa_, a1_ref[...]).astype(BF16), a2_ref[...]))
    gate_out[0] = dot(jax.nn.sigmoid(dot(xg, g1_ref[...])).astype(BF16), g2_ref[...])
    ones_bd = _head_ones()
    kk = k * kk_ref[...]
    kk = kk / jnp.maximum(jnp.sqrt(_head_sums(kk * kk, ones_bd)), 1e-12)
    r_out[0] = r
    v_out[0] = v.astype(v_out.dtype)
    kk_out[0] = kk
    bonus_out[0] = _head_sums(r * k * rk_ref[...], ones_bd) * v
    for z in range(2):
        a_z = icl[:, z * D_A:(z + 1) * D_A]
        lw_out[z, 0] = -DECAY_SCALE * jax.nn.sigmoid(w_pre[:, z * D_A:(z + 1) * D_A])
        a_out[z, 0] = a_z
        k_out[z, 0] = k * (1.0 + (a_z - 1.0) * ka_ref[...])


def _block_diag2(w):
    z = jnp.zeros_like(w[0])
    return jnp.concatenate([jnp.concatenate([w[0], z], axis=1), jnp.concatenate([z, w[1]], axis=1)], axis=0)


def even_pre(x, pos, mod, norm_mix, w_in, mu_rkv, mu_wag, w0, w1, w2, a0, a1, a2, g1, g2, k_k, k_a, r_k):
    bsz, seq, d = x.shape
    rows = min(EVEN_TILE, seq)
    cat = lambda w: jnp.concatenate([w[0], w[1]], axis=1)
    consts = [_row(norm_mix), w_in.astype(BF16), mu_rkv, mu_wag,
              cat(w1).astype(BF16), _block_diag2(w2).astype(BF16), _row(w0),
              cat(a1).astype(BF16), _block_diag2(a2).astype(BF16), _row(a0),
              g1.astype(BF16), g2.astype(BF16), _row(k_k), _row(k_a), _row(r_k)]
    tile = lambda w: pl.BlockSpec((1, rows, w), lambda b, i: (b, i, 0))
    tile2 = pl.BlockSpec((2, 1, rows, D_A), lambda b, i: (0, b, i, 0))
    sds = lambda w, dt=F32: jax.ShapeDtypeStruct((bsz, seq, w), dt)
    sds2 = jax.ShapeDtypeStruct((2, bsz, seq, D_A), F32)
    return pl.pallas_call(
        _even_pre_kernel,
        grid=(bsz, seq // rows),
        in_specs=_tile_specs(rows, seq, d) + _shared_tile_specs(rows, seq, d)
        + [pl.BlockSpec((1, 6, d), lambda b, i: (b, 0, 0))] + [_full(a) for a in consts],
        out_specs=[tile(d), tile(D_A), tile(D_A), tile(D_A), tile2, tile2, tile2, tile(D_A), tile(D_A), tile(D_B)],
        out_shape=[sds(d), sds(D_A), sds(D_A, BF16), sds(D_A), sds2, sds2, sds2, sds(D_A), sds(D_A), sds(D_B, BF16)],
        scratch_shapes=[pltpu.VMEM((rows + 2 * HALO, 3 * D_A), F32), pltpu.VMEM((rows + 2 * HALO, d), F32)],
        compiler_params=pltpu.CompilerParams(dimension_semantics=("arbitrary", "arbitrary"),
                                             vmem_limit_bytes=LAYER_VMEM_BYTES),
        name="even_pre",
    )(x, x, x, pos, pos, pos, mod, *consts)


FOURIER_ROWS = 512
DFT_SPLIT = 64


def _fourier_kernel(u_ref, f64_ref, fl_ref, o_ref, ucs_s):
    seq = u_ref.shape[1]

    @pl.when(pl.program_id(1) == 0)
    def _():
        ucs = jnp.dot(u_ref[0], f64_ref[...], preferred_element_type=F32)
        ucs_s[0:seq] = ucs[:, 0:D_B].astype(BF16)
        ucs_s[seq:2 * seq] = ucs[:, D_B:].astype(BF16)

    scale = 1.0 / math.sqrt(seq * B_GROUP_DIM)
    o_ref[0] = (jnp.dot(fl_ref[...], ucs_s[...], preferred_element_type=F32) * scale).astype(o_ref.dtype)


def _dft_tables(seq):
    def cs(n):
        i = jnp.arange(n, dtype=jnp.int32)
        ang = ((i[:, None] * i[None, :]) % n).astype(F32) * (2.0 * math.pi / n)
        return jnp.cos(ang), jnp.sin(ang)
    c64, s64 = cs(B_GROUP_DIM)
    eye = jnp.eye(B_GROUPS, dtype=F32)
    f64 = jnp.concatenate([jnp.kron(eye, c64), jnp.kron(eye, s64)], axis=1)
    split = min(DFT_SPLIT, seq)
    s = jnp.arange(seq, dtype=jnp.int32)[None, :]
    ang = lambda t: ((t[:, None] * s) % seq).astype(F32) * (2.0 * math.pi / seq)
    ang_a = ang(jnp.arange(seq // split, dtype=jnp.int32) * split)
    ang_b = ang(jnp.arange(split, dtype=jnp.int32))
    ca, sa = jnp.cos(ang_a)[:, None, :], jnp.sin(ang_a)[:, None, :]
    cb, sb = jnp.cos(ang_b)[None, :, :], jnp.sin(ang_b)[None, :, :]
    cl = (ca * cb - sa * sb).reshape(seq, seq)
    sl = (sa * cb + ca * sb).reshape(seq, seq)
    return f64.astype(BF16), jnp.concatenate([cl, -sl], axis=1).astype(BF16)


def fourier_mixer(u):
    bsz, seq, _ = u.shape
    rows = min(FOURIER_ROWS, seq)
    f64, fl = _dft_tables(seq)
    return pl.pallas_call(
        _fourier_kernel,
        grid=(bsz, seq // rows),
        in_specs=[pl.BlockSpec((1, seq, D_B), lambda b, i: (b, 0, 0)),
                  pl.BlockSpec(f64.shape, lambda b, i: (0, 0)),
                  pl.BlockSpec((rows, 2 * seq), lambda b, i: (i, 0))],
        out_specs=pl.BlockSpec((1, rows, D_B), lambda b, i: (b, i, 0)),
        out_shape=jax.ShapeDtypeStruct((bsz, seq, D_B), BF16),
        scratch_shapes=[pltpu.VMEM((2 * seq, D_B), BF16)],
        compiler_params=pltpu.CompilerParams(dimension_semantics=("arbitrary", "arbitrary"),
                                             vmem_limit_bytes=LAYER_VMEM_BYTES),
        name="fourier_mixer",
    )(u, f64, fl)


def _even_post_kernel(yf_ref, yb_ref, bonus_ref, gate_ref, ob_ref, x_ref, mod_ref, gnw_ref, gnb_ref, wout_ref,
                      nf_ref, wr_hi_ref, wr_lo_ref, br_ref, x1_ref, h2_ref, aff_ref):
    ones_bd = _head_ones()
    y = yf_ref[0] + yb_ref[0]
    mu = _head_sums(y, ones_bd) * (1.0 / A_HEAD_DIM)
    yc = y - mu
    var = _head_sums(yc * yc, ones_bd) * (1.0 / A_HEAD_DIM)
    o_a = (yc * lax.rsqrt(var + GN_EPS) * gnw_ref[...] + gnb_ref[...] + bonus_ref[0]) * gate_ref[0]
    mixed = (jnp.dot(o_a.astype(BF16), wout_ref[0:D_A], preferred_element_type=F32)
             + jnp.dot(ob_ref[0], wout_ref[D_A:], preferred_element_type=F32))
    x1 = x_ref[0] + mod_ref[0, 2:3] * mixed
    x1_ref[0] = x1
    _router(x1, mod_ref, nf_ref, wr_hi_ref, wr_lo_ref, br_ref, h2_ref, aff_ref)


def even_post(yf, yb, bonus, gate, o_b, x, mod, gn_w, gn_b, w_out, norm_ffn, w_router, b_router):
    bsz, seq, d = x.shape
    rows = min(TOKEN_TILE, seq)
    n_exp = w_router.shape[1]
    wr_hi, wr_lo = _hi_lo(w_router.T)
    consts = [_row(gn_w), _row(gn_b), w_out.astype(BF16), _row(norm_ffn), wr_hi, wr_lo, b_router.reshape(-1, 1)]
    tile = lambda w: pl.BlockSpec((1, rows, w), lambda b, i: (b, i, 0))
    return pl.pallas_call(
        _even_post_kernel,
        grid=(bsz, seq // rows),
        in_specs=[tile(D_A), tile(D_A), tile(D_A), tile(D_A), tile(D_B), tile(d),
                  pl.BlockSpec((1, 6, d), lambda b, i: (b, 0, 0))] + [_full(a) for a in consts],
        out_specs=[tile(d), tile(d), pl.BlockSpec((1, n_exp, rows), lambda b, i: (b, 0, i))],
        out_shape=[jax.ShapeDtypeStruct((bsz, seq, d), F32), jax.ShapeDtypeStruct((bsz, seq, d), BF16),
                   jax.ShapeDtypeStruct((bsz, n_exp, seq), F32)],
        compiler_params=pltpu.CompilerParams(dimension_semantics=("arbitrary", "arbitrary"),
                                             vmem_limit_bytes=LAYER_VMEM_BYTES),
        name="even_post",
    )(yf, yb, bonus, gate, o_b, x, mod, *consts)


def run_trunk(x, pos, mods, s_init, P):
    states = []
    for l in range(DEPTH):
        j = l // 2
        mod = jnp.broadcast_to(mods[l], (x.shape[0], 6, x.shape[2]))
        if l % 2 == 0:
            x, r, v, kk, lw, kd, a, gate, bonus, u = even_pre(
                x, pos if l == 0 else jnp.zeros_like(pos), mod, P['norm_mix'][l], P['ev_w_in'][j], P['ev_mu_rkv'][j], P['ev_mu_wag'][j],
                P['ev_w0'][j], P['ev_w1'][j], P['ev_w2'][j], P['ev_a0'][j], P['ev_a1'][j], P['ev_a2'][j],
                P['ev_g1'][j], P['ev_g2'][j], P['ev_k_k'][j], P['ev_k_a'][j], P['ev_r_k'][j])
            yf, yb, s_fin = wkv7_scan_pallas(r, v, kk, lw, kd, a, s_init[:, j])
            states.append(s_fin)
            x1, h2, aff_t = even_post(yf, yb, bonus, gate, fourier_mixer(u), x, mod, P['ev_gn_w'][j], P['ev_gn_b'][j],
                                      P['ev_w_out'][j], P['norm_ffn'][l], P['moe_router'][l], P['moe_router_b'][l])
        else:
            x1, h2, aff_t = odd_layer(x, mod, P['norm_mix'][l], P['norm_ffn'][l], P['od_w_in'][j], P['od_conv_w'][j],
                                      P['od_conv_b'][j], P['od_cln_g'][j], P['od_cln_b'][j], P['od_vln_g'][j],
                                      P['od_vln_b'][j], P['od_w_s'][j], P['od_b_s'][j], P['od_w_out'][j],
                                      P['moe_router'][l], P['moe_router_b'][l])
        x = expert_choice_ffn(x1, mod[:, 5:6], h2, aff_t, l, P['moe_w_gate'], P['moe_w_up'], P['moe_w_down'],
                              P['final_norm'], l == DEPTH - 1)
    return x, jnp.stack(states, axis=1)


def kernel(x_prompt, x_sample, state_wkv, c, c_ctx, mod_w, mod_b, norm_mix, norm_ffn, final_norm,
           ev_w_in, ev_w_out, ev_mu_rkv, ev_mu_wag, ev_w0, ev_w1, ev_w2, ev_a0, ev_a1, ev_a2,
           ev_g1, ev_g2, ev_k_k, ev_k_a, ev_r_k, ev_gn_w, ev_gn_b,
           od_w_in, od_w_out, od_conv_w, od_conv_b, od_cln_g, od_cln_b, od_vln_g, od_vln_b,
           od_w_s, od_b_s, moe_router, moe_router_b, moe_w_gate, moe_w_up, moe_w_down):
    P = dict(mod_w=mod_w, mod_b=mod_b, norm_mix=norm_mix, norm_ffn=norm_ffn, final_norm=final_norm,
             ev_w_in=ev_w_in, ev_w_out=ev_w_out, ev_mu_rkv=ev_mu_rkv, ev_mu_wag=ev_mu_wag,
             ev_w0=ev_w0, ev_w1=ev_w1, ev_w2=ev_w2, ev_a0=ev_a0, ev_a1=ev_a1, ev_a2=ev_a2,
             ev_g1=ev_g1, ev_g2=ev_g2, ev_k_k=ev_k_k, ev_k_a=ev_k_a, ev_r_k=ev_r_k,
             ev_gn_w=ev_gn_w, ev_gn_b=ev_gn_b,
             od_w_in=od_w_in, od_w_out=od_w_out, od_conv_w=od_conv_w, od_conv_b=od_conv_b,
             od_cln_g=od_cln_g, od_cln_b=od_cln_b, od_vln_g=od_vln_g, od_vln_b=od_vln_b,
             od_w_s=od_w_s, od_b_s=od_b_s, moe_router=moe_router, moe_router_b=moe_router_b,
             moe_w_gate=moe_w_gate, moe_w_up=moe_w_up, moe_w_down=moe_w_down)
    n_even = state_wkv.shape[1]
    s_zero = jnp.zeros((x_prompt.shape[0], n_even, 2, A_HEADS, A_HEAD_DIM, A_HEAD_DIM), F32)
    mods = modulation(jnp.concatenate([c_ctx[None, :], c], axis=0), mod_w, mod_b)
    no_pos = jnp.zeros(x_prompt.shape[1:], x_prompt.dtype)
    y_prompt, new_state_wkv = run_trunk(x_prompt, no_pos, mods[:, 0:1], s_zero, P)
    y_sample, _ = run_trunk(x_sample, grid_posemb(x_sample.shape[1], x_sample.dtype), mods[:, 1:], state_wkv, P)
    return (y_prompt, y_sample, new_state_wkv)
```

```python
import functools
import math

import jax
import jax.numpy as jnp
from jax import lax
from jax.experimental import pallas as pl
from jax.experimental.pallas import tpu as pltpu

D_MODEL = 1024
DEPTH = 2
GRID_W = 64
POS_BASE = 10000.0
A_HEADS = 12
A_HEAD_DIM = 64
D_A = A_HEADS * A_HEAD_DIM
B_GROUPS = 4
B_GROUP_DIM = 64
D_B = B_GROUPS * B_GROUP_DIM
D_C = 512
CONV_WIDTH = 31
D_HEADS = 4
D_HEAD_DIM = 128
D_D = D_HEADS * D_HEAD_DIM
CHUNK = 128
N_EXPERTS = 16
EC_FACTOR = 2
RMS_EPS = 1e-6
LN_EPS = 1e-5
GN_EPS = 64e-5

LANES = 128
SUBLANES = 8
SCAN_CHUNK = 64
HEADS_PER_TILE = LANES // A_HEAD_DIM
N_HEAD_PAIRS = A_HEADS // HEADS_PER_TILE
PAIRS_PER_STEP = 6
SCAN_REQUESTS = 2

F32 = jnp.float32
BF16 = jnp.bfloat16


def _mm(a, b, dims):
    return lax.dot_general(a.astype(BF16), b.astype(BF16), (dims, ((), ())),
                           preferred_element_type=F32)


_NN = ((1,), (0,))
_NT = ((1,), (1,))
_TN = ((0,), (0,))


def _cumsum_rows(tri, x):
    x1 = x.astype(BF16)
    r1 = x - x1.astype(F32)
    x2 = r1.astype(BF16)
    x3 = (r1 - x2.astype(F32)).astype(BF16)
    dot = lambda y: lax.dot_general(tri, y, (_NN, ((), ())), preferred_element_type=F32)
    return dot(x1) + dot(x2) + dot(x3)


def _tile_block_diag(x, mask):
    n = x.shape[1] // x.shape[0]
    return jnp.where(mask, jnp.concatenate([x] * n, axis=0), 0.0)


def _unit_triangular_inverses(mats, same16, same32, bd_mask):
    c = mats[0].shape[0]
    bd = lambda x: _tile_block_diag(x, bd_mask)
    lane = lax.broadcasted_iota(jnp.int32, mats[0].shape, 1) & (c - 1)
    eye = (lax.broadcasted_iota(jnp.int32, mats[0].shape, 0) == lane).astype(F32)
    pws = [jnp.where(same16, -a, 0.0) for a in mats]
    ts = [eye + n for n in pws]
    for _ in range(3):
        pws = [_mm(pw, bd(pw), _NN) for pw in pws]
        ts = [t + _mm(t, bd(pw), _NN) for t, pw in zip(ts, pws)]
    in32 = jnp.logical_and(same32, jnp.logical_not(same16))
    tmp = [_mm(t, bd(jnp.where(in32, a, 0.0)), _NN) for t, a in zip(ts, mats)]
    ts = [t - _mm(x, bd(t), _NN) for t, x in zip(ts, tmp)]
    tmp = [_mm(t, bd(jnp.where(same32, 0.0, a)), _NN) for t, a in zip(ts, mats)]
    ts = [t - _mm(x, bd(t), _NN) for t, x in zip(ts, tmp)]
    return ts


INVERSE_GROUP = 2


def _wkv_chunks(items):
    c = items[0][0].shape[0]
    iota = lambda shape, axis: lax.broadcasted_iota(jnp.int32, shape, axis)
    t1, s1 = iota((c, c), 0), iota((c, c), 1)
    t2, s2 = iota((c, LANES), 0), iota((c, LANES), 1) & (c - 1)
    t4, s4 = iota((c, 2 * LANES), 0), iota((c, 2 * LANES), 1) & (c - 1)
    tri = {False: (t1 >= s1).astype(F32).astype(BF16), True: (t1 <= s1).astype(F32).astype(BF16)}
    strict2 = {False: t2 > s2, True: t2 < s2}
    incl4 = {False: t4 >= s4, True: t4 <= s4}
    wide = (c, INVERSE_GROUP * LANES)
    tw, sw = iota(wide, 0), iota(wide, 1) & (c - 1)
    same16 = (tw >> 4) == (sw >> 4)
    same32 = (tw >> 5) == (sw >> 5)
    head_shift = A_HEAD_DIM.bit_length() - 1
    bd_mask = lambda n: (iota((n, n), 0) >> head_shift) == (iota((n, n), 1) >> head_shift)
    bd2_mask, bdw_mask = bd_mask(LANES), bd_mask(wide[1])
    bd2 = lambda x: _tile_block_diag(x, bd2_mask)

    cs_all = [_cumsum_rows(tri[rev], lw) for (_, _, _, lw, _, _, _, rev) in items]
    prep = []
    for (r, v, kk, lw, k, a, s, rev), cs in zip(items, cs_all):
        cs_end = cs[0:1] if rev else cs[c - 1:c]
        b = kk * a
        g_inv = jnp.exp(-cs)
        g_tail = jnp.exp(cs_end - cs)
        kt = kk * jnp.exp(cs - lw)
        rt = r * jnp.exp(cs)
        lhs = jnp.concatenate([kt, rt], axis=0)
        kb_inv = jnp.concatenate([bd2(k * g_inv), bd2(b * g_inv)], axis=0)
        kb_tail = jnp.concatenate([k * g_tail, b * g_tail], axis=0)
        prep.append((lhs, kb_inv, kb_tail, jnp.exp(cs_end)))
    ps = [_mm(pr[0], it[6], _NT) for pr, it in zip(prep, items)]
    gs = [_mm(pr[0], pr[1], _NT) for pr in prep]
    a_kbs = [jnp.where(strict2[it[7]], g[0:c, LANES:2 * LANES], 0.0) for g, it in zip(gs, items)]
    groups = [jnp.concatenate(a_kbs[i:i + INVERSE_GROUP], axis=1) for i in range(0, len(items), INVERSE_GROUP)]
    t_groups = _unit_triangular_inverses(groups, same16, same32, bdw_mask)
    ts = [tg[:, j * LANES:(j + 1) * LANES] for tg in t_groups for j in range(INVERSE_GROUP)]
    ws = [p[0:c] + _mm(jnp.where(strict2[it[7]], g[0:c, 0:LANES], 0.0), bd2(it[1]), _NN)
          for p, g, it in zip(ps, gs, items)]
    us = [_mm(t, bd2(w), _NN) for t, w in zip(ts, ws)]
    ys = [p[c:2 * c] + _mm(jnp.where(incl4[it[7]], g[c:2 * c], 0.0),
                           jnp.concatenate([bd2(it[1]), bd2(-u)], axis=0), _NN)
          for p, g, it, u in zip(ps, gs, items, us)]
    dss = [_mm(jnp.concatenate([it[1], -u], axis=0), pr[2], _TN)
           for it, u, pr in zip(items, us, prep)]
    return [(y, it[6] * pr[3] + jnp.where(bd2_mask, ds, 0.0))
            for y, it, pr, ds in zip(ys, items, prep, dss)]


def _wkv_kernel(rf, vf, kkf, lwf, kf, af, rb, vb, kkb, lwb, kb, ab, s0_ref,
                yf_ref, yb_ref, s_ref):
    @pl.when(pl.program_id(2) == 0)
    def _():
        s_ref[...] = s0_ref[...]

    where = [(q, p) for q in range(SCAN_REQUESTS) for p in range(PAIRS_PER_STEP)]
    items = []
    for q, p in where:
        ln = slice(p * LANES, (p + 1) * LANES)
        items.append((rf[q, :, ln], vf[q, :, ln].astype(F32), kkf[q, :, ln], lwf[0, q, :, ln], kf[0, q, :, ln],
                      af[0, q, :, ln], s_ref[q, 0, p], False))
        items.append((rb[q, :, ln], vb[q, :, ln].astype(F32), kkb[q, :, ln], lwb[0, q, :, ln], kb[0, q, :, ln],
                      ab[0, q, :, ln], s_ref[q, 1, p], True))
    out = _wkv_chunks(items)
    for n, (q, p) in enumerate(where):
        ln = slice(p * LANES, (p + 1) * LANES)
        yf_ref[q, :, ln], s_ref[q, 0, p] = out[2 * n]
        yb_ref[q, :, ln], s_ref[q, 1, p] = out[2 * n + 1]


def wkv7_scan_pallas(r, v, kk, lw, k, a, s0):
    bsz, seq, _ = r.shape
    c = SCAN_CHUNK
    nc = seq // c
    n = A_HEAD_DIM
    s0p = s0.reshape(bsz, 2, N_HEAD_PAIRS, HEADS_PER_TILE, n, n)
    z = jnp.zeros_like(s0p[:, :, :, 0])
    s0bd = jnp.concatenate([jnp.concatenate([s0p[:, :, :, 0], z], axis=-1),
                            jnp.concatenate([z, s0p[:, :, :, 1]], axis=-1)], axis=-2)

    wd = PAIRS_PER_STEP * LANES
    nq = SCAN_REQUESTS
    shared_f = pl.BlockSpec((nq, c, wd), lambda b, p, i: (b, i, p))
    shared_b = pl.BlockSpec((nq, c, wd), lambda b, p, i: (b, nc - 1 - i, p))
    dir_f = pl.BlockSpec((1, nq, c, wd), lambda b, p, i: (0, b, i, p))
    dir_b = pl.BlockSpec((1, nq, c, wd), lambda b, p, i: (1, b, nc - 1 - i, p))
    st = pl.BlockSpec((nq, 2, PAIRS_PER_STEP, LANES, LANES), lambda b, p, i: (b, 0, p, 0, 0))
    yf, yb, sfin = pl.pallas_call(
        _wkv_kernel,
        grid=(bsz // SCAN_REQUESTS, N_HEAD_PAIRS // PAIRS_PER_STEP, nc),
        in_specs=[shared_f, shared_f, shared_f, dir_f, dir_f, dir_f,
                  shared_b, shared_b, shared_b, dir_b, dir_b, dir_b, st],
        out_specs=[shared_f, shared_b, st],
        out_shape=[jax.ShapeDtypeStruct((bsz, seq, D_A), F32),
                   jax.ShapeDtypeStruct((bsz, seq, D_A), F32),
                   jax.ShapeDtypeStruct((bsz, 2, N_HEAD_PAIRS, LANES, LANES), F32)],
        compiler_params=pltpu.CompilerParams(
            dimension_semantics=("arbitrary", "arbitrary", "arbitrary")),
        name="wkv7_scan",
    )(r, v, kk, lw, k, a, r, v, kk, lw, k, a, s0bd)
    s_fin = jnp.stack([sfin[..., :n, :n], sfin[..., n:, n:]], axis=3)
    return yf, yb, s_fin.reshape(bsz, 2, A_HEADS, n, n)


def grid_posemb(n_tokens, dtype):
    rows = n_tokens // GRID_W
    row = jnp.repeat(jnp.arange(rows, dtype=F32), GRID_W)
    col = jnp.tile(jnp.arange(GRID_W, dtype=F32), rows)
    quarter = D_MODEL // 4
    freq = 1.0 / (POS_BASE ** (jnp.arange(quarter, dtype=F32) / quarter))
    ar = row[:, None] * freq[None, :]
    ac = col[:, None] * freq[None, :]
    return jnp.concatenate([jnp.sin(ar), jnp.cos(ar), jnp.sin(ac), jnp.cos(ac)], axis=-1).astype(dtype)


def _exclusive_prefix(x):
    rows, seq = x.shape
    tri = jnp.where(lax.broadcasted_iota(jnp.int32, (LANES, LANES), 0)
                    <= lax.broadcasted_iota(jnp.int32, (LANES, LANES), 1), 1.0, 0.0).astype(BF16)
    carry = jnp.zeros((rows, 1), F32)
    out = []
    for blk in range(seq // LANES):
        xb = x[:, blk * LANES:(blk + 1) * LANES]
        inc = jnp.dot(xb.astype(BF16), tri, preferred_element_type=F32)
        out.append(inc - xb + carry)
        carry = carry + inc[:, LANES - 1:LANES]
    return jnp.concatenate(out, axis=1)


SELECT_STEPS = 80


def _select_kernel(aff_ref, slot_ref, *, cap):
    bsz, n_exp, seq = aff_ref.shape
    aff = aff_ref[...].reshape(bsz * n_exp, seq)
    count = lambda t: jnp.sum(jnp.where(aff >= t, 1.0, 0.0), axis=1, keepdims=True)
    lo = jnp.min(aff, axis=1, keepdims=True)
    hi = 2.0 * jnp.max(aff, axis=1, keepdims=True) + 1e-30

    def halve(_, bracket):
        lo, hi = bracket
        mid = lo + 0.5 * (hi - lo)
        ok = count(mid) >= cap
        return jnp.where(ok, mid, lo), jnp.where(ok, hi, mid)

    lo, hi = lax.fori_loop(0, SELECT_STEPS, halve, (lo, hi))
    above = jnp.where(aff >= hi, 1.0, 0.0)
    tied = jnp.where(aff >= lo, 1.0, 0.0) - above
    need = cap - jnp.sum(above, axis=1, keepdims=True)
    keep = above + tied * jnp.where(_exclusive_prefix(tied) < need, 1.0, 0.0)
    slot = _exclusive_prefix(keep)
    slot_ref[...] = jnp.where(keep > 0.5, slot, -1.0).astype(jnp.int32).reshape(bsz, n_exp, seq)


def _expert_kernel(slot_ref, aff_ref, h_ref, wg_ref, wu_ref, wd_ref, y_ref, wg_s, wu_s, wd_s, *, cap):
    @pl.when(pl.program_id(1) == 0)
    def _():
        wg_s[...] = wg_ref[0, 0].astype(BF16)
        wu_s[...] = wu_ref[0, 0].astype(BF16)
        wd_s[...] = wd_ref[0, 0].astype(BF16)

    group, seq, _ = h_ref.shape
    slot_iota = lax.broadcasted_iota(jnp.int32, (cap, seq), 0)
    xs, vals = [], []
    for rb in range(group):
        hit = slot_iota == slot_ref[rb, 0]
        xs.append(jnp.dot(jnp.where(hit, 1.0, 0.0).astype(BF16), h_ref[rb], preferred_element_type=F32))
        vals.append(jnp.sum(jnp.where(hit, aff_ref[rb, 0], 0.0), axis=1, keepdims=True))
    xs = jnp.concatenate(xs, axis=0).astype(BF16)
    g = jnp.dot(xs, wg_s[...], preferred_element_type=F32)
    u = jnp.dot(xs, wu_s[...], preferred_element_type=F32)
    hid = (g * jax.nn.sigmoid(g) * u).astype(BF16)
    y = jnp.dot(hid, wd_s[...], preferred_element_type=F32) * jnp.concatenate(vals, axis=0)
    for rb in range(group):
        y_ref[rb, 0] = y[rb * cap:(rb + 1) * cap].astype(y_ref.dtype)


def _combine_kernel(slot_ref, ys_ref, x_ref, gate_ref, fg_ref, o_ref, *, cap, final):
    rows = x_ref.shape[1]
    n_exp = slot_ref.shape[2]
    slot_iota = lax.broadcasted_iota(jnp.int32, (rows, cap), 1)
    moe = jnp.zeros(x_ref.shape[1:], F32)
    for e in range(n_exp):
        onehot = jnp.where(slot_iota == slot_ref[0, :, e:e + 1], 1.0, 0.0).astype(BF16)
        moe = moe + jnp.dot(onehot, ys_ref[0, e * cap:(e + 1) * cap], preferred_element_type=F32)
    out = x_ref[0] + gate_ref[0] * moe
    o_ref[0] = _rms(out, fg_ref[...]) if final else out


MOE_VMEM_BYTES = 56 * 1024 * 1024
COMBINE_ROWS = 512
EXPERT_ROWS = 512


def expert_choice_ffn(x, gate, h, aff_t, layer, w_gate, w_up, w_down, final_g, final):
    bsz, seq, d = h.shape
    _, n_exp, _, f = w_gate.shape
    cap = EC_FACTOR * seq // n_exp
    slot = pl.pallas_call(
        functools.partial(_select_kernel, cap=float(cap)),
        out_shape=jax.ShapeDtypeStruct((bsz, n_exp, seq), jnp.int32),
        compiler_params=pltpu.CompilerParams(vmem_limit_bytes=MOE_VMEM_BYTES),
        name="moe_select",
    )(aff_t)
    group = min(bsz, max(1, EXPERT_ROWS // cap))
    per_expert = lambda w: pl.BlockSpec((group, 1, 1, w), lambda e, b: (b, e, 0, 0))
    weight = lambda k, n: pl.BlockSpec((1, 1, k, n), lambda e, b: (layer, e, 0, 0))
    ys = pl.pallas_call(
        functools.partial(_expert_kernel, cap=cap),
        grid=(n_exp, bsz // group),
        in_specs=[per_expert(seq), per_expert(seq),
                  pl.BlockSpec((group, seq, d), lambda e, b: (b, 0, 0)),
                  weight(d, f), weight(d, f), weight(f, d)],
        out_specs=pl.BlockSpec((group, 1, cap, d), lambda e, b: (b, e, 0, 0)),
        out_shape=jax.ShapeDtypeStruct((bsz, n_exp, cap, d), BF16),
        scratch_shapes=[pltpu.VMEM((d, f), BF16), pltpu.VMEM((d, f), BF16), pltpu.VMEM((f, d), BF16)],
        compiler_params=pltpu.CompilerParams(dimension_semantics=("arbitrary", "arbitrary"),
                                             vmem_limit_bytes=MOE_VMEM_BYTES),
        name="moe_experts",
    )(slot.reshape(bsz, n_exp, 1, seq), aff_t.reshape(bsz, n_exp, 1, seq), h, w_gate, w_up, w_down)
    rows = min(COMBINE_ROWS, seq)
    return pl.pallas_call(
        functools.partial(_combine_kernel, cap=cap, final=final),
        grid=(bsz, seq // rows),
        in_specs=[pl.BlockSpec((1, rows, n_exp), lambda b, i: (b, i, 0)),
                  pl.BlockSpec((1, n_exp * cap, d), lambda b, i: (b, 0, 0)),
                  pl.BlockSpec((1, rows, d), lambda b, i: (b, i, 0)),
                  pl.BlockSpec((1, 1, d), lambda b, i: (b, 0, 0)),
                  pl.BlockSpec((1, d), lambda b, i: (0, 0))],
        out_specs=pl.BlockSpec((1, rows, d), lambda b, i: (b, i, 0)),
        out_shape=jax.ShapeDtypeStruct((bsz, seq, d), F32),
        compiler_params=pltpu.CompilerParams(dimension_semantics=("arbitrary", "arbitrary"),
                                             vmem_limit_bytes=MOE_VMEM_BYTES),
        name="moe_combine",
    )(jnp.swapaxes(slot, 1, 2), ys.reshape(bsz, n_exp * cap, d), x, gate, _row(final_g))


def _mod_kernel(c_ref, w_ref, b_ref, o_ref):
    c = c_ref[...]
    w_hi, w_lo = _hi_lo(w_ref[0])
    o_ref[0] = _split_dot(c * jax.nn.sigmoid(c), w_hi, w_lo) + b_ref[0]


def modulation(cvecs, w_mod, b_mod):
    n, d = cvecs.shape
    depth = w_mod.shape[0]
    rows = -(-n // 8) * 8
    cpad = jnp.pad(cvecs, ((0, rows - n), (0, 0)))
    out = pl.pallas_call(
        _mod_kernel,
        grid=(depth, 6),
        in_specs=[pl.BlockSpec((rows, d), lambda l, n_: (0, 0)),
                  pl.BlockSpec((1, d, d), lambda l, n_: (l, 0, n_)),
                  pl.BlockSpec((1, 1, d), lambda l, n_: (l, 0, n_))],
        out_specs=pl.BlockSpec((1, rows, d), lambda l, n_: (l, 0, n_)),
        out_shape=jax.ShapeDtypeStruct((depth, rows, 6 * d), F32),
        compiler_params=pltpu.CompilerParams(dimension_semantics=("arbitrary", "arbitrary")),
        name="modulation",
    )(cpad, w_mod, b_mod.reshape(depth, 1, 6 * d))
    return out[:, :n].reshape(depth, n, 6, d)


TOKEN_TILE = 512
HALO = 16
LAYER_VMEM_BYTES = 48 * 1024 * 1024


def _rms(x, g):
    return x * lax.rsqrt(jnp.mean(x * x, axis=-1, keepdims=True) + RMS_EPS) * g


def _ln(x, g, b):
    mu = jnp.mean(x, axis=-1, keepdims=True)
    xc = x - mu
    return xc * lax.rsqrt(jnp.mean(xc * xc, axis=-1, keepdims=True) + LN_EPS) * g + b


def _split_dot(x, w_hi, w_lo):
    x_hi = x.astype(BF16)
    x_lo = (x - x_hi.astype(F32)).astype(BF16)
    dot = lambda a, b: jnp.dot(a, b, preferred_element_type=F32)
    return dot(x_hi, w_hi) + dot(x_lo, w_hi) + dot(x_hi, w_lo)


def _with_halo(x_ref, prev_ref, next_ref):
    return jnp.concatenate([prev_ref[0], x_ref[0], next_ref[0]], axis=0)


def _inside_mask(rows):
    i, n = pl.program_id(1), pl.num_programs(1)
    r = lax.broadcasted_iota(jnp.int32, (rows + 2 * HALO, 1), 0)
    return jnp.logical_and(jnp.logical_or(i > 0, r >= HALO), jnp.logical_or(i < n - 1, r < rows + HALO))


def _router(x1, mod_ref, nf_ref, wr_hi_ref, wr_lo_ref, br_ref, h2_ref, aff_ref):
    h2 = _rms(x1, nf_ref[...]) * (1.0 + mod_ref[0, 4:5]) + mod_ref[0, 3:4]
    h2_ref[0] = h2.astype(h2_ref.dtype)
    h_hi = h2.astype(BF16)
    h_lo = (h2 - h_hi.astype(F32)).astype(BF16)
    dot = lambda w, a: lax.dot_general(w, a, (_NT, ((), ())), preferred_element_type=F32)
    logits = dot(wr_hi_ref[...], h_hi) + dot(wr_hi_ref[...], h_lo) + dot(wr_lo_ref[...], h_hi) + br_ref[...]
    e = jnp.exp(logits - jnp.max(logits, axis=0, keepdims=True))
    aff_ref[0] = e / jnp.sum(e, axis=0, keepdims=True)


def _odd_kernel(x_ref, xp_ref, xn_ref, mod_ref, nm_ref, nf_ref, win_ref, cw_ref, cb_ref, cg_ref, cbb_ref,
                vg_ref, vb_ref, ws_ref, bs_ref, wout_ref, wr_hi_ref, wr_lo_ref, br_ref,
                x1_ref, h2_ref, aff_ref, glu_s):
    rows = x_ref.shape[1]
    inside = _inside_mask(rows)
    xa = _with_halo(x_ref, xp_ref, xn_ref)
    h = _rms(xa, nm_ref[...]) * (1.0 + mod_ref[0, 1:2]) + mod_ref[0, 0:1]
    h = jnp.where(inside, h, 0.0).astype(BF16)
    pc = jnp.dot(h, win_ref[:, 0:2 * D_C], preferred_element_type=F32)
    glu = pc[:, 0:D_C] * jax.nn.sigmoid(pc[:, D_C:2 * D_C])
    span = glu_s.shape[1]
    for r in range(SUBLANES):
        glu_s[r] = glu[r:r + span]
    acc = jnp.zeros((rows, D_C), F32)
    for j in range(CONV_WIDTH):
        off = HALO - CONV_WIDTH // 2 + j
        base = off // SUBLANES * SUBLANES
        acc = acc + cw_ref[j:j + 1, :] * glu_s[off % SUBLANES, base:base + rows, :]
    o_c = _ln(acc + cb_ref[...], cg_ref[...], cbb_ref[...])
    o_c = o_c * jax.nn.sigmoid(o_c)
    pd = jax.nn.gelu(jnp.dot(h[HALO:HALO + rows], win_ref[:, 2 * D_C:], preferred_element_type=F32))
    u = pd[:, 0:D_D]
    v = _ln(pd[:, D_D:], vg_ref[...], vb_ref[...]).astype(BF16)
    chunks = []
    for ck in range(rows // CHUNK):
        vc = v[ck * CHUNK:(ck + 1) * CHUNK]
        chunks.append(jnp.concatenate(
            [jnp.dot(ws_ref[hd], vc[:, hd * D_HEAD_DIM:(hd + 1) * D_HEAD_DIM], preferred_element_type=F32)
             for hd in range(D_HEADS)], axis=1) + bs_ref[...])
    o_d = u * jnp.concatenate(chunks, axis=0)
    mixed = jnp.dot(jnp.concatenate([o_c, o_d], axis=1).astype(BF16), wout_ref[...], preferred_element_type=F32)
    x1 = x_ref[0] + mod_ref[0, 2:3] * mixed
    x1_ref[0] = x1
    _router(x1, mod_ref, nf_ref, wr_hi_ref, wr_lo_ref, br_ref, h2_ref, aff_ref)


def _row(v):
    return v.reshape(1, -1)


def _hi_lo(w):
    hi = w.astype(BF16)
    return hi, (w - hi.astype(F32)).astype(BF16)


def _tile_specs(rows, seq, d):
    per = rows // HALO
    last = seq // HALO - 1
    return [pl.BlockSpec((1, rows, d), lambda b, i: (b, i, 0)),
            pl.BlockSpec((1, HALO, d), lambda b, i: (b, jnp.maximum(i * per - 1, 0), 0)),
            pl.BlockSpec((1, HALO, d), lambda b, i: (b, jnp.minimum((i + 1) * per, last), 0))]


def _shared_tile_specs(rows, seq, d):
    per = rows // HALO
    last = seq // HALO - 1
    return [pl.BlockSpec((rows, d), lambda b, i: (i, 0)),
            pl.BlockSpec((HALO, d), lambda b, i: (jnp.maximum(i * per - 1, 0), 0)),
            pl.BlockSpec((HALO, d), lambda b, i: (jnp.minimum((i + 1) * per, last), 0))]


def _full(a):
    return pl.BlockSpec(a.shape, lambda b, i: (0,) * a.ndim)


def odd_layer(x, mod, norm_mix, norm_ffn, w_in, conv_w, conv_b, cln_g, cln_b, vln_g, vln_b, w_s, b_s, w_out,
              w_router, b_router):
    bsz, seq, d = x.shape
    rows = min(TOKEN_TILE, seq)
    n_exp = w_router.shape[1]
    wr_hi, wr_lo = _hi_lo(w_router.T)
    bs_full = jnp.repeat(b_s.T, D_HEAD_DIM, axis=1)
    consts = [_row(norm_mix), _row(norm_ffn), w_in.astype(BF16), conv_w, _row(conv_b), _row(cln_g), _row(cln_b),
              _row(vln_g), _row(vln_b), w_s.astype(BF16), bs_full, w_out.astype(BF16), wr_hi, wr_lo,
              b_router.reshape(-1, 1)]
    tile = lambda w: pl.BlockSpec((1, rows, w), lambda b, i: (b, i, 0))
    return pl.pallas_call(
        _odd_kernel,
        grid=(bsz, seq // rows),
        in_specs=_tile_specs(rows, seq, d) + [pl.BlockSpec((1, 6, d), lambda b, i: (b, 0, 0))]
        + [_full(a) for a in consts],
        out_specs=[tile(d), tile(d), pl.BlockSpec((1, n_exp, rows), lambda b, i: (b, 0, i))],
        out_shape=[jax.ShapeDtypeStruct((bsz, seq, d), F32), jax.ShapeDtypeStruct((bsz, seq, d), BF16),
                   jax.ShapeDtypeStruct((bsz, n_exp, seq), F32)],
        scratch_shapes=[pltpu.VMEM((SUBLANES, rows + 2 * HALO - SUBLANES, D_C), F32)],
        compiler_params=pltpu.CompilerParams(dimension_semantics=("arbitrary", "arbitrary"),
                                             vmem_limit_bytes=LAYER_VMEM_BYTES),
        name="odd_layer",
    )(x, x, x, mod, *consts)


EVEN_TILE = 256
DECAY_SCALE = math.exp(-0.5)


def _head_sums(x, ones_bd):
    hi = x.astype(BF16)
    lo = (x - hi.astype(F32)).astype(BF16)
    dot = lambda a: jnp.dot(a, ones_bd, preferred_element_type=F32)
    return jnp.concatenate([dot(hi[:, g * LANES:(g + 1) * LANES]) + dot(lo[:, g * LANES:(g + 1) * LANES])
                            for g in range(x.shape[1] // LANES)], axis=1)


def _head_ones():
    head_shift = A_HEAD_DIM.bit_length() - 1
    same = ((lax.broadcasted_iota(jnp.int32, (LANES, LANES), 0) >> head_shift)
            == (lax.broadcasted_iota(jnp.int32, (LANES, LANES), 1) >> head_shift))
    return jnp.where(same, 1.0, 0.0).astype(BF16)


def _even_pre_kernel(x_ref, xp_ref, xn_ref, pos_ref, posp_ref, posn_ref, mod_ref, nm_ref, win_ref, mu_rkv_ref,
                     mu_wag_ref, w1_ref, w2_ref, w0_ref, a1_ref, a2_ref, a0_ref, g1_ref, g2_ref, kk_ref, ka_ref,
                     rk_ref, x_out, r_out, v_out, kk_out, lw_out, k_out, a_out, gate_out, bonus_out, u_out,
                     proj_s, h_s):
    rows = x_ref.shape[1]
    inside = _inside_mask(rows)
    xa = _with_halo(x_ref, xp_ref, xn_ref) + jnp.concatenate([posp_ref[...], pos_ref[...], posn_ref[...]], axis=0)
    x_out[0] = xa[HALO:HALO + rows]
    h = _rms(xa, nm_ref[...]) * (1.0 + mod_ref[0, 1:2]) + mod_ref[0, 0:1]
    h = jnp.where(inside, h, 0.0)
    h_s[...] = h
    proj = jnp.dot(h.astype(BF16), win_ref[...], preferred_element_type=F32)
    proj_s[...] = proj[:, 0:3 * D_A]
    u_out[0] = proj[HALO:HALO + rows, 3 * D_A:].astype(u_out.dtype)

    def shifted(ref, lo, hi):
        cur = ref[HALO:HALO + rows, lo:hi]
        return cur, 0.5 * (ref[HALO - 1:HALO - 1 + rows, lo:hi] + ref[HALO + 1:HALO + 1 + rows, lo:hi]) - cur

    r, dr = shifted(proj_s, 0, D_A)
    k, dk = shifted(proj_s, D_A, 2 * D_A)
    v, dv = shifted(proj_s, 2 * D_A, 3 * D_A)
    r = r + dr * mu_rkv_ref[0:1]
    k = k + dk * mu_rkv_ref[1:2]
    v = v + dv * mu_rkv_ref[2:3]
    hc, dh = shifted(h_s, 0, D_MODEL)
    xw = (hc + dh * mu_wag_ref[0:1]).astype(BF16)
    xa_ = (hc + dh * mu_wag_ref[1:2]).astype(BF16)
    xg = (hc + dh * mu_wag_ref[2:3]).astype(BF16)
    dot = lambda a, b: jnp.dot(a, b, preferred_element_type=F32)
    w_pre = w0_ref[...] + dot(jnp.tanh(dot(xw, w1_ref[...])).astype(BF16), w2_ref[...])
    icl = jax.nn.sigmoid(a0_ref[...] + dot(dot(xa_, a1_ref[...]).astype(BF16), a2_ref[...]))
    gate_out[0] = dot(jax.nn.sigmoid(dot(xg, g1_ref[...])).astype(BF16), g2_ref[...])
    ones_bd = _head_ones()
    kk = k * kk_ref[...]
    kk = kk / jnp.maximum(jnp.sqrt(_head_sums(kk * kk, ones_bd)), 1e-12)
    r_out[0] = r
    v_out[0] = v.astype(v_out.dtype)
    kk_out[0] = kk
    bonus_out[0] = _head_sums(r * k * rk_ref[...], ones_bd) * v
    for z in range(2):
        a_z = icl[:, z * D_A:(z + 1) * D_A]
        lw_out[z, 0] = -DECAY_SCALE * jax.nn.sigmoid(w_pre[:, z * D_A:(z + 1) * D_A])
        a_out[z, 0] = a_z
        k_out[z, 0] = k * (1.0 + (a_z - 1.0) * ka_ref[...])


def _block_diag2(w):
    z = jnp.zeros_like(w[0])
    return jnp.concatenate([jnp.concatenate([w[0], z], axis=1), jnp.concatenate([z, w[1]], axis=1)], axis=0)


def even_pre(x, pos, mod, norm_mix, w_in, mu_rkv, mu_wag, w0, w1, w2, a0, a1, a2, g1, g2, k_k, k_a, r_k):
    bsz, seq, d = x.shape
    rows = min(EVEN_TILE, seq)
    cat = lambda w: jnp.concatenate([w[0], w[1]], axis=1)
    consts = [_row(norm_mix), w_in.astype(BF16), mu_rkv, mu_wag,
              cat(w1).astype(BF16), _block_diag2(w2).astype(BF16), _row(w0),
              cat(a1).astype(BF16), _block_diag2(a2).astype(BF16), _row(a0),
              g1.astype(BF16), g2.astype(BF16), _row(k_k), _row(k_a), _row(r_k)]
    tile = lambda w: pl.BlockSpec((1, rows, w), lambda b, i: (b, i, 0))
    tile2 = pl.BlockSpec((2, 1, rows, D_A), lambda b, i: (0, b, i, 0))
    sds = lambda w, dt=F32: jax.ShapeDtypeStruct((bsz, seq, w), dt)
    sds2 = jax.ShapeDtypeStruct((2, bsz, seq, D_A), F32)
    return pl.pallas_call(
        _even_pre_kernel,
        grid=(bsz, seq // rows),
        in_specs=_tile_specs(rows, seq, d) + _shared_tile_specs(rows, seq, d)
        + [pl.BlockSpec((1, 6, d), lambda b, i: (b, 0, 0))] + [_full(a) for a in consts],
        out_specs=[tile(d), tile(D_A), tile(D_A), tile(D_A), tile2, tile2, tile2, tile(D_A), tile(D_A), tile(D_B)],
        out_shape=[sds(d), sds(D_A), sds(D_A, BF16), sds(D_A), sds2, sds2, sds2, sds(D_A), sds(D_A), sds(D_B, BF16)],
        scratch_shapes=[pltpu.VMEM((rows + 2 * HALO, 3 * D_A), F32), pltpu.VMEM((rows + 2 * HALO, d), F32)],
        compiler_params=pltpu.CompilerParams(dimension_semantics=("arbitrary", "arbitrary"),
                                             vmem_limit_bytes=LAYER_VMEM_BYTES),
        name="even_pre",
    )(x, x, x, pos, pos, pos, mod, *consts)


FOURIER_ROWS = 512
DFT_SPLIT = 64


def _fourier_kernel(u_ref, f64_ref, fl_ref, o_ref, ucs_s):
    seq = u_ref.shape[1]

    @pl.when(pl.program_id(1) == 0)
    def _():
        ucs = jnp.dot(u_ref[0], f64_ref[...], preferred_element_type=F32)
        ucs_s[0:seq] = ucs[:, 0:D_B].astype(BF16)
        ucs_s[seq:2 * seq] = ucs[:, D_B:].astype(BF16)

    scale = 1.0 / math.sqrt(seq * B_GROUP_DIM)
    o_ref[0] = (jnp.dot(fl_ref[...], ucs_s[...], preferred_element_type=F32) * scale).astype(o_ref.dtype)


def _dft_tables(seq):
    def cs(n):
        i = jnp.arange(n, dtype=jnp.int32)
        ang = ((i[:, None] * i[None, :]) % n).astype(F32) * (2.0 * math.pi / n)
        return jnp.cos(ang), jnp.sin(ang)
    c64, s64 = cs(B_GROUP_DIM)
    eye = jnp.eye(B_GROUPS, dtype=F32)
    f64 = jnp.concatenate([jnp.kron(eye, c64), jnp.kron(eye, s64)], axis=1)
    split = min(DFT_SPLIT, seq)
    s = jnp.arange(seq, dtype=jnp.int32)[None, :]
    ang = lambda t: ((t[:, None] * s) % seq).astype(F32) * (2.0 * math.pi / seq)
    ang_a = ang(jnp.arange(seq // split, dtype=jnp.int32) * split)
    ang_b = ang(jnp.arange(split, dtype=jnp.int32))
    ca, sa = jnp.cos(ang_a)[:, None, :], jnp.sin(ang_a)[:, None, :]
    cb, sb = jnp.cos(ang_b)[None, :, :], jnp.sin(ang_b)[None, :, :]
    cl = (ca * cb - sa * sb).reshape(seq, seq)
    sl = (sa * cb + ca * sb).reshape(seq, seq)
    return f64.astype(BF16), jnp.concatenate([cl, -sl], axis=1).astype(BF16)


def fourier_mixer(u):
    bsz, seq, _ = u.shape
    rows = min(FOURIER_ROWS, seq)
    f64, fl = _dft_tables(seq)
    return pl.pallas_call(
        _fourier_kernel,
        grid=(bsz, seq // rows),
        in_specs=[pl.BlockSpec((1, seq, D_B), lambda b, i: (b, 0, 0)),
                  pl.BlockSpec(f64.shape, lambda b, i: (0, 0)),
                  pl.BlockSpec((rows, 2 * seq), lambda b, i: (i, 0))],
        out_specs=pl.BlockSpec((1, rows, D_B), lambda b, i: (b, i, 0)),
        out_shape=jax.ShapeDtypeStruct((bsz, seq, D_B), BF16),
        scratch_shapes=[pltpu.VMEM((2 * seq, D_B), BF16)],
        compiler_params=pltpu.CompilerParams(dimension_semantics=("arbitrary", "arbitrary"),
                                             vmem_limit_bytes=LAYER_VMEM_BYTES),
        name="fourier_mixer",
    )(u, f64, fl)


def _even_post_kernel(yf_ref, yb_ref, bonus_ref, gate_ref, ob_ref, x_ref, mod_ref, gnw_ref, gnb_ref, wout_ref,
                      nf_ref, wr_hi_ref, wr_lo_ref, br_ref, x1_ref, h2_ref, aff_ref):
    ones_bd = _head_ones()
    y = yf_ref[0] + yb_ref[0]
    mu = _head_sums(y, ones_bd) * (1.0 / A_HEAD_DIM)
    yc = y - mu
    var = _head_sums(yc * yc, ones_bd) * (1.0 / A_HEAD_DIM)
    o_a = (yc * lax.rsqrt(var + GN_EPS) * gnw_ref[...] + gnb_ref[...] + bonus_ref[0]) * gate_ref[0]
    mixed = (jnp.dot(o_a.astype(BF16), wout_ref[0:D_A], preferred_element_type=F32)
             + jnp.dot(ob_ref[0], wout_ref[D_A:], preferred_element_type=F32))
    x1 = x_ref[0] + mod_ref[0, 2:3] * mixed
    x1_ref[0] = x1
    _router(x1, mod_ref, nf_ref, wr_hi_ref, wr_lo_ref, br_ref, h2_ref, aff_ref)


def even_post(yf, yb, bonus, gate, o_b, x, mod, gn_w, gn_b, w_out, norm_ffn, w_router, b_router):
    bsz, seq, d = x.shape
    rows = min(TOKEN_TILE, seq)
    n_exp = w_router.shape[1]
    wr_hi, wr_lo = _hi_lo(w_router.T)
    consts = [_row(gn_w), _row(gn_b), w_out.astype(BF16), _row(norm_ffn), wr_hi, wr_lo, b_router.reshape(-1, 1)]
    tile = lambda w: pl.BlockSpec((1, rows, w), lambda b, i: (b, i, 0))
    return pl.pallas_call(
        _even_post_kernel,
        grid=(bsz, seq // rows),
        in_specs=[tile(D_A), tile(D_A), tile(D_A), tile(D_A), tile(D_B), tile(d),
                  pl.BlockSpec((1, 6, d), lambda b, i: (b, 0, 0))] + [_full(a) for a in consts],
        out_specs=[tile(d), tile(d), pl.BlockSpec((1, n_exp, rows), lambda b, i: (b, 0, i))],
        out_shape=[jax.ShapeDtypeStruct((bsz, seq, d), F32), jax.ShapeDtypeStruct((bsz, seq, d), BF16),
                   jax.ShapeDtypeStruct((bsz, n_exp, seq), F32)],
        compiler_params=pltpu.CompilerParams(dimension_semantics=("arbitrary", "arbitrary"),
                                             vmem_limit_bytes=LAYER_VMEM_BYTES),
        name="even_post",
    )(yf, yb, bonus, gate, o_b, x, mod, *consts)


def run_trunk(x, pos, mods, s_init, P):
    states = []
    for l in range(DEPTH):
        j = l // 2
        mod = jnp.broadcast_to(mods[l], (x.shape[0], 6, x.shape[2]))
        if l % 2 == 0:
            x, r, v, kk, lw, kd, a, gate, bonus, u = even_pre(
                x, pos if l == 0 else jnp.zeros_like(pos), mod, P['norm_mix'][l], P['ev_w_in'][j], P['ev_mu_rkv'][j], P['ev_mu_wag'][j],
                P['ev_w0'][j], P['ev_w1'][j], P['ev_w2'][j], P['ev_a0'][j], P['ev_a1'][j], P['ev_a2'][j],
                P['ev_g1'][j], P['ev_g2'][j], P['ev_k_k'][j], P['ev_k_a'][j], P['ev_r_k'][j])
            yf, yb, s_fin = wkv7_scan_pallas(r, v, kk, lw, kd, a, s_init[:, j])
            states.append(s_fin)
            x1, h2, aff_t = even_post(yf, yb, bonus, gate, fourier_mixer(u), x, mod, P['ev_gn_w'][j], P['ev_gn_b'][j],
                                      P['ev_w_out'][j], P['norm_ffn'][l], P['moe_router'][l], P['moe_router_b'][l])
        else:
            x1, h2, aff_t = odd_layer(x, mod, P['norm_mix'][l], P['norm_ffn'][l], P['od_w_in'][j], P['od_conv_w'][j],
                                      P['od_conv_b'][j], P['od_cln_g'][j], P['od_cln_b'][j], P['od_vln_g'][j],
                                      P['od_vln_b'][j], P['od_w_s'][j], P['od_b_s'][j], P['od_w_out'][j],
                                      P['moe_router'][l], P['moe_router_b'][l])
        x = expert_choice_ffn(x1, mod[:, 5:6], h2, aff_t, l, P['moe_w_gate'], P['moe_w_up'], P['moe_w_down'],
                              P['final_norm'], l == DEPTH - 1)
    return x, jnp.stack(states, axis=1)


def kernel(x_prompt, x_sample, state_wkv, c, c_ctx, mod_w, mod_b, norm_mix, norm_ffn, final_norm,
           ev_w_in, ev_w_out, ev_mu_rkv, ev_mu_wag, ev_w0, ev_w1, ev_w2, ev_a0, ev_a1, ev_a2,
           ev_g1, ev_g2, ev_k_k, ev_k_a, ev_r_k, ev_gn_w, ev_gn_b,
           od_w_in, od_w_out, od_conv_w, od_conv_b, od_cln_g, od_cln_b, od_vln_g, od_vln_b,
           od_w_s, od_b_s, moe_router, moe_router_b, moe_w_gate, moe_w_up, moe_w_down):
    P = dict(mod_w=mod_w, mod_b=mod_b, norm_mix=norm_mix, norm_ffn=norm_ffn, final_norm=final_norm,
             ev_w_in=ev_w_in, ev_w_out=ev_w_out, ev_mu_rkv=ev_mu_rkv, ev_mu_wag=ev_mu_wag,
             ev_w0=ev_w0, ev_w1=ev_w1, ev_w2=ev_w2, ev_a0=ev_a0, ev_a1=ev_a1, ev_a2=ev_a2,
             ev_g1=ev_g1, ev_g2=ev_g2, ev_k_k=ev_k_k, ev_k_a=ev_k_a, ev_r_k=ev_r_k,
             ev_gn_w=ev_gn_w, ev_gn_b=ev_gn_b,
             od_w_in=od_w_in, od_w_out=od_w_out, od_conv_w=od_conv_w, od_conv_b=od_conv_b,
             od_cln_g=od_cln_g, od_cln_b=od_cln_b, od_vln_g=od_vln_g, od_vln_b=od_vln_b,
             od_w_s=od_w_s, od_b_s=od_b_s, moe_router=moe_router, moe_router_b=moe_router_b,
             moe_w_gate=moe_w_gate, moe_w_up=moe_w_up, moe_w_down=moe_w_down)
    n_even = state_wkv.shape[1]
    s_zero = jnp.zeros((x_prompt.shape[0], n_even, 2, A_HEADS, A_HEAD_DIM, A_HEAD_DIM), F32)
    mods = modulation(jnp.concatenate([c_ctx[None, :], c], axis=0), mod_w, mod_b)
    no_pos = jnp.zeros(x_prompt.shape[1:], x_prompt.dtype)
    y_prompt, new_state_wkv = run_trunk(x_prompt, no_pos, mods[:, 0:1], s_zero, P)
    y_sample, _ = run_trunk(x_sample, grid_posemb(x_sample.shape[1], x_sample.dtype), mods[:, 1:], state_wkv, P)
    return (y_prompt, y_sample, new_state_wkv)
```

```python
import functools
import math

import jax
import jax.numpy as jnp
from jax import lax
from jax.experimental import pallas as pl
from jax.experimental.pallas import tpu as pltpu

D_MODEL = 1024
DEPTH = 2
GRID_W = 64
POS_BASE = 10000.0
A_HEADS = 12
A_HEAD_DIM = 64
D_A = A_HEADS * A_HEAD_DIM
B_GROUPS = 4
B_GROUP_DIM = 64
D_B = B_GROUPS * B_GROUP_DIM
D_C = 512
CONV_WIDTH = 31
D_HEADS = 4
D_HEAD_DIM = 128
D_D = D_HEADS * D_HEAD_DIM
CHUNK = 128
N_EXPERTS = 16
EC_FACTOR = 2
RMS_EPS = 1e-6
LN_EPS = 1e-5
GN_EPS = 64e-5

LANES = 128
SUBLANES = 8
SCAN_CHUNK = 64
HEADS_PER_TILE = LANES // A_HEAD_DIM
N_HEAD_PAIRS = A_HEADS // HEADS_PER_TILE
PAIRS_PER_STEP = 6
SCAN_REQUESTS = 2

F32 = jnp.float32
BF16 = jnp.bfloat16


def _mm(a, b, dims):
    return lax.dot_general(a.astype(BF16), b.astype(BF16), (dims, ((), ())),
                           preferred_element_type=F32)


_NN = ((1,), (0,))
_NT = ((1,), (1,))
_TN = ((0,), (0,))


def _cumsum_rows(tri, x):
    x1 = x.astype(BF16)
    r1 = x - x1.astype(F32)
    x2 = r1.astype(BF16)
    x3 = (r1 - x2.astype(F32)).astype(BF16)
    dot = lambda y: lax.dot_general(tri, y, (_NN, ((), ())), preferred_element_type=F32)
    return dot(x1) + dot(x2) + dot(x3)


def _tile_block_diag(x, mask):
    n = x.shape[1] // x.shape[0]
    return jnp.where(mask, jnp.concatenate([x] * n, axis=0), 0.0)


def _unit_triangular_inverses(mats, same16, same32, bd_mask):
    c = mats[0].shape[0]
    bd = lambda x: _tile_block_diag(x, bd_mask)
    lane = lax.broadcasted_iota(jnp.int32, mats[0].shape, 1) & (c - 1)
    eye = (lax.broadcasted_iota(jnp.int32, mats[0].shape, 0) == lane).astype(F32)
    pws = [jnp.where(same16, -a, 0.0) for a in mats]
    ts = [eye + n for n in pws]
    for _ in range(3):
        pws = [_mm(pw, bd(pw), _NN) for pw in pws]
        ts = [t + _mm(t, bd(pw), _NN) for t, pw in zip(ts, pws)]
    in32 = jnp.logical_and(same32, jnp.logical_not(same16))
    tmp = [_mm(t, bd(jnp.where(in32, a, 0.0)), _NN) for t, a in zip(ts, mats)]
    ts = [t - _mm(x, bd(t), _NN) for t, x in zip(ts, tmp)]
    tmp = [_mm(t, bd(jnp.where(same32, 0.0, a)), _NN) for t, a in zip(ts, mats)]
    ts = [t - _mm(x, bd(t), _NN) for t, x in zip(ts, tmp)]
    return ts


INVERSE_GROUP = 2


def _wkv_chunks(items):
    c = items[0][0].shape[0]
    iota = lambda shape, axis: lax.broadcasted_iota(jnp.int32, shape, axis)
    t1, s1 = iota((c, c), 0), iota((c, c), 1)
    t2, s2 = iota((c, LANES), 0), iota((c, LANES), 1) & (c - 1)
    t4, s4 = iota((c, 2 * LANES), 0), iota((c, 2 * LANES), 1) & (c - 1)
    tri = {False: (t1 >= s1).astype(F32).astype(BF16), True: (t1 <= s1).astype(F32).astype(BF16)}
    strict2 = {False: t2 > s2, True: t2 < s2}
    incl4 = {False: t4 >= s4, True: t4 <= s4}
    wide = (c, INVERSE_GROUP * LANES)
    tw, sw = iota(wide, 0), iota(wide, 1) & (c - 1)
    same16 = (tw >> 4) == (sw >> 4)
    same32 = (tw >> 5) == (sw >> 5)
    head_shift = A_HEAD_DIM.bit_length() - 1
    bd_mask = lambda n: (iota((n, n), 0) >> head_shift) == (iota((n, n), 1) >> head_shift)
    bd2_mask, bdw_mask = bd_mask(LANES), bd_mask(wide[1])
    bd2 = lambda x: _tile_block_diag(x, bd2_mask)

    cs_all = [_cumsum_rows(tri[rev], lw) for (_, _, _, lw, _, _, _, rev) in items]
    prep = []
    for (r, v, kk, lw, k, a, s, rev), cs in zip(items, cs_all):
        cs_end = cs[0:1] if rev else cs[c - 1:c]
        b = kk * a
        g_inv = jnp.exp(-cs)
        g_tail = jnp.exp(cs_end - cs)
        kt = kk * jnp.exp(cs - lw)
        rt = r * jnp.exp(cs)
        lhs = jnp.concatenate([kt, rt], axis=0)
        kb_inv = jnp.concatenate([bd2(k * g_inv), bd2(b * g_inv)], axis=0)
        kb_tail = jnp.concatenate([k * g_tail, b * g_tail], axis=0)
        prep.append((lhs, kb_inv, kb_tail, jnp.exp(cs_end)))
    ps = [_mm(pr[0], it[6], _NT) for pr, it in zip(prep, items)]
    gs = [_mm(pr[0], pr[1], _NT) for pr in prep]
    a_kbs = [jnp.where(strict2[it[7]], g[0:c, LANES:2 * LANES], 0.0) for g, it in zip(gs, items)]
    groups = [jnp.concatenate(a_kbs[i:i + INVERSE_GROUP], axis=1) for i in range(0, len(items), INVERSE_GROUP)]
    t_groups = _unit_triangular_inverses(groups, same16, same32, bdw_mask)
    ts = [tg[:, j * LANES:(j + 1) * LANES] for tg in t_groups for j in range(INVERSE_GROUP)]
    ws = [p[0:c] + _mm(jnp.where(strict2[it[7]], g[0:c, 0:LANES], 0.0), bd2(it[1]), _NN)
          for p, g, it in zip(ps, gs, items)]
    us = [_mm(t, bd2(w), _NN) for t, w in zip(ts, ws)]
    ys = [p[c:2 * c] + _mm(jnp.where(incl4[it[7]], g[c:2 * c], 0.0),
                           jnp.concatenate([bd2(it[1]), bd2(-u)], axis=0), _NN)
          for p, g, it, u in zip(ps, gs, items, us)]
    dss = [_mm(jnp.concatenate([it[1], -u], axis=0), pr[2], _TN)
           for it, u, pr in zip(items, us, prep)]
    return [(y, it[6] * pr[3] + jnp.where(bd2_mask, ds, 0.0))
            for y, it, pr, ds in zip(ys, items, prep, dss)]


def _wkv_kernel(rf, vf, kkf, lwf, kf, af, rb, vb, kkb, lwb, kb, ab, s0_ref,
                yf_ref, yb_ref, sfin_ref, st_s):
    n = A_HEAD_DIM
    where = [(q, p) for q in range(SCAN_REQUESTS) for p in range(PAIRS_PER_STEP)]

    @pl.when(pl.program_id(2) == 0)
    def _():
        zero = jnp.zeros((n, n), F32)
        for q, p in where:
            for z in range(2):
                st_s[q, z, p] = jnp.concatenate(
                    [jnp.concatenate([s0_ref[q, z, HEADS_PER_TILE * p], zero], axis=1),
                     jnp.concatenate([zero, s0_ref[q, z, HEADS_PER_TILE * p + 1]], axis=1)], axis=0)

    items = []
    for q, p in where:
        ln = slice(p * LANES, (p + 1) * LANES)
        items.append((rf[q, :, ln], vf[q, :, ln].astype(F32), kkf[q, :, ln], lwf[0, q, :, ln], kf[0, q, :, ln],
                      af[0, q, :, ln], st_s[q, 0, p], False))
        items.append((rb[q, :, ln], vb[q, :, ln].astype(F32), kkb[q, :, ln], lwb[0, q, :, ln], kb[0, q, :, ln],
                      ab[0, q, :, ln], st_s[q, 1, p], True))
    out = _wkv_chunks(items)
    for i, (q, p) in enumerate(where):
        ln = slice(p * LANES, (p + 1) * LANES)
        yf_ref[q, :, ln], st_s[q, 0, p] = out[2 * i]
        yb_ref[q, :, ln], st_s[q, 1, p] = out[2 * i + 1]

    @pl.when(pl.program_id(2) == pl.num_programs(2) - 1)
    def _():
        for q, p in where:
            for z in range(2):
                s = st_s[q, z, p]
                sfin_ref[q, z, HEADS_PER_TILE * p] = s[0:n, 0:n]
                sfin_ref[q, z, HEADS_PER_TILE * p + 1] = s[n:2 * n, n:2 * n]


def wkv7_scan_pallas(r, v, kk, lw, k, a, s0):
    bsz, seq, _ = r.shape
    c = SCAN_CHUNK
    nc = seq // c
    n = A_HEAD_DIM
    wd = PAIRS_PER_STEP * LANES
    nq = SCAN_REQUESTS
    heads = HEADS_PER_TILE * PAIRS_PER_STEP
    shared_f = pl.BlockSpec((nq, c, wd), lambda b, p, i: (b, i, p))
    shared_b = pl.BlockSpec((nq, c, wd), lambda b, p, i: (b, nc - 1 - i, p))
    dir_f = pl.BlockSpec((1, nq, c, wd), lambda b, p, i: (0, b, i, p))
    dir_b = pl.BlockSpec((1, nq, c, wd), lambda b, p, i: (1, b, nc - 1 - i, p))
    st = pl.BlockSpec((nq, 2, heads, n, n), lambda b, p, i: (b, 0, p, 0, 0))
    return pl.pallas_call(
        _wkv_kernel,
        grid=(bsz // SCAN_REQUESTS, N_HEAD_PAIRS // PAIRS_PER_STEP, nc),
        in_specs=[shared_f, shared_f, shared_f, dir_f, dir_f, dir_f,
                  shared_b, shared_b, shared_b, dir_b, dir_b, dir_b, st],
        out_specs=[shared_f, shared_b, st],
        out_shape=[jax.ShapeDtypeStruct((bsz, seq, D_A), F32),
                   jax.ShapeDtypeStruct((bsz, seq, D_A), F32),
                   jax.ShapeDtypeStruct((bsz, 2, A_HEADS, n, n), F32)],
        scratch_shapes=[pltpu.VMEM((nq, 2, PAIRS_PER_STEP, LANES, LANES), F32)],
        compiler_params=pltpu.CompilerParams(
            dimension_semantics=("arbitrary", "arbitrary", "arbitrary")),
        name="wkv7_scan",
    )(r, v, kk, lw, k, a, r, v, kk, lw, k, a, s0)


def grid_posemb(n_tokens, dtype):
    rows = n_tokens // GRID_W
    row = jnp.repeat(jnp.arange(rows, dtype=F32), GRID_W)
    col = jnp.tile(jnp.arange(GRID_W, dtype=F32), rows)
    quarter = D_MODEL // 4
    freq = 1.0 / (POS_BASE ** (jnp.arange(quarter, dtype=F32) / quarter))
    ar = row[:, None] * freq[None, :]
    ac = col[:, None] * freq[None, :]
    return jnp.concatenate([jnp.sin(ar), jnp.cos(ar), jnp.sin(ac), jnp.cos(ac)], axis=-1).astype(dtype)


def _exclusive_prefix(x):
    rows, seq = x.shape
    tri = jnp.where(lax.broadcasted_iota(jnp.int32, (LANES, LANES), 0)
                    <= lax.broadcasted_iota(jnp.int32, (LANES, LANES), 1), 1.0, 0.0).astype(BF16)
    carry = jnp.zeros((rows, 1), F32)
    out = []
    for blk in range(seq // LANES):
        xb = x[:, blk * LANES:(blk + 1) * LANES]
        inc = jnp.dot(xb.astype(BF16), tri, preferred_element_type=F32)
        out.append(inc - xb + carry)
        carry = carry + inc[:, LANES - 1:LANES]
    return jnp.concatenate(out, axis=1)


SELECT_STEPS = 80


def _select_kernel(aff_ref, slot_ref, *, cap):
    bsz, n_exp, seq = aff_ref.shape
    aff = aff_ref[...].reshape(bsz * n_exp, seq)
    count = lambda t: jnp.sum(jnp.where(aff >= t, 1.0, 0.0), axis=1, keepdims=True)
    lo = jnp.min(aff, axis=1, keepdims=True)
    hi = 2.0 * jnp.max(aff, axis=1, keepdims=True) + 1e-30

    def halve(_, bracket):
        lo, hi = bracket
        mid = lo + 0.5 * (hi - lo)
        ok = count(mid) >= cap
        return jnp.where(ok, mid, lo), jnp.where(ok, hi, mid)

    lo, hi = lax.fori_loop(0, SELECT_STEPS, halve, (lo, hi))
    above = jnp.where(aff >= hi, 1.0, 0.0)
    tied = jnp.where(aff >= lo, 1.0, 0.0) - above
    need = cap - jnp.sum(above, axis=1, keepdims=True)
    keep = above + tied * jnp.where(_exclusive_prefix(tied) < need, 1.0, 0.0)
    slot = _exclusive_prefix(keep)
    slot_ref[...] = jnp.where(keep > 0.5, slot, -1.0).astype(jnp.int32).reshape(bsz, n_exp, seq)


def _expert_kernel(slot_ref, aff_ref, h_ref, wg_ref, wu_ref, wd_ref, y_ref, wg_s, wu_s, wd_s, *, cap):
    @pl.when(pl.program_id(1) == 0)
    def _():
        wg_s[...] = wg_ref[0, 0].astype(BF16)
        wu_s[...] = wu_ref[0, 0].astype(BF16)
        wd_s[...] = wd_ref[0, 0].astype(BF16)

    group, seq, _ = h_ref.shape
    slot_iota = lax.broadcasted_iota(jnp.int32, (cap, seq), 0)
    expert = pl.ds(pl.program_id(0), 1)
    xs, vals = [], []
    for rb in range(group):
        hit = slot_iota == slot_ref[rb, expert, :]
        xs.append(jnp.dot(jnp.where(hit, 1.0, 0.0).astype(BF16), h_ref[rb], preferred_element_type=F32))
        vals.append(jnp.sum(jnp.where(hit, aff_ref[rb, expert, :], 0.0), axis=1, keepdims=True))
    xs = jnp.concatenate(xs, axis=0).astype(BF16)
    g = jnp.dot(xs, wg_s[...], preferred_element_type=F32)
    u = jnp.dot(xs, wu_s[...], preferred_element_type=F32)
    hid = (g * jax.nn.sigmoid(g) * u).astype(BF16)
    y = jnp.dot(hid, wd_s[...], preferred_element_type=F32) * jnp.concatenate(vals, axis=0)
    for rb in range(group):
        y_ref[rb, 0] = y[rb * cap:(rb + 1) * cap].astype(y_ref.dtype)


def _combine_kernel(slot_ref, ys_ref, x_ref, gate_ref, fg_ref, o_ref, *, cap, final):
    rows = x_ref.shape[1]
    n_exp = slot_ref.shape[2]
    slot_iota = lax.broadcasted_iota(jnp.int32, (rows, cap), 1)
    moe = jnp.zeros(x_ref.shape[1:], F32)
    for e in range(n_exp):
        onehot = jnp.where(slot_iota == slot_ref[0, :, e:e + 1], 1.0, 0.0).astype(BF16)
        moe = moe + jnp.dot(onehot, ys_ref[0, e * cap:(e + 1) * cap], preferred_element_type=F32)
    out = x_ref[0] + gate_ref[0] * moe
    o_ref[0] = _rms(out, fg_ref[...]) if final else out


MOE_VMEM_BYTES = 56 * 1024 * 1024
COMBINE_ROWS = 512
EXPERT_ROWS = 512


def expert_choice_ffn(x, gate, h, aff_t, layer, w_gate, w_up, w_down, final_g, final):
    bsz, seq, d = h.shape
    _, n_exp, _, f = w_gate.shape
    cap = EC_FACTOR * seq // n_exp
    slot = pl.pallas_call(
        functools.partial(_select_kernel, cap=float(cap)),
        out_shape=jax.ShapeDtypeStruct((bsz, n_exp, seq), jnp.int32),
        compiler_params=pltpu.CompilerParams(vmem_limit_bytes=MOE_VMEM_BYTES),
        name="moe_select",
    )(aff_t)
    group = min(bsz, max(1, EXPERT_ROWS // cap))
    per_request = pl.BlockSpec((group, n_exp, seq), lambda e, b: (b, 0, 0))
    weight = lambda k, n: pl.BlockSpec((1, 1, k, n), lambda e, b: (layer, e, 0, 0))
    ys = pl.pallas_call(
        functools.partial(_expert_kernel, cap=cap),
        grid=(n_exp, bsz // group),
        in_specs=[per_request, per_request,
                  pl.BlockSpec((group, seq, d), lambda e, b: (b, 0, 0)),
                  weight(d, f), weight(d, f), weight(f, d)],
        out_specs=pl.BlockSpec((group, 1, cap, d), lambda e, b: (b, e, 0, 0)),
        out_shape=jax.ShapeDtypeStruct((bsz, n_exp, cap, d), BF16),
        scratch_shapes=[pltpu.VMEM((d, f), BF16), pltpu.VMEM((d, f), BF16), pltpu.VMEM((f, d), BF16)],
        compiler_params=pltpu.CompilerParams(dimension_semantics=("arbitrary", "arbitrary"),
                                             vmem_limit_bytes=MOE_VMEM_BYTES),
        name="moe_experts",
    )(slot, aff_t, h, w_gate, w_up, w_down)
    rows = min(COMBINE_ROWS, seq)
    return pl.pallas_call(
        functools.partial(_combine_kernel, cap=cap, final=final),
        grid=(bsz, seq // rows),
        in_specs=[pl.BlockSpec((1, rows, n_exp), lambda b, i: (b, i, 0)),
                  pl.BlockSpec((1, n_exp * cap, d), lambda b, i: (b, 0, 0)),
                  pl.BlockSpec((1, rows, d), lambda b, i: (b, i, 0)),
                  pl.BlockSpec((1, 1, d), lambda b, i: (b, 0, 0)),
                  pl.BlockSpec((1, d), lambda b, i: (0, 0))],
        out_specs=pl.BlockSpec((1, rows, d), lambda b, i: (b, i, 0)),
        out_shape=jax.ShapeDtypeStruct((bsz, seq, d), F32),
        compiler_params=pltpu.CompilerParams(dimension_semantics=("arbitrary", "arbitrary"),
                                             vmem_limit_bytes=MOE_VMEM_BYTES),
        name="moe_combine",
    )(jnp.swapaxes(slot, 1, 2), ys.reshape(bsz, n_exp * cap, d), x, gate, _row(final_g))


def _mod_kernel(c_ref, w_ref, b_ref, o_ref):
    c = c_ref[...]
    w_hi, w_lo = _hi_lo(w_ref[0])
    o_ref[0] = _split_dot(c * jax.nn.sigmoid(c), w_hi, w_lo) + b_ref[0]


def modulation(cvecs, w_mod, b_mod):
    n, d = cvecs.shape
    depth = w_mod.shape[0]
    rows = -(-n // 8) * 8
    cpad = jnp.pad(cvecs, ((0, rows - n), (0, 0)))
    out = pl.pallas_call(
        _mod_kernel,
        grid=(depth, 6),
        in_specs=[pl.BlockSpec((rows, d), lambda l, n_: (0, 0)),
                  pl.BlockSpec((1, d, d), lambda l, n_: (l, 0, n_)),
                  pl.BlockSpec((1, 1, d), lambda l, n_: (l, 0, n_))],
        out_specs=pl.BlockSpec((1, rows, d), lambda l, n_: (l, 0, n_)),
        out_shape=jax.ShapeDtypeStruct((depth, rows, 6 * d), F32),
        compiler_params=pltpu.CompilerParams(dimension_semantics=("arbitrary", "arbitrary")),
        name="modulation",
    )(cpad, w_mod, b_mod.reshape(depth, 1, 6 * d))
    return out[:, :n].reshape(depth, n, 6, d)


TOKEN_TILE = 512
HALO = 16
LAYER_VMEM_BYTES = 48 * 1024 * 1024


def _rms(x, g):
    return x * lax.rsqrt(jnp.mean(x * x, axis=-1, keepdims=True) + RMS_EPS) * g


def _ln(x, g, b):
    mu = jnp.mean(x, axis=-1, keepdims=True)
    xc = x - mu
    return xc * lax.rsqrt(jnp.mean(xc * xc, axis=-1, keepdims=True) + LN_EPS) * g + b


def _split_dot(x, w_hi, w_lo):
    x_hi = x.astype(BF16)
    x_lo = (x - x_hi.astype(F32)).astype(BF16)
    dot = lambda a, b: jnp.dot(a, b, preferred_element_type=F32)
    return dot(x_hi, w_hi) + dot(x_lo, w_hi) + dot(x_hi, w_lo)


def _with_halo(x_ref, prev_ref, next_ref):
    return jnp.concatenate([prev_ref[0], x_ref[0], next_ref[0]], axis=0)


def _inside_mask(rows):
    i, n = pl.program_id(1), pl.num_programs(1)
    r = lax.broadcasted_iota(jnp.int32, (rows + 2 * HALO, 1), 0)
    return jnp.logical_and(jnp.logical_or(i > 0, r >= HALO), jnp.logical_or(i < n - 1, r < rows + HALO))


def _router(x1, mod_ref, nf_ref, wr_hi_ref, wr_lo_ref, br_ref, h2_ref, aff_ref):
    h2 = _rms(x1, nf_ref[...]) * (1.0 + mod_ref[0, 4:5]) + mod_ref[0, 3:4]
    h2_ref[0] = h2.astype(h2_ref.dtype)
    h_hi = h2.astype(BF16)
    h_lo = (h2 - h_hi.astype(F32)).astype(BF16)
    dot = lambda w, a: lax.dot_general(w, a, (_NT, ((), ())), preferred_element_type=F32)
    logits = dot(wr_hi_ref[...], h_hi) + dot(wr_hi_ref[...], h_lo) + dot(wr_lo_ref[...], h_hi) + br_ref[...]
    e = jnp.exp(logits - jnp.max(logits, axis=0, keepdims=True))
    aff_ref[0] = e / jnp.sum(e, axis=0, keepdims=True)


def _odd_kernel(x_ref, xp_ref, xn_ref, mod_ref, nm_ref, nf_ref, win_ref, cw_ref, cb_ref, cg_ref, cbb_ref,
                vg_ref, vb_ref, ws_ref, bs_ref, wout_ref, wr_hi_ref, wr_lo_ref, br_ref,
                x1_ref, h2_ref, aff_ref, glu_s):
    rows = x_ref.shape[1]
    inside = _inside_mask(rows)
    xa = _with_halo(x_ref, xp_ref, xn_ref)
    h = _rms(xa, nm_ref[...]) * (1.0 + mod_ref[0, 1:2]) + mod_ref[0, 0:1]
    h = jnp.where(inside, h, 0.0).astype(BF16)
    pc = jnp.dot(h, win_ref[:, 0:2 * D_C], preferred_element_type=F32)
    glu = pc[:, 0:D_C] * jax.nn.sigmoid(pc[:, D_C:2 * D_C])
    span = glu_s.shape[1]
    for r in range(SUBLANES):
        glu_s[r] = glu[r:r + span]
    acc = jnp.zeros((rows, D_C), F32)
    for j in range(CONV_WIDTH):
        off = HALO - CONV_WIDTH // 2 + j
        base = off // SUBLANES * SUBLANES
        acc = acc + cw_ref[j:j + 1, :] * glu_s[off % SUBLANES, base:base + rows, :]
    o_c = _ln(acc + cb_ref[...], cg_ref[...], cbb_ref[...])
    o_c = o_c * jax.nn.sigmoid(o_c)
    pd = jax.nn.gelu(jnp.dot(h[HALO:HALO + rows], win_ref[:, 2 * D_C:], preferred_element_type=F32))
    u = pd[:, 0:D_D]
    v = _ln(pd[:, D_D:], vg_ref[...], vb_ref[...]).astype(BF16)
    chunks = []
    for ck in range(rows // CHUNK):
        vc = v[ck * CHUNK:(ck + 1) * CHUNK]
        chunks.append(jnp.concatenate(
            [jnp.dot(ws_ref[hd], vc[:, hd * D_HEAD_DIM:(hd + 1) * D_HEAD_DIM], preferred_element_type=F32)
             for hd in range(D_HEADS)], axis=1) + bs_ref[...])
    o_d = u * jnp.concatenate(chunks, axis=0)
    mixed = jnp.dot(jnp.concatenate([o_c, o_d], axis=1).astype(BF16), wout_ref[...], preferred_element_type=F32)
    x1 = x_ref[0] + mod_ref[0, 2:3] * mixed
    x1_ref[0] = x1
    _router(x1, mod_ref, nf_ref, wr_hi_ref, wr_lo_ref, br_ref, h2_ref, aff_ref)


def _row(v):
    return v.reshape(1, -1)


def _hi_lo(w):
    hi = w.astype(BF16)
    return hi, (w - hi.astype(F32)).astype(BF16)


def _tile_specs(rows, seq, d):
    per = rows // HALO
    last = seq // HALO - 1
    return [pl.BlockSpec((1, rows, d), lambda b, i: (b, i, 0)),
            pl.BlockSpec((1, HALO, d), lambda b, i: (b, jnp.maximum(i * per - 1, 0), 0)),
            pl.BlockSpec((1, HALO, d), lambda b, i: (b, jnp.minimum((i + 1) * per, last), 0))]


def _shared_tile_specs(rows, seq, d):
    per = rows // HALO
    last = seq // HALO - 1
    return [pl.BlockSpec((rows, d), lambda b, i: (i, 0)),
            pl.BlockSpec((HALO, d), lambda b, i: (jnp.maximum(i * per - 1, 0), 0)),
            pl.BlockSpec((HALO, d), lambda b, i: (jnp.minimum((i + 1) * per, last), 0))]


def _full(a):
    return pl.BlockSpec(a.shape, lambda b, i: (0,) * a.ndim)


def odd_layer(x, mod, norm_mix, norm_ffn, w_in, conv_w, conv_b, cln_g, cln_b, vln_g, vln_b, w_s, b_s, w_out,
              w_router, b_router):
    bsz, seq, d = x.shape
    rows = min(TOKEN_TILE, seq)
    n_exp = w_router.shape[1]
    wr_hi, wr_lo = _hi_lo(w_router.T)
    bs_full = jnp.repeat(b_s.T, D_HEAD_DIM, axis=1)
    consts = [_row(norm_mix), _row(norm_ffn), w_in.astype(BF16), conv_w, _row(conv_b), _row(cln_g), _row(cln_b),
              _row(vln_g), _row(vln_b), w_s.astype(BF16), bs_full, w_out.astype(BF16), wr_hi, wr_lo,
              b_router.reshape(-1, 1)]
    tile = lambda w: pl.BlockSpec((1, rows, w), lambda b, i: (b, i, 0))
    return pl.pallas_call(
        _odd_kernel,
        grid=(bsz, seq // rows),
        in_specs=_tile_specs(rows, seq, d) + [pl.BlockSpec((1, 6, d), lambda b, i: (b, 0, 0))]
        + [_full(a) for a in consts],
        out_specs=[tile(d), tile(d), pl.BlockSpec((1, n_exp, rows), lambda b, i: (b, 0, i))],
        out_shape=[jax.ShapeDtypeStruct((bsz, seq, d), F32), jax.ShapeDtypeStruct((bsz, seq, d), BF16),
                   jax.ShapeDtypeStruct((bsz, n_exp, seq), F32)],
        scratch_shapes=[pltpu.VMEM((SUBLANES, rows + 2 * HALO - SUBLANES, D_C), F32)],
        compiler_params=pltpu.CompilerParams(dimension_semantics=("arbitrary", "arbitrary"),
                                             vmem_limit_bytes=LAYER_VMEM_BYTES),
        name="odd_layer",
    )(x, x, x, mod, *consts)


EVEN_TILE = 256
DECAY_SCALE = math.exp(-0.5)


def _head_sums(x, ones_bd):
    hi = x.astype(BF16)
    lo = (x - hi.astype(F32)).astype(BF16)
    dot = lambda a: jnp.dot(a, ones_bd, preferred_element_type=F32)
    return jnp.concatenate([dot(hi[:, g * LANES:(g + 1) * LANES]) + dot(lo[:, g * LANES:(g + 1) * LANES])
                            for g in range(x.shape[1] // LANES)], axis=1)


def _head_ones():
    head_shift = A_HEAD_DIM.bit_length() - 1
    same = ((lax.broadcasted_iota(jnp.int32, (LANES, LANES), 0) >> head_shift)
            == (lax.broadcasted_iota(jnp.int32, (LANES, LANES), 1) >> head_shift))
    return jnp.where(same, 1.0, 0.0).astype(BF16)


def _even_pre_kernel(x_ref, xp_ref, xn_ref, pos_ref, posp_ref, posn_ref, mod_ref, nm_ref, win_ref, mu_rkv_ref,
                     down_h_ref, down_dh_ref, w2_ref, w0_ref, a2_ref, a0_ref, g2_ref, kk_ref, ka_ref,
                     rk_ref, x_out, r_out, v_out, kk_out, lw_out, k_out, a_out, gate_out, bonus_out, u_out,
                     proj_s, h_s):
    rows = x_ref.shape[1]
    inside = _inside_mask(rows)
    xa = _with_halo(x_ref, xp_ref, xn_ref) + jnp.concatenate([posp_ref[...], pos_ref[...], posn_ref[...]], axis=0)
    x_out[0] = xa[HALO:HALO + rows]
    h = _rms(xa, nm_ref[...]) * (1.0 + mod_ref[0, 1:2]) + mod_ref[0, 0:1]
    h = jnp.where(inside, h, 0.0)
    h_s[...] = h
    proj = jnp.dot(h.astype(BF16), win_ref[...], preferred_element_type=F32)
    proj_s[...] = proj[:, 0:3 * D_A]
    u_out[0] = proj[HALO:HALO + rows, 3 * D_A:].astype(u_out.dtype)

    def shifted(ref, lo, hi):
        cur = ref[HALO:HALO + rows, lo:hi]
        return cur, 0.5 * (ref[HALO - 1:HALO - 1 + rows, lo:hi] + ref[HALO + 1:HALO + 1 + rows, lo:hi]) - cur

    r, dr = shifted(proj_s, 0, D_A)
    k, dk = shifted(proj_s, D_A, 2 * D_A)
    v, dv = shifted(proj_s, 2 * D_A, 3 * D_A)
    r = r + dr * mu_rkv_ref[0:1]
    k = k + dk * mu_rkv_ref[1:2]
    v = v + dv * mu_rkv_ref[2:3]
    hc, dh = shifted(h_s, 0, D_MODEL)
    dot = lambda a, b: jnp.dot(a, b, preferred_element_type=F32)
    low = dot(hc.astype(BF16), down_h_ref[...]) + dot(dh.astype(BF16), down_dh_ref[...])
    n_w, n_a = w2_ref.shape[0], a2_ref.shape[0]
    w_pre = w0_ref[...] + dot(jnp.tanh(low[:, 0:n_w]).astype(BF16), w2_ref[...])
    icl = jax.nn.sigmoid(a0_ref[...] + dot(low[:, n_w:n_w + n_a].astype(BF16), a2_ref[...]))
    gate_out[0] = dot(jax.nn.sigmoid(low[:, n_w + n_a:]).astype(BF16), g2_ref[...])
    ones_bd = _head_ones()
    kk = k * kk_ref[...]
    kk = kk / jnp.maximum(jnp.sqrt(_head_sums(kk * kk, ones_bd)), 1e-12)
    r_out[0] = r
    v_out[0] = v.astype(v_out.dtype)
    kk_out[0] = kk
    bonus_out[0] = _head_sums(r * k * rk_ref[...], ones_bd) * v
    for z in range(2):
        a_z = icl[:, z * D_A:(z + 1) * D_A]
        lw_out[z, 0] = -DECAY_SCALE * jax.nn.sigmoid(w_pre[:, z * D_A:(z + 1) * D_A])
        a_out[z, 0] = a_z
        k_out[z, 0] = k * (1.0 + (a_z - 1.0) * ka_ref[...])


def _block_diag2(w):
    z = jnp.zeros_like(w[0])
    return jnp.concatenate([jnp.concatenate([w[0], z], axis=1), jnp.concatenate([z, w[1]], axis=1)], axis=0)


def even_pre(x, pos, mod, norm_mix, w_in, mu_rkv, mu_wag, w0, w1, w2, a0, a1, a2, g1, g2, k_k, k_a, r_k):
    bsz, seq, d = x.shape
    rows = min(EVEN_TILE, seq)
    cat = lambda w: jnp.concatenate([w[0], w[1]], axis=1)
    downs = [cat(w1), cat(a1), g1]
    down_h = jnp.concatenate(downs, axis=1)
    down_dh = jnp.concatenate([mu_wag[i][:, None] * w for i, w in enumerate(downs)], axis=1)
    consts = [_row(norm_mix), w_in.astype(BF16), mu_rkv, down_h.astype(BF16), down_dh.astype(BF16),
              _block_diag2(w2).astype(BF16), _row(w0), _block_diag2(a2).astype(BF16), _row(a0),
              g2.astype(BF16), _row(k_k), _row(k_a), _row(r_k)]
    tile = lambda w: pl.BlockSpec((1, rows, w), lambda b, i: (b, i, 0))
    tile2 = pl.BlockSpec((2, 1, rows, D_A), lambda b, i: (0, b, i, 0))
    sds = lambda w, dt=F32: jax.ShapeDtypeStruct((bsz, seq, w), dt)
    sds2 = jax.ShapeDtypeStruct((2, bsz, seq, D_A), F32)
    return pl.pallas_call(
        _even_pre_kernel,
        grid=(bsz, seq // rows),
        in_specs=_tile_specs(rows, seq, d) + _shared_tile_specs(rows, seq, d)
        + [pl.BlockSpec((1, 6, d), lambda b, i: (b, 0, 0))] + [_full(a) for a in consts],
        out_specs=[tile(d), tile(D_A), tile(D_A), tile(D_A), tile2, tile2, tile2, tile(D_A), tile(D_A), tile(D_B)],
        out_shape=[sds(d), sds(D_A), sds(D_A, BF16), sds(D_A), sds2, sds2, sds2, sds(D_A), sds(D_A), sds(D_B, BF16)],
        scratch_shapes=[pltpu.VMEM((rows + 2 * HALO, 3 * D_A), F32), pltpu.VMEM((rows + 2 * HALO, d), F32)],
        compiler_params=pltpu.CompilerParams(dimension_semantics=("arbitrary", "arbitrary"),
                                             vmem_limit_bytes=LAYER_VMEM_BYTES),
        name="even_pre",
    )(x, x, x, pos, pos, pos, mod, *consts)


FOURIER_ROWS = 512
DFT_SPLIT = 64


def _fourier_kernel(u_ref, f64_ref, fl_ref, o_ref, ucs_s):
    seq = u_ref.shape[1]

    @pl.when(pl.program_id(1) == 0)
    def _():
        ucs = jnp.dot(u_ref[0], f64_ref[...], preferred_element_type=F32)
        ucs_s[0:seq] = ucs[:, 0:D_B].astype(BF16)
        ucs_s[seq:2 * seq] = ucs[:, D_B:].astype(BF16)

    scale = 1.0 / math.sqrt(seq * B_GROUP_DIM)
    o_ref[0] = (jnp.dot(fl_ref[...], ucs_s[...], preferred_element_type=F32) * scale).astype(o_ref.dtype)


def _dft_tables(seq):
    def cs(n):
        i = jnp.arange(n, dtype=jnp.int32)
        ang = ((i[:, None] * i[None, :]) % n).astype(F32) * (2.0 * math.pi / n)
        return jnp.cos(ang), jnp.sin(ang)
    c64, s64 = cs(B_GROUP_DIM)
    eye = jnp.eye(B_GROUPS, dtype=F32)
    f64 = jnp.concatenate([jnp.kron(eye, c64), jnp.kron(eye, s64)], axis=1)
    split = min(DFT_SPLIT, seq)
    s = jnp.arange(seq, dtype=jnp.int32)[None, :]
    ang = lambda t: ((t[:, None] * s) % seq).astype(F32) * (2.0 * math.pi / seq)
    ang_a = ang(jnp.arange(seq // split, dtype=jnp.int32) * split)
    ang_b = ang(jnp.arange(split, dtype=jnp.int32))
    ca, sa = jnp.cos(ang_a)[:, None, :], jnp.sin(ang_a)[:, None, :]
    cb, sb = jnp.cos(ang_b)[None, :, :], jnp.sin(ang_b)[None, :, :]
    cl = (ca * cb - sa * sb).reshape(seq, seq)
    sl = (sa * cb + ca * sb).reshape(seq, seq)
    return f64.astype(BF16), jnp.concatenate([cl, -sl], axis=1).astype(BF16)


def fourier_mixer(u):
    bsz, seq, _ = u.shape
    rows = min(FOURIER_ROWS, seq)
    f64, fl = _dft_tables(seq)
    return pl.pallas_call(
        _fourier_kernel,
        grid=(bsz, seq // rows),
        in_specs=[pl.BlockSpec((1, seq, D_B), lambda b, i: (b, 0, 0)),
                  pl.BlockSpec(f64.shape, lambda b, i: (0, 0)),
                  pl.BlockSpec((rows, 2 * seq), lambda b, i: (i, 0))],
        out_specs=pl.BlockSpec((1, rows, D_B), lambda b, i: (b, i, 0)),
        out_shape=jax.ShapeDtypeStruct((bsz, seq, D_B), BF16),
        scratch_shapes=[pltpu.VMEM((2 * seq, D_B), BF16)],
        compiler_params=pltpu.CompilerParams(dimension_semantics=("arbitrary", "arbitrary"),
                                             vmem_limit_bytes=LAYER_VMEM_BYTES),
        name="fourier_mixer",
    )(u, f64, fl)


def _even_post_kernel(yf_ref, yb_ref, bonus_ref, gate_ref, ob_ref, x_ref, mod_ref, gnw_ref, gnb_ref, wout_ref,
                      nf_ref, wr_hi_ref, wr_lo_ref, br_ref, x1_ref, h2_ref, aff_ref):
    ones_bd = _head_ones()
    y = yf_ref[0] + yb_ref[0]
    mu = _head_sums(y, ones_bd) * (1.0 / A_HEAD_DIM)
    yc = y - mu
    var = _head_sums(yc * yc, ones_bd) * (1.0 / A_HEAD_DIM)
    o_a = (yc * lax.rsqrt(var + GN_EPS) * gnw_ref[...] + gnb_ref[...] + bonus_ref[0]) * gate_ref[0]
    mixed = (jnp.dot(o_a.astype(BF16), wout_ref[0:D_A], preferred_element_type=F32)
             + jnp.dot(ob_ref[0], wout_ref[D_A:], preferred_element_type=F32))
    x1 = x_ref[0] + mod_ref[0, 2:3] * mixed
    x1_ref[0] = x1
    _router(x1, mod_ref, nf_ref, wr_hi_ref, wr_lo_ref, br_ref, h2_ref, aff_ref)


def even_post(yf, yb, bonus, gate, o_b, x, mod, gn_w, gn_b, w_out, norm_ffn, w_router, b_router):
    bsz, seq, d = x.shape
    rows = min(TOKEN_TILE, seq)
    n_exp = w_router.shape[1]
    wr_hi, wr_lo = _hi_lo(w_router.T)
    consts = [_row(gn_w), _row(gn_b), w_out.astype(BF16), _row(norm_ffn), wr_hi, wr_lo, b_router.reshape(-1, 1)]
    tile = lambda w: pl.BlockSpec((1, rows, w), lambda b, i: (b, i, 0))
    return pl.pallas_call(
        _even_post_kernel,
        grid=(bsz, seq // rows),
        in_specs=[tile(D_A), tile(D_A), tile(D_A), tile(D_A), tile(D_B), tile(d),
                  pl.BlockSpec((1, 6, d), lambda b, i: (b, 0, 0))] + [_full(a) for a in consts],
        out_specs=[tile(d), tile(d), pl.BlockSpec((1, n_exp, rows), lambda b, i: (b, 0, i))],
        out_shape=[jax.ShapeDtypeStruct((bsz, seq, d), F32), jax.ShapeDtypeStruct((bsz, seq, d), BF16),
                   jax.ShapeDtypeStruct((bsz, n_exp, seq), F32)],
        compiler_params=pltpu.CompilerParams(dimension_semantics=("arbitrary", "arbitrary"),
                                             vmem_limit_bytes=LAYER_VMEM_BYTES),
        name="even_post",
    )(yf, yb, bonus, gate, o_b, x, mod, *consts)


def run_trunk(x, pos, mods, s_init, P):
    states = []
    for l in range(DEPTH):
        j = l // 2
        mod = jnp.broadcast_to(mods[l], (x.shape[0], 6, x.shape[2]))
        if l % 2 == 0:
            x, r, v, kk, lw, kd, a, gate, bonus, u = even_pre(
                x, pos if l == 0 else jnp.zeros_like(pos), mod, P['norm_mix'][l], P['ev_w_in'][j], P['ev_mu_rkv'][j], P['ev_mu_wag'][j],
                P['ev_w0'][j], P['ev_w1'][j], P['ev_w2'][j], P['ev_a0'][j], P['ev_a1'][j], P['ev_a2'][j],
                P['ev_g1'][j], P['ev_g2'][j], P['ev_k_k'][j], P['ev_k_a'][j], P['ev_r_k'][j])
            yf, yb, s_fin = wkv7_scan_pallas(r, v, kk, lw, kd, a, s_init[:, j])
            states.append(s_fin)
            x1, h2, aff_t = even_post(yf, yb, bonus, gate, fourier_mixer(u), x, mod, P['ev_gn_w'][j], P['ev_gn_b'][j],
                                      P['ev_w_out'][j], P['norm_ffn'][l], P['moe_router'][l], P['moe_router_b'][l])
        else:
            x1, h2, aff_t = odd_layer(x, mod, P['norm_mix'][l], P['norm_ffn'][l], P['od_w_in'][j], P['od_conv_w'][j],
                                      P['od_conv_b'][j], P['od_cln_g'][j], P['od_cln_b'][j], P['od_vln_g'][j],
                                      P['od_vln_b'][j], P['od_w_s'][j], P['od_b_s'][j], P['od_w_out'][j],
                                      P['moe_router'][l], P['moe_router_b'][l])
        x = expert_choice_ffn(x1, mod[:, 5:6], h2, aff_t, l, P['moe_w_gate'], P['moe_w_up'], P['moe_w_down'],
                              P['final_norm'], l == DEPTH - 1)
    return x, jnp.stack(states, axis=1)


def kernel(x_prompt, x_sample, state_wkv, c, c_ctx, mod_w, mod_b, norm_mix, norm_ffn, final_norm,
           ev_w_in, ev_w_out, ev_mu_rkv, ev_mu_wag, ev_w0, ev_w1, ev_w2, ev_a0, ev_a1, ev_a2,
           ev_g1, ev_g2, ev_k_k, ev_k_a, ev_r_k, ev_gn_w, ev_gn_b,
           od_w_in, od_w_out, od_conv_w, od_conv_b, od_cln_g, od_cln_b, od_vln_g, od_vln_b,
           od_w_s, od_b_s, moe_router, moe_router_b, moe_w_gate, moe_w_up, moe_w_down):
    P = dict(mod_w=mod_w, mod_b=mod_b, norm_mix=norm_mix, norm_ffn=norm_ffn, final_norm=final_norm,
             ev_w_in=ev_w_in, ev_w_out=ev_w_out, ev_mu_rkv=ev_mu_rkv, ev_mu_wag=ev_mu_wag,
             ev_w0=ev_w0, ev_w1=ev_w1, ev_w2=ev_w2, ev_a0=ev_a0, ev_a1=ev_a1, ev_a2=ev_a2,
             ev_g1=ev_g1, ev_g2=ev_g2, ev_k_k=ev_k_k, ev_k_a=ev_k_a, ev_r_k=ev_r_k,
             ev_gn_w=ev_gn_w, ev_gn_b=ev_gn_b,
             od_w_in=od_w_in, od_w_out=od_w_out, od_conv_w=od_conv_w, od_conv_b=od_conv_b,
             od_cln_g=od_cln_g, od_cln_b=od_cln_b, od_vln_g=od_vln_g, od_vln_b=od_vln_b,
             od_w_s=od_w_s, od_b_s=od_b_s, moe_router=moe_router, moe_router_b=moe_router_b,
             moe_w_gate=moe_w_gate, moe_w_up=moe_w_up, moe_w_down=moe_w_down)
    n_even = state_wkv.shape[1]
    s_zero = jnp.zeros((x_prompt.shape[0], n_even, 2, A_HEADS, A_HEAD_DIM, A_HEAD_DIM), F32)
    mods = modulation(jnp.concatenate([c_ctx[None, :], c], axis=0), mod_w, mod_b)
    no_pos = jnp.zeros(x_prompt.shape[1:], x_prompt.dtype)
    y_prompt, new_state_wkv = run_trunk(x_prompt, no_pos, mods[:, 0:1], s_zero, P)
    y_sample, _ = run_trunk(x_sample, grid_posemb(x_sample.shape[1], x_sample.dtype), mods[:, 1:], state_wkv, P)
    return (y_prompt, y_sample, new_state_wkv)
```

```python
import functools
import math

import jax
import jax.numpy as jnp
from jax import lax
from jax.experimental import pallas as pl
from jax.experimental.pallas import tpu as pltpu

D_MODEL = 1024
DEPTH = 2
GRID_W = 64
POS_BASE = 10000.0
A_HEADS = 12
A_HEAD_DIM = 64
D_A = A_HEADS * A_HEAD_DIM
B_GROUPS = 4
B_GROUP_DIM = 64
D_B = B_GROUPS * B_GROUP_DIM
D_C = 512
CONV_WIDTH = 31
D_HEADS = 4
D_HEAD_DIM = 128
D_D = D_HEADS * D_HEAD_DIM
CHUNK = 128
N_EXPERTS = 16
EC_FACTOR = 2
RMS_EPS = 1e-6
LN_EPS = 1e-5
GN_EPS = 64e-5

LANES = 128
SUBLANES = 8
MXU_DEPTH = 256
SCAN_CHUNK = 64
HEADS_PER_TILE = LANES // A_HEAD_DIM
N_HEAD_PAIRS = A_HEADS // HEADS_PER_TILE
PAIRS_PER_STEP = 6
SCAN_REQUESTS = 2

F32 = jnp.float32
BF16 = jnp.bfloat16


def _mm(a, b, dims):
    return lax.dot_general(a.astype(BF16), b.astype(BF16), (dims, ((), ())),
                           preferred_element_type=F32)


_NN = ((1,), (0,))
_NT = ((1,), (1,))
_TN = ((0,), (0,))


def _cumsum_rows(tri, x):
    x1 = x.astype(BF16)
    r1 = x - x1.astype(F32)
    x2 = r1.astype(BF16)
    x3 = (r1 - x2.astype(F32)).astype(BF16)
    dot = lambda y: lax.dot_general(tri, y, (_NN, ((), ())), preferred_element_type=F32)
    return dot(x1) + dot(x2) + dot(x3)


def _tile_block_diag(x, mask):
    n = x.shape[1] // x.shape[0]
    return jnp.where(mask, jnp.concatenate([x] * n, axis=0), 0.0)


def _unit_triangular_inverses(mats, same16, same32, bd_mask):
    c = mats[0].shape[0]
    bd = lambda x: _tile_block_diag(x, bd_mask)
    lane = lax.broadcasted_iota(jnp.int32, mats[0].shape, 1) & (c - 1)
    eye = (lax.broadcasted_iota(jnp.int32, mats[0].shape, 0) == lane).astype(F32)
    pws = [jnp.where(same16, -a, 0.0) for a in mats]
    ts = [eye + n for n in pws]
    for _ in range(3):
        pws = [_mm(pw, bd(pw), _NN) for pw in pws]
        ts = [t + _mm(t, bd(pw), _NN) for t, pw in zip(ts, pws)]
    in32 = jnp.logical_and(same32, jnp.logical_not(same16))
    tmp = [_mm(t, bd(jnp.where(in32, a, 0.0)), _NN) for t, a in zip(ts, mats)]
    ts = [t - _mm(x, bd(t), _NN) for t, x in zip(ts, tmp)]
    tmp = [_mm(t, bd(jnp.where(same32, 0.0, a)), _NN) for t, a in zip(ts, mats)]
    ts = [t - _mm(x, bd(t), _NN) for t, x in zip(ts, tmp)]
    return ts


INVERSE_GROUP = 2


def _wkv_chunks(items):
    c = items[0][0].shape[0]
    iota = lambda shape, axis: lax.broadcasted_iota(jnp.int32, shape, axis)
    t1, s1 = iota((c, c), 0), iota((c, c), 1)
    t2, s2 = iota((c, LANES), 0), iota((c, LANES), 1) & (c - 1)
    t4, s4 = iota((c, 2 * LANES), 0), iota((c, 2 * LANES), 1) & (c - 1)
    tri = {False: (t1 >= s1).astype(F32).astype(BF16), True: (t1 <= s1).astype(F32).astype(BF16)}
    strict2 = {False: t2 > s2, True: t2 < s2}
    incl4 = {False: t4 >= s4, True: t4 <= s4}
    wide = (c, INVERSE_GROUP * LANES)
    tw, sw = iota(wide, 0), iota(wide, 1) & (c - 1)
    same16 = (tw >> 4) == (sw >> 4)
    same32 = (tw >> 5) == (sw >> 5)
    head_shift = A_HEAD_DIM.bit_length() - 1
    bd_mask = lambda n: (iota((n, n), 0) >> head_shift) == (iota((n, n), 1) >> head_shift)
    bd2_mask, bdw_mask = bd_mask(LANES), bd_mask(wide[1])
    bd2 = lambda x: _tile_block_diag(x, bd2_mask)

    cs_all = [_cumsum_rows(tri[rev], lw) for (_, _, _, lw, _, _, _, rev) in items]
    prep = []
    for (r, v, kk, lw, k, a, s, rev), cs in zip(items, cs_all):
        cs_end = cs[0:1] if rev else cs[c - 1:c]
        b = kk * a
        g_inv = jnp.exp(-cs)
        g_tail = jnp.exp(cs_end - cs)
        kt = kk * jnp.exp(cs - lw)
        rt = r * jnp.exp(cs)
        lhs = jnp.concatenate([kt, rt], axis=0)
        kb_inv = jnp.concatenate([bd2(k * g_inv), bd2(b * g_inv)], axis=0)
        kb_tail = jnp.concatenate([k * g_tail, b * g_tail], axis=0)
        prep.append((lhs, kb_inv, kb_tail, jnp.exp(cs_end)))
    ps = [_mm(pr[0], it[6], _NT) for pr, it in zip(prep, items)]
    gs = [_mm(pr[0], pr[1], _NT) for pr in prep]
    a_kbs = [jnp.where(strict2[it[7]], g[0:c, LANES:2 * LANES], 0.0) for g, it in zip(gs, items)]
    groups = [jnp.concatenate(a_kbs[i:i + INVERSE_GROUP], axis=1) for i in range(0, len(items), INVERSE_GROUP)]
    t_groups = _unit_triangular_inverses(groups, same16, same32, bdw_mask)
    ts = [tg[:, j * LANES:(j + 1) * LANES] for tg in t_groups for j in range(INVERSE_GROUP)]
    ws = [p[0:c] + _mm(jnp.where(strict2[it[7]], g[0:c, 0:LANES], 0.0), bd2(it[1]), _NN)
          for p, g, it in zip(ps, gs, items)]
    us = [_mm(t, bd2(w), _NN) for t, w in zip(ts, ws)]
    ys = [p[c:2 * c] + _mm(jnp.where(incl4[it[7]], g[c:2 * c], 0.0),
                           jnp.concatenate([bd2(it[1]), bd2(-u)], axis=0), _NN)
          for p, g, it, u in zip(ps, gs, items, us)]
    dss = [_mm(jnp.concatenate([it[1], -u], axis=0), pr[2], _TN)
           for it, u, pr in zip(items, us, prep)]
    return [(y, it[6] * pr[3] + jnp.where(bd2_mask, ds, 0.0))
            for y, it, pr, ds in zip(ys, items, prep, dss)]


def _wkv_kernel(rf, vf, kkf, lwf, kf, af, rb, vb, kkb, lwb, kb, ab, s0_ref,
                yf_ref, yb_ref, sfin_ref, st_s):
    n = A_HEAD_DIM
    where = [(q, p) for q in range(SCAN_REQUESTS) for p in range(PAIRS_PER_STEP)]

    @pl.when(pl.program_id(2) == 0)
    def _():
        zero = jnp.zeros((n, n), F32)
        for q, p in where:
            for z in range(2):
                st_s[q, z, p] = jnp.concatenate(
                    [jnp.concatenate([s0_ref[q, z, HEADS_PER_TILE * p], zero], axis=1),
                     jnp.concatenate([zero, s0_ref[q, z, HEADS_PER_TILE * p + 1]], axis=1)], axis=0)

    items = []
    for q, p in where:
        ln = slice(p * LANES, (p + 1) * LANES)
        items.append((rf[q, :, ln], vf[q, :, ln].astype(F32), kkf[q, :, ln], lwf[0, q, :, ln], kf[0, q, :, ln],
                      af[0, q, :, ln], st_s[q, 0, p], False))
        items.append((rb[q, :, ln], vb[q, :, ln].astype(F32), kkb[q, :, ln], lwb[0, q, :, ln], kb[0, q, :, ln],
                      ab[0, q, :, ln], st_s[q, 1, p], True))
    out = _wkv_chunks(items)
    for i, (q, p) in enumerate(where):
        ln = slice(p * LANES, (p + 1) * LANES)
        yf_ref[q, :, ln], st_s[q, 0, p] = out[2 * i]
        yb_ref[q, :, ln], st_s[q, 1, p] = out[2 * i + 1]

    @pl.when(pl.program_id(2) == pl.num_programs(2) - 1)
    def _():
        for q, p in where:
            for z in range(2):
                s = st_s[q, z, p]
                sfin_ref[q, z, HEADS_PER_TILE * p] = s[0:n, 0:n]
                sfin_ref[q, z, HEADS_PER_TILE * p + 1] = s[n:2 * n, n:2 * n]


def wkv7_scan_pallas(r, v, kk, lw, k, a, s0):
    bsz, seq, _ = r.shape
    c = SCAN_CHUNK
    nc = seq // c
    n = A_HEAD_DIM
    wd = PAIRS_PER_STEP * LANES
    nq = SCAN_REQUESTS
    heads = HEADS_PER_TILE * PAIRS_PER_STEP
    shared_f = pl.BlockSpec((nq, c, wd), lambda b, p, i: (b, i, p))
    shared_b = pl.BlockSpec((nq, c, wd), lambda b, p, i: (b, nc - 1 - i, p))
    dir_f = pl.BlockSpec((1, nq, c, wd), lambda b, p, i: (0, b, i, p))
    dir_b = pl.BlockSpec((1, nq, c, wd), lambda b, p, i: (1, b, nc - 1 - i, p))
    st = pl.BlockSpec((nq, 2, heads, n, n), lambda b, p, i: (b, 0, p, 0, 0))
    return pl.pallas_call(
        _wkv_kernel,
        grid=(bsz // SCAN_REQUESTS, N_HEAD_PAIRS // PAIRS_PER_STEP, nc),
        in_specs=[shared_f, shared_f, shared_f, dir_f, dir_f, dir_f,
                  shared_b, shared_b, shared_b, dir_b, dir_b, dir_b, st],
        out_specs=[shared_f, shared_b, st],
        out_shape=[jax.ShapeDtypeStruct((bsz, seq, D_A), F32),
                   jax.ShapeDtypeStruct((bsz, seq, D_A), F32),
                   jax.ShapeDtypeStruct((bsz, 2, A_HEADS, n, n), F32)],
        scratch_shapes=[pltpu.VMEM((nq, 2, PAIRS_PER_STEP, LANES, LANES), F32)],
        compiler_params=pltpu.CompilerParams(
            dimension_semantics=("arbitrary", "arbitrary", "arbitrary")),
        name="wkv7_scan",
    )(r, v, kk, lw, k, a, r, v, kk, lw, k, a, s0)


def grid_posemb(n_tokens, dtype):
    rows = n_tokens // GRID_W
    row = jnp.repeat(jnp.arange(rows, dtype=F32), GRID_W)
    col = jnp.tile(jnp.arange(GRID_W, dtype=F32), rows)
    quarter = D_MODEL // 4
    freq = 1.0 / (POS_BASE ** (jnp.arange(quarter, dtype=F32) / quarter))
    ar = row[:, None] * freq[None, :]
    ac = col[:, None] * freq[None, :]
    return jnp.concatenate([jnp.sin(ar), jnp.cos(ar), jnp.sin(ac), jnp.cos(ac)], axis=-1).astype(dtype)


def _exclusive_prefix(x):
    rows, seq = x.shape
    tri = jnp.where(lax.broadcasted_iota(jnp.int32, (LANES, LANES), 0)
                    <= lax.broadcasted_iota(jnp.int32, (LANES, LANES), 1), 1.0, 0.0).astype(BF16)
    carry = jnp.zeros((rows, 1), F32)
    out = []
    for blk in range(seq // LANES):
        xb = x[:, blk * LANES:(blk + 1) * LANES]
        inc = jnp.dot(xb.astype(BF16), tri, preferred_element_type=F32)
        out.append(inc - xb + carry)
        carry = carry + inc[:, LANES - 1:LANES]
    return jnp.concatenate(out, axis=1)


SELECT_STEPS = 80


def _select_kernel(aff_ref, slot_ref, *, cap):
    bsz, n_exp, seq = aff_ref.shape
    aff = aff_ref[...].reshape(bsz * n_exp, seq)
    count = lambda t: jnp.sum(jnp.where(aff >= t, 1.0, 0.0), axis=1, keepdims=True)
    lo = jnp.min(aff, axis=1, keepdims=True)
    hi = 2.0 * jnp.max(aff, axis=1, keepdims=True) + 1e-30

    def halve(_, bracket):
        lo, hi = bracket
        mid = lo + 0.5 * (hi - lo)
        ok = count(mid) >= cap
        return jnp.where(ok, mid, lo), jnp.where(ok, hi, mid)

    lo, hi = lax.fori_loop(0, SELECT_STEPS, halve, (lo, hi))
    above = jnp.where(aff >= hi, 1.0, 0.0)
    tied = jnp.where(aff >= lo, 1.0, 0.0) - above
    need = cap - jnp.sum(above, axis=1, keepdims=True)
    keep = above + tied * jnp.where(_exclusive_prefix(tied) < need, 1.0, 0.0)
    slot = _exclusive_prefix(keep)
    slot_ref[...] = jnp.where(keep > 0.5, slot, -1.0).astype(jnp.int32).reshape(bsz, n_exp, seq)


def _expert_kernel(slot_ref, aff_ref, h_ref, wg_ref, wu_ref, wd_ref, y_ref, wg_s, wu_s, wd_s, *, cap):
    @pl.when(pl.program_id(1) == 0)
    def _():
        wg_s[...] = wg_ref[0, 0].astype(BF16)
        wu_s[...] = wu_ref[0, 0].astype(BF16)
        wd_s[...] = wd_ref[0, 0].astype(BF16)

    group, seq, _ = h_ref.shape
    slot_iota = lax.broadcasted_iota(jnp.int32, (cap, seq), 0)
    expert = pl.ds(pl.program_id(0), 1)
    xs, vals = [], []
    for rb in range(group):
        hit = slot_iota == slot_ref[rb, expert, :]
        xs.append(jnp.dot(jnp.where(hit, 1.0, 0.0).astype(BF16), h_ref[rb], preferred_element_type=F32))
        vals.append(jnp.sum(jnp.where(hit, aff_ref[rb, expert, :], 0.0), axis=1, keepdims=True))
    xs = jnp.concatenate(xs, axis=0).astype(BF16)
    g = jnp.dot(xs, wg_s[...], preferred_element_type=F32)
    u = jnp.dot(xs, wu_s[...], preferred_element_type=F32)
    hid = (g * jax.nn.sigmoid(g) * u).astype(BF16)
    y = jnp.dot(hid, wd_s[...], preferred_element_type=F32) * jnp.concatenate(vals, axis=0)
    for rb in range(group):
        y_ref[rb, 0] = y[rb * cap:(rb + 1) * cap].astype(y_ref.dtype)


def _combine_kernel(slot_ref, ys_ref, x_ref, gate_ref, fg_ref, o_ref, *, cap, final):
    rows = x_ref.shape[1]
    n_exp = slot_ref.shape[2]
    if cap >= MXU_DEPTH:
        slot_iota = lax.broadcasted_iota(jnp.int32, (rows, cap), 1)
        moe = jnp.zeros(x_ref.shape[1:], F32)
        for e in range(n_exp):
            onehot = jnp.where(slot_iota == slot_ref[0, :, e:e + 1], 1.0, 0.0).astype(BF16)
            moe = moe + jnp.dot(onehot, ys_ref[0, e * cap:(e + 1) * cap], preferred_element_type=F32)
    else:
        lane = lax.broadcasted_iota(jnp.int32, (n_exp, n_exp * cap), 1)
        first = lax.broadcasted_iota(jnp.int32, (n_exp, n_exp * cap), 0) * cap
        spread = jnp.where(jnp.logical_and(lane >= first, lane < first + cap), 1.0, 0.0).astype(BF16)
        slot_wide = jnp.dot(slot_ref[0].astype(F32).astype(BF16), spread, preferred_element_type=F32)
        within = (lane - first)[0:1].astype(F32)
        for e in range(1, n_exp):
            within = jnp.where(lane[0:1] >= e * cap, (lane - first)[e:e + 1].astype(F32), within)
        onehot = jnp.where(slot_wide == within, 1.0, 0.0).astype(BF16)
        moe = jnp.dot(onehot, ys_ref[0], preferred_element_type=F32)
    out = x_ref[0] + gate_ref[0] * moe
    o_ref[0] = _rms(out, fg_ref[...]) if final else out


MOE_VMEM_BYTES = 56 * 1024 * 1024
COMBINE_ROWS = 512
EXPERT_ROWS = 512


def moe_select(aff_t):
    bsz, n_exp, seq = aff_t.shape
    return pl.pallas_call(
        functools.partial(_select_kernel, cap=float(EC_FACTOR * seq // n_exp)),
        out_shape=jax.ShapeDtypeStruct((bsz, n_exp, seq), jnp.int32),
        compiler_params=pltpu.CompilerParams(vmem_limit_bytes=MOE_VMEM_BYTES),
        name="moe_select",
    )(aff_t)


def moe_experts(slot, aff_t, h, layer, w_gate, w_up, w_down):
    bsz, seq, d = h.shape
    _, n_exp, _, f = w_gate.shape
    cap = EC_FACTOR * seq // n_exp
    group = max(g for g in range(1, min(bsz, max(1, EXPERT_ROWS // cap)) + 1) if bsz % g == 0)
    per_request = pl.BlockSpec((group, n_exp, seq), lambda e, b: (b, 0, 0))
    weight = lambda k, n: pl.BlockSpec((1, 1, k, n), lambda e, b: (layer, e, 0, 0))
    return pl.pallas_call(
        functools.partial(_expert_kernel, cap=cap),
        grid=(n_exp, bsz // group),
        in_specs=[per_request, per_request,
                  pl.BlockSpec((group, seq, d), lambda e, b: (b, 0, 0)),
                  weight(d, f), weight(d, f), weight(f, d)],
        out_specs=pl.BlockSpec((group, 1, cap, d), lambda e, b: (b, e, 0, 0)),
        out_shape=jax.ShapeDtypeStruct((bsz, n_exp, cap, d), BF16),
        scratch_shapes=[pltpu.VMEM((d, f), BF16), pltpu.VMEM((d, f), BF16), pltpu.VMEM((f, d), BF16)],
        compiler_params=pltpu.CompilerParams(dimension_semantics=("arbitrary", "arbitrary"),
                                             vmem_limit_bytes=MOE_VMEM_BYTES),
        name="moe_experts",
    )(slot, aff_t, h, w_gate, w_up, w_down)


def moe_combine(slot, ys, first, x, gate, final_g, final):
    bsz, seq, d = x.shape
    _, n_exp, cap, _ = ys.shape
    rows = min(COMBINE_ROWS, seq)
    return pl.pallas_call(
        functools.partial(_combine_kernel, cap=cap, final=final),
        grid=(bsz, seq // rows),
        in_specs=[pl.BlockSpec((1, rows, n_exp), lambda b, i: (b, i, 0)),
                  pl.BlockSpec((1, n_exp * cap, d), lambda b, i: (b + first, 0, 0)),
                  pl.BlockSpec((1, rows, d), lambda b, i: (b, i, 0)),
                  pl.BlockSpec((1, 1, d), lambda b, i: (b, 0, 0)),
                  pl.BlockSpec((1, d), lambda b, i: (0, 0))],
        out_specs=pl.BlockSpec((1, rows, d), lambda b, i: (b, i, 0)),
        out_shape=jax.ShapeDtypeStruct((bsz, seq, d), F32),
        compiler_params=pltpu.CompilerParams(dimension_semantics=("arbitrary", "arbitrary"),
                                             vmem_limit_bytes=MOE_VMEM_BYTES),
        name="moe_combine",
    )(jnp.swapaxes(slot, 1, 2), ys.reshape(ys.shape[0], n_exp * cap, d), x, gate, _row(final_g))


def _pack_requests(z, m, slot_step=0):
    bsz, n_exp, seq = z.shape
    z = z.reshape(bsz // m, m, n_exp, seq)
    if slot_step:
        shift = (jnp.arange(m, dtype=z.dtype) * slot_step)[None, :, None, None]
        z = jnp.where(z >= 0, z + shift, z)
    return jnp.swapaxes(z, 1, 2).reshape(bsz // m, n_exp, m * seq)


def expert_choice_ffn(xs, gates, hs, affs, layer, w_gate, w_up, w_down, final_g, final):
    slots = [moe_select(aff) for aff in affs]
    long = max(h.shape[1] for h in hs)
    packable = all(long % h.shape[1] == 0 and h.shape[0] % (long // h.shape[1]) == 0 for h in hs)
    if not packable:
        return [moe_combine(slot, moe_experts(slot, aff, h, layer, w_gate, w_up, w_down), 0, x, gate, final_g, final)
                for slot, aff, h, x, gate in zip(slots, affs, hs, xs, gates)]
    n_exp = affs[0].shape[1]
    packed_slot, packed_aff, packed_h, firsts = [], [], [], []
    for slot, aff, h in zip(slots, affs, hs):
        m = long // h.shape[1]
        firsts.append(sum(z.shape[0] for z in packed_h))
        packed_slot.append(_pack_requests(slot, m, EC_FACTOR * h.shape[1] // n_exp))
        packed_aff.append(_pack_requests(aff, m))
        packed_h.append(h.reshape(h.shape[0] // m, long, h.shape[2]))
    ys = moe_experts(jnp.concatenate(packed_slot), jnp.concatenate(packed_aff), jnp.concatenate(packed_h),
                     layer, w_gate, w_up, w_down)
    outs = []
    for slot, h, x, gate, first in zip(slots, hs, xs, gates, firsts):
        m = long // h.shape[1]
        if m == 1:
            outs.append(moe_combine(slot, ys, first, x, gate, final_g, final))
            continue
        cap = EC_FACTOR * h.shape[1] // n_exp
        mine = ys[first:first + h.shape[0] // m].reshape(h.shape[0] // m, n_exp, m, cap, h.shape[2])
        mine = jnp.swapaxes(mine, 1, 2).reshape(h.shape[0], n_exp, cap, h.shape[2])
        outs.append(moe_combine(slot, mine, 0, x, gate, final_g, final))
    return outs


def _mod_kernel(c_ref, w_ref, b_ref, o_ref):
    c = c_ref[...]
    w_hi, w_lo = _hi_lo(w_ref[0])
    o_ref[0] = _split_dot(c * jax.nn.sigmoid(c), w_hi, w_lo) + b_ref[0]


def modulation(cvecs, w_mod, b_mod):
    n, d = cvecs.shape
    depth = w_mod.shape[0]
    rows = -(-n // 8) * 8
    cpad = jnp.pad(cvecs, ((0, rows - n), (0, 0)))
    out = pl.pallas_call(
        _mod_kernel,
        grid=(depth, 6),
        in_specs=[pl.BlockSpec((rows, d), lambda l, n_: (0, 0)),
                  pl.BlockSpec((1, d, d), lambda l, n_: (l, 0, n_)),
                  pl.BlockSpec((1, 1, d), lambda l, n_: (l, 0, n_))],
        out_specs=pl.BlockSpec((1, rows, d), lambda l, n_: (l, 0, n_)),
        out_shape=jax.ShapeDtypeStruct((depth, rows, 6 * d), F32),
        compiler_params=pltpu.CompilerParams(dimension_semantics=("arbitrary", "arbitrary")),
        name="modulation",
    )(cpad, w_mod, b_mod.reshape(depth, 1, 6 * d))
    return out[:, :n].reshape(depth, n, 6, d)


TOKEN_TILE = 512
HALO = 16
LAYER_VMEM_BYTES = 48 * 1024 * 1024


def _rms(x, g):
    return x * lax.rsqrt(jnp.mean(x * x, axis=-1, keepdims=True) + RMS_EPS) * g


def _ln(x, g, b):
    mu = jnp.mean(x, axis=-1, keepdims=True)
    xc = x - mu
    return xc * lax.rsqrt(jnp.mean(xc * xc, axis=-1, keepdims=True) + LN_EPS) * g + b


def _split_dot(x, w_hi, w_lo):
    x_hi = x.astype(BF16)
    x_lo = (x - x_hi.astype(F32)).astype(BF16)
    dot = lambda a, b: jnp.dot(a, b, preferred_element_type=F32)
    return dot(x_hi, w_hi) + dot(x_lo, w_hi) + dot(x_hi, w_lo)


def _with_halo(x_ref, prev_ref, next_ref):
    return jnp.concatenate([prev_ref[0], x_ref[0], next_ref[0]], axis=0)


def _inside_mask(rows):
    i, n = pl.program_id(1), pl.num_programs(1)
    r = lax.broadcasted_iota(jnp.int32, (rows + 2 * HALO, 1), 0)
    return jnp.logical_and(jnp.logical_or(i > 0, r >= HALO), jnp.logical_or(i < n - 1, r < rows + HALO))


def _router(x1, mod_ref, nf_ref, wr_hi_ref, wr_lo_ref, br_ref, h2_ref, aff_ref):
    h2 = _rms(x1, nf_ref[...]) * (1.0 + mod_ref[0, 4:5]) + mod_ref[0, 3:4]
    h2_ref[0] = h2.astype(h2_ref.dtype)
    h_hi = h2.astype(BF16)
    h_lo = (h2 - h_hi.astype(F32)).astype(BF16)
    dot = lambda w, a: lax.dot_general(w, a, (_NT, ((), ())), preferred_element_type=F32)
    logits = dot(wr_hi_ref[...], h_hi) + dot(wr_hi_ref[...], h_lo) + dot(wr_lo_ref[...], h_hi) + br_ref[...]
    e = jnp.exp(logits - jnp.max(logits, axis=0, keepdims=True))
    aff_ref[0] = e / jnp.sum(e, axis=0, keepdims=True)


def _odd_kernel(x_ref, xp_ref, xn_ref, mod_ref, nm_ref, nf_ref, win_ref, cw_ref, cb_ref, cg_ref, cbb_ref,
                vg_ref, vb_ref, ws_ref, bs_ref, wout_ref, wr_hi_ref, wr_lo_ref, br_ref,
                x1_ref, h2_ref, aff_ref, glu_s):
    rows = x_ref.shape[1]
    inside = _inside_mask(rows)
    xa = _with_halo(x_ref, xp_ref, xn_ref)
    h = _rms(xa, nm_ref[...]) * (1.0 + mod_ref[0, 1:2]) + mod_ref[0, 0:1]
    h = jnp.where(inside, h, 0.0).astype(BF16)
    pc = jnp.dot(h, win_ref[:, 0:2 * D_C], preferred_element_type=F32)
    glu = pc[:, 0:D_C] * jax.nn.sigmoid(pc[:, D_C:2 * D_C])
    span = glu_s.shape[1]
    for r in range(SUBLANES):
        glu_s[r] = glu[r:r + span]
    acc = jnp.zeros((rows, D_C), F32)
    for j in range(CONV_WIDTH):
        off = HALO - CONV_WIDTH // 2 + j
        base = off // SUBLANES * SUBLANES
        acc = acc + cw_ref[j:j + 1, :] * glu_s[off % SUBLANES, base:base + rows, :]
    o_c = _ln(acc + cb_ref[...], cg_ref[...], cbb_ref[...])
    o_c = o_c * jax.nn.sigmoid(o_c)
    pd = jax.nn.gelu(jnp.dot(h[HALO:HALO + rows], win_ref[:, 2 * D_C:], preferred_element_type=F32))
    u = pd[:, 0:D_D]
    v = _ln(pd[:, D_D:], vg_ref[...], vb_ref[...]).astype(BF16)
    chunks = []
    for ck in range(rows // CHUNK):
        vc = v[ck * CHUNK:(ck + 1) * CHUNK]
        chunks.append(jnp.concatenate(
            [jnp.dot(ws_ref[hd], vc[:, hd * D_HEAD_DIM:(hd + 1) * D_HEAD_DIM], preferred_element_type=F32)
             for hd in range(D_HEADS)], axis=1) + bs_ref[...])
    o_d = u * jnp.concatenate(chunks, axis=0)
    mixed = jnp.dot(jnp.concatenate([o_c, o_d], axis=1).astype(BF16), wout_ref[...], preferred_element_type=F32)
    x1 = x_ref[0] + mod_ref[0, 2:3] * mixed
    x1_ref[0] = x1
    _router(x1, mod_ref, nf_ref, wr_hi_ref, wr_lo_ref, br_ref, h2_ref, aff_ref)


def _row(v):
    return v.reshape(1, -1)


def _hi_lo(w):
    hi = w.astype(BF16)
    return hi, (w - hi.astype(F32)).astype(BF16)


def _tile_specs(rows, seq, d):
    per = rows // HALO
    last = seq // HALO - 1
    return [pl.BlockSpec((1, rows, d), lambda b, i: (b, i, 0)),
            pl.BlockSpec((1, HALO, d), lambda b, i: (b, jnp.maximum(i * per - 1, 0), 0)),
            pl.BlockSpec((1, HALO, d), lambda b, i: (b, jnp.minimum((i + 1) * per, last), 0))]


def _shared_tile_specs(rows, seq, d):
    per = rows // HALO
    last = seq // HALO - 1
    return [pl.BlockSpec((rows, d), lambda b, i: (i, 0)),
            pl.BlockSpec((HALO, d), lambda b, i: (jnp.maximum(i * per - 1, 0), 0)),
            pl.BlockSpec((HALO, d), lambda b, i: (jnp.minimum((i + 1) * per, last), 0))]


def _full(a):
    return pl.BlockSpec(a.shape, lambda b, i: (0,) * a.ndim)


def odd_layer(x, mod, norm_mix, norm_ffn, w_in, conv_w, conv_b, cln_g, cln_b, vln_g, vln_b, w_s, b_s, w_out,
              w_router, b_router):
    bsz, seq, d = x.shape
    rows = min(TOKEN_TILE, seq)
    n_exp = w_router.shape[1]
    wr_hi, wr_lo = _hi_lo(w_router.T)
    bs_full = jnp.repeat(b_s.T, D_HEAD_DIM, axis=1)
    consts = [_row(norm_mix), _row(norm_ffn), w_in.astype(BF16), conv_w, _row(conv_b), _row(cln_g), _row(cln_b),
              _row(vln_g), _row(vln_b), w_s.astype(BF16), bs_full, w_out.astype(BF16), wr_hi, wr_lo,
              b_router.reshape(-1, 1)]
    tile = lambda w: pl.BlockSpec((1, rows, w), lambda b, i: (b, i, 0))
    return pl.pallas_call(
        _odd_kernel,
        grid=(bsz, seq // rows),
        in_specs=_tile_specs(rows, seq, d) + [pl.BlockSpec((1, 6, d), lambda b, i: (b, 0, 0))]
        + [_full(a) for a in consts],
        out_specs=[tile(d), tile(d), pl.BlockSpec((1, n_exp, rows), lambda b, i: (b, 0, i))],
        out_shape=[jax.ShapeDtypeStruct((bsz, seq, d), F32), jax.ShapeDtypeStruct((bsz, seq, d), BF16),
                   jax.ShapeDtypeStruct((bsz, n_exp, seq), F32)],
        scratch_shapes=[pltpu.VMEM((SUBLANES, rows + 2 * HALO - SUBLANES, D_C), F32)],
        compiler_params=pltpu.CompilerParams(dimension_semantics=("arbitrary", "arbitrary"),
                                             vmem_limit_bytes=LAYER_VMEM_BYTES),
        name="odd_layer",
    )(x, x, x, mod, *consts)


EVEN_TILE = 256
DECAY_SCALE = math.exp(-0.5)


def _head_sums(x, ones_bd):
    hi = x.astype(BF16)
    lo = (x - hi.astype(F32)).astype(BF16)
    dot = lambda a: jnp.dot(a, ones_bd, preferred_element_type=F32)
    return jnp.concatenate([dot(hi[:, g * LANES:(g + 1) * LANES]) + dot(lo[:, g * LANES:(g + 1) * LANES])
                            for g in range(x.shape[1] // LANES)], axis=1)


def _head_ones():
    head_shift = A_HEAD_DIM.bit_length() - 1
    same = ((lax.broadcasted_iota(jnp.int32, (LANES, LANES), 0) >> head_shift)
            == (lax.broadcasted_iota(jnp.int32, (LANES, LANES), 1) >> head_shift))
    return jnp.where(same, 1.0, 0.0).astype(BF16)


def _even_pre_kernel(x_ref, xp_ref, xn_ref, pos_ref, posp_ref, posn_ref, mod_ref, nm_ref, win_ref, mu_rkv_ref,
                     down_h_ref, down_dh_ref, w2_ref, w0_ref, a2_ref, a0_ref, g2_ref, kk_ref, ka_ref,
                     rk_ref, x_out, r_out, v_out, kk_out, lw_out, k_out, a_out, gate_out, bonus_out, u_out,
                     proj_s, h_s):
    rows = x_ref.shape[1]
    inside = _inside_mask(rows)
    xa = _with_halo(x_ref, xp_ref, xn_ref) + jnp.concatenate([posp_ref[...], pos_ref[...], posn_ref[...]], axis=0)
    x_out[0] = xa[HALO:HALO + rows]
    h = _rms(xa, nm_ref[...]) * (1.0 + mod_ref[0, 1:2]) + mod_ref[0, 0:1]
    h = jnp.where(inside, h, 0.0)
    h_s[...] = h
    proj = jnp.dot(h.astype(BF16), win_ref[...], preferred_element_type=F32)
    proj_s[...] = proj[:, 0:3 * D_A]
    u_out[0] = proj[HALO:HALO + rows, 3 * D_A:].astype(u_out.dtype)

    def shifted(ref, lo, hi):
        cur = ref[HALO:HALO + rows, lo:hi]
        return cur, 0.5 * (ref[HALO - 1:HALO - 1 + rows, lo:hi] + ref[HALO + 1:HALO + 1 + rows, lo:hi]) - cur

    r, dr = shifted(proj_s, 0, D_A)
    k, dk = shifted(proj_s, D_A, 2 * D_A)
    v, dv = shifted(proj_s, 2 * D_A, 3 * D_A)
    r = r + dr * mu_rkv_ref[0:1]
    k = k + dk * mu_rkv_ref[1:2]
    v = v + dv * mu_rkv_ref[2:3]
    hc, dh = shifted(h_s, 0, D_MODEL)
    dot = lambda a, b: jnp.dot(a, b, preferred_element_type=F32)
    low = dot(hc.astype(BF16), down_h_ref[...]) + dot(dh.astype(BF16), down_dh_ref[...])
    n_w, n_a = w2_ref.shape[0], a2_ref.shape[0]
    w_pre = w0_ref[...] + dot(jnp.tanh(low[:, 0:n_w]).astype(BF16), w2_ref[...])
    icl = jax.nn.sigmoid(a0_ref[...] + dot(low[:, n_w:n_w + n_a].astype(BF16), a2_ref[...]))
    gate_out[0] = dot(jax.nn.sigmoid(low[:, n_w + n_a:]).astype(BF16), g2_ref[...])
    ones_bd = _head_ones()
    kk = k * kk_ref[...]
    kk = kk / jnp.maximum(jnp.sqrt(_head_sums(kk * kk, ones_bd)), 1e-12)
    r_out[0] = r
    v_out[0] = v.astype(v_out.dtype)
    kk_out[0] = kk
    bonus_out[0] = _head_sums(r * k * rk_ref[...], ones_bd) * v
    for z in range(2):
        a_z = icl[:, z * D_A:(z + 1) * D_A]
        lw_out[z, 0] = -DECAY_SCALE * jax.nn.sigmoid(w_pre[:, z * D_A:(z + 1) * D_A])
        a_out[z, 0] = a_z
        k_out[z, 0] = k * (1.0 + (a_z - 1.0) * ka_ref[...])


def _block_diag2(w):
    z = jnp.zeros_like(w[0])
    return jnp.concatenate([jnp.concatenate([w[0], z], axis=1), jnp.concatenate([z, w[1]], axis=1)], axis=0)


def even_pre(x, pos, mod, norm_mix, w_in, mu_rkv, mu_wag, w0, w1, w2, a0, a1, a2, g1, g2, k_k, k_a, r_k):
    bsz, seq, d = x.shape
    rows = min(EVEN_TILE, seq)
    cat = lambda w: jnp.concatenate([w[0], w[1]], axis=1)
    downs = [cat(w1), cat(a1), g1]
    down_h = jnp.concatenate(downs, axis=1)
    down_dh = jnp.concatenate([mu_wag[i][:, None] * w for i, w in enumerate(downs)], axis=1)
    consts = [_row(norm_mix), w_in.astype(BF16), mu_rkv, down_h.astype(BF16), down_dh.astype(BF16),
              _block_diag2(w2).astype(BF16), _row(w0), _block_diag2(a2).astype(BF16), _row(a0),
              g2.astype(BF16), _row(k_k), _row(k_a), _row(r_k)]
    tile = lambda w: pl.BlockSpec((1, rows, w), lambda b, i: (b, i, 0))
    tile2 = pl.BlockSpec((2, 1, rows, D_A), lambda b, i: (0, b, i, 0))
    sds = lambda w, dt=F32: jax.ShapeDtypeStruct((bsz, seq, w), dt)
    sds2 = jax.ShapeDtypeStruct((2, bsz, seq, D_A), F32)
    return pl.pallas_call(
        _even_pre_kernel,
        grid=(bsz, seq // rows),
        in_specs=_tile_specs(rows, seq, d) + _shared_tile_specs(rows, seq, d)
        + [pl.BlockSpec((1, 6, d), lambda b, i: (b, 0, 0))] + [_full(a) for a in consts],
        out_specs=[tile(d), tile(D_A), tile(D_A), tile(D_A), tile2, tile2, tile2, tile(D_A), tile(D_A), tile(D_B)],
        out_shape=[sds(d), sds(D_A), sds(D_A, BF16), sds(D_A), sds2, sds2, sds2, sds(D_A), sds(D_A), sds(D_B, BF16)],
        scratch_shapes=[pltpu.VMEM((rows + 2 * HALO, 3 * D_A), F32), pltpu.VMEM((rows + 2 * HALO, d), F32)],
        compiler_params=pltpu.CompilerParams(dimension_semantics=("arbitrary", "arbitrary"),
                                             vmem_limit_bytes=LAYER_VMEM_BYTES),
        name="even_pre",
    )(x, x, x, pos, pos, pos, mod, *consts)


FOURIER_ROWS = 512
DFT_SPLIT = 64


def _fourier_kernel(u_ref, f64_ref, fl_ref, o_ref, ucs_s):
    seq = u_ref.shape[1]

    @pl.when(pl.program_id(1) == 0)
    def _():
        ucs = jnp.dot(u_ref[0], f64_ref[...], preferred_element_type=F32)
        ucs_s[0:seq] = ucs[:, 0:D_B].astype(BF16)
        ucs_s[seq:2 * seq] = ucs[:, D_B:].astype(BF16)

    scale = 1.0 / math.sqrt(seq * B_GROUP_DIM)
    o_ref[0] = (jnp.dot(fl_ref[...], ucs_s[...], preferred_element_type=F32) * scale).astype(o_ref.dtype)


def _dft_tables(seq):
    def cs(n):
        i = jnp.arange(n, dtype=jnp.int32)
        ang = ((i[:, None] * i[None, :]) % n).astype(F32) * (2.0 * math.pi / n)
        return jnp.cos(ang), jnp.sin(ang)
    c64, s64 = cs(B_GROUP_DIM)
    eye = jnp.eye(B_GROUPS, dtype=F32)
    f64 = jnp.concatenate([jnp.kron(eye, c64), jnp.kron(eye, s64)], axis=1)
    split = min(DFT_SPLIT, seq)
    s = jnp.arange(seq, dtype=jnp.int32)[None, :]
    ang = lambda t: ((t[:, None] * s) % seq).astype(F32) * (2.0 * math.pi / seq)
    ang_a = ang(jnp.arange(seq // split, dtype=jnp.int32) * split)
    ang_b = ang(jnp.arange(split, dtype=jnp.int32))
    ca, sa = jnp.cos(ang_a)[:, None, :], jnp.sin(ang_a)[:, None, :]
    cb, sb = jnp.cos(ang_b)[None, :, :], jnp.sin(ang_b)[None, :, :]
    cl = (ca * cb - sa * sb).reshape(seq, seq)
    sl = (sa * cb + ca * sb).reshape(seq, seq)
    return f64.astype(BF16), jnp.concatenate([cl, -sl], axis=1).astype(BF16)


def fourier_mixer(u):
    bsz, seq, _ = u.shape
    rows = min(FOURIER_ROWS, seq)
    f64, fl = _dft_tables(seq)
    return pl.pallas_call(
        _fourier_kernel,
        grid=(bsz, seq // rows),
        in_specs=[pl.BlockSpec((1, seq, D_B), lambda b, i: (b, 0, 0)),
                  pl.BlockSpec(f64.shape, lambda b, i: (0, 0)),
                  pl.BlockSpec((rows, 2 * seq), lambda b, i: (i, 0))],
        out_specs=pl.BlockSpec((1, rows, D_B), lambda b, i: (b, i, 0)),
        out_shape=jax.ShapeDtypeStruct((bsz, seq, D_B), BF16),
        scratch_shapes=[pltpu.VMEM((2 * seq, D_B), BF16)],
        compiler_params=pltpu.CompilerParams(dimension_semantics=("arbitrary", "arbitrary"),
                                             vmem_limit_bytes=LAYER_VMEM_BYTES),
        name="fourier_mixer",
    )(u, f64, fl)


def _even_post_kernel(yf_ref, yb_ref, bonus_ref, gate_ref, ob_ref, x_ref, mod_ref, gnw_ref, gnb_ref, wout_ref,
                      nf_ref, wr_hi_ref, wr_lo_ref, br_ref, x1_ref, h2_ref, aff_ref):
    ones_bd = _head_ones()
    y = yf_ref[0] + yb_ref[0]
    mu = _head_sums(y, ones_bd) * (1.0 / A_HEAD_DIM)
    yc = y - mu
    var = _head_sums(yc * yc, ones_bd) * (1.0 / A_HEAD_DIM)
    o_a = (yc * lax.rsqrt(var + GN_EPS) * gnw_ref[...] + gnb_ref[...] + bonus_ref[0]) * gate_ref[0]
    mixed = (jnp.dot(o_a.astype(BF16), wout_ref[0:D_A], preferred_element_type=F32)
             + jnp.dot(ob_ref[0], wout_ref[D_A:], preferred_element_type=F32))
    x1 = x_ref[0] + mod_ref[0, 2:3] * mixed
    x1_ref[0] = x1
    _router(x1, mod_ref, nf_ref, wr_hi_ref, wr_lo_ref, br_ref, h2_ref, aff_ref)


def even_post(yf, yb, bonus, gate, o_b, x, mod, gn_w, gn_b, w_out, norm_ffn, w_router, b_router):
    bsz, seq, d = x.shape
    rows = min(TOKEN_TILE, seq)
    n_exp = w_router.shape[1]
    wr_hi, wr_lo = _hi_lo(w_router.T)
    consts = [_row(gn_w), _row(gn_b), w_out.astype(BF16), _row(norm_ffn), wr_hi, wr_lo, b_router.reshape(-1, 1)]
    tile = lambda w: pl.BlockSpec((1, rows, w), lambda b, i: (b, i, 0))
    return pl.pallas_call(
        _even_post_kernel,
        grid=(bsz, seq // rows),
        in_specs=[tile(D_A), tile(D_A), tile(D_A), tile(D_A), tile(D_B), tile(d),
                  pl.BlockSpec((1, 6, d), lambda b, i: (b, 0, 0))] + [_full(a) for a in consts],
        out_specs=[tile(d), tile(d), pl.BlockSpec((1, n_exp, rows), lambda b, i: (b, 0, i))],
        out_shape=[jax.ShapeDtypeStruct((bsz, seq, d), F32), jax.ShapeDtypeStruct((bsz, seq, d), BF16),
                   jax.ShapeDtypeStruct((bsz, n_exp, seq), F32)],
        compiler_params=pltpu.CompilerParams(dimension_semantics=("arbitrary", "arbitrary"),
                                             vmem_limit_bytes=LAYER_VMEM_BYTES),
        name="even_post",
    )(yf, yb, bonus, gate, o_b, x, mod, *consts)


def mixer_layer(l, x, pos, mod, s_init, P):
    j = l // 2
    if l % 2 == 0:
        x, r, v, kk, lw, kd, a, gate, bonus, u = even_pre(
            x, pos if l == 0 else jnp.zeros_like(pos), mod, P['norm_mix'][l], P['ev_w_in'][j], P['ev_mu_rkv'][j],
            P['ev_mu_wag'][j], P['ev_w0'][j], P['ev_w1'][j], P['ev_w2'][j], P['ev_a0'][j], P['ev_a1'][j],
            P['ev_a2'][j], P['ev_g1'][j], P['ev_g2'][j], P['ev_k_k'][j], P['ev_k_a'][j], P['ev_r_k'][j])
        yf, yb, s_fin = wkv7_scan_pallas(r, v, kk, lw, kd, a, s_init[:, j])
        return even_post(yf, yb, bonus, gate, fourier_mixer(u), x, mod, P['ev_gn_w'][j], P['ev_gn_b'][j],
                         P['ev_w_out'][j], P['norm_ffn'][l], P['moe_router'][l], P['moe_router_b'][l]) + (s_fin,)
    return odd_layer(x, mod, P['norm_mix'][l], P['norm_ffn'][l], P['od_w_in'][j], P['od_conv_w'][j],
                     P['od_conv_b'][j], P['od_cln_g'][j], P['od_cln_b'][j], P['od_vln_g'][j], P['od_vln_b'][j],
                     P['od_w_s'][j], P['od_b_s'][j], P['od_w_out'][j], P['moe_router'][l],
                     P['moe_router_b'][l]) + (None,)


def run_trunks(xs, poss, mods, s_inits, P):
    states = [[] for _ in xs]
    for l in range(DEPTH):
        mod = [jnp.broadcast_to(m[l], (x.shape[0], 6, x.shape[2])) for m, x in zip(mods, xs)]
        x1s, h2s, affs = [], [], []
        for i, x in enumerate(xs):
            x1, h2, aff_t, s_fin = mixer_layer(l, x, poss[i], mod[i], s_inits[i], P)
            x1s.append(x1), h2s.append(h2), affs.append(aff_t)
            if s_fin is not None:
                states[i].append(s_fin)
        xs = expert_choice_ffn(x1s, [m[:, 5:6] for m in mod], h2s, affs, l, P['moe_w_gate'], P['moe_w_up'],
                               P['moe_w_down'], P['final_norm'], l == DEPTH - 1)
    return xs, [jnp.stack(st, axis=1) for st in states]


def kernel(x_prompt, x_sample, state_wkv, c, c_ctx, mod_w, mod_b, norm_mix, norm_ffn, final_norm,
           ev_w_in, ev_w_out, ev_mu_rkv, ev_mu_wag, ev_w0, ev_w1, ev_w2, ev_a0, ev_a1, ev_a2,
           ev_g1, ev_g2, ev_k_k, ev_k_a, ev_r_k, ev_gn_w, ev_gn_b,
           od_w_in, od_w_out, od_conv_w, od_conv_b, od_cln_g, od_cln_b, od_vln_g, od_vln_b,
           od_w_s, od_b_s, moe_router, moe_router_b, moe_w_gate, moe_w_up, moe_w_down):
    P = dict(mod_w=mod_w, mod_b=mod_b, norm_mix=norm_mix, norm_ffn=norm_ffn, final_norm=final_norm,
             ev_w_in=ev_w_in, ev_w_out=ev_w_out, ev_mu_rkv=ev_mu_rkv, ev_mu_wag=ev_mu_wag,
             ev_w0=ev_w0, ev_w1=ev_w1, ev_w2=ev_w2, ev_a0=ev_a0, ev_a1=ev_a1, ev_a2=ev_a2,
             ev_g1=ev_g1, ev_g2=ev_g2, ev_k_k=ev_k_k, ev_k_a=ev_k_a, ev_r_k=ev_r_k,
             ev_gn_w=ev_gn_w, ev_gn_b=ev_gn_b,
             od_w_in=od_w_in, od_w_out=od_w_out, od_conv_w=od_conv_w, od_conv_b=od_conv_b,
             od_cln_g=od_cln_g, od_cln_b=od_cln_b, od_vln_g=od_vln_g, od_vln_b=od_vln_b,
             od_w_s=od_w_s, od_b_s=od_b_s, moe_router=moe_router, moe_router_b=moe_router_b,
             moe_w_gate=moe_w_gate, moe_w_up=moe_w_up, moe_w_down=moe_w_down)
    n_even = state_wkv.shape[1]
    s_zero = jnp.zeros((x_prompt.shape[0], n_even, 2, A_HEADS, A_HEAD_DIM, A_HEAD_DIM), F32)
    mods = modulation(jnp.concatenate([c_ctx[None, :], c], axis=0), mod_w, mod_b)
    no_pos = jnp.zeros(x_prompt.shape[1:], x_prompt.dtype)
    (y_prompt, y_sample), (new_state_wkv, _) = run_trunks(
        [x_prompt, x_sample], [no_pos, grid_posemb(x_sample.shape[1], x_sample.dtype)],
        [mods[:, 0:1], mods[:, 1:]], [s_zero, state_wkv], P)
    return (y_prompt, y_sample, new_state_wkv)
```

```python
import functools
import math

import jax
import jax.numpy as jnp
from jax import lax
from jax.experimental import pallas as pl
from jax.experimental.pallas import tpu as pltpu

D_MODEL = 1024
DEPTH = 2
GRID_W = 64
POS_BASE = 10000.0
A_HEADS = 12
A_HEAD_DIM = 64
D_A = A_HEADS * A_HEAD_DIM
B_GROUPS = 4
B_GROUP_DIM = 64
D_B = B_GROUPS * B_GROUP_DIM
D_C = 512
CONV_WIDTH = 31
D_HEADS = 4
D_HEAD_DIM = 128
D_D = D_HEADS * D_HEAD_DIM
CHUNK = 128
N_EXPERTS = 16
EC_FACTOR = 2
RMS_EPS = 1e-6
LN_EPS = 1e-5
GN_EPS = 64e-5

LANES = 128
SUBLANES = 8
MXU_DEPTH = 256
SCAN_CHUNK = 64
HEADS_PER_TILE = LANES // A_HEAD_DIM
N_HEAD_PAIRS = A_HEADS // HEADS_PER_TILE
PAIRS_PER_STEP = 6
SCAN_REQUESTS = 2

F32 = jnp.float32
BF16 = jnp.bfloat16


def _mm(a, b, dims):
    return lax.dot_general(a.astype(BF16), b.astype(BF16), (dims, ((), ())),
                           preferred_element_type=F32)


_NN = ((1,), (0,))
_NT = ((1,), (1,))
_TN = ((0,), (0,))


def _cumsum_rows(tri, x):
    x1 = x.astype(BF16)
    r1 = x - x1.astype(F32)
    x2 = r1.astype(BF16)
    x3 = (r1 - x2.astype(F32)).astype(BF16)
    dot = lambda y: lax.dot_general(tri, y, (_NN, ((), ())), preferred_element_type=F32)
    return dot(x1) + dot(x2) + dot(x3)


def _tile_block_diag(x, mask):
    n = x.shape[1] // x.shape[0]
    return jnp.where(mask, jnp.concatenate([x] * n, axis=0), 0.0)


def _unit_triangular_inverses(mats, same16, same32, bd_mask):
    c = mats[0].shape[0]
    bd = lambda x: _tile_block_diag(x, bd_mask)
    lane = lax.broadcasted_iota(jnp.int32, mats[0].shape, 1) & (c - 1)
    eye = (lax.broadcasted_iota(jnp.int32, mats[0].shape, 0) == lane).astype(F32)
    pws = [jnp.where(same16, -a, 0.0) for a in mats]
    ts = [eye + n for n in pws]
    for _ in range(3):
        pws = [_mm(pw, bd(pw), _NN) for pw in pws]
        ts = [t + _mm(t, bd(pw), _NN) for t, pw in zip(ts, pws)]
    in32 = jnp.logical_and(same32, jnp.logical_not(same16))
    tmp = [_mm(t, bd(jnp.where(in32, a, 0.0)), _NN) for t, a in zip(ts, mats)]
    ts = [t - _mm(x, bd(t), _NN) for t, x in zip(ts, tmp)]
    tmp = [_mm(t, bd(jnp.where(same32, 0.0, a)), _NN) for t, a in zip(ts, mats)]
    ts = [t - _mm(x, bd(t), _NN) for t, x in zip(ts, tmp)]
    return ts


INVERSE_GROUP = 2


def _wkv_chunks(items):
    c = items[0][0].shape[0]
    iota = lambda shape, axis: lax.broadcasted_iota(jnp.int32, shape, axis)
    t1, s1 = iota((c, c), 0), iota((c, c), 1)
    t2, s2 = iota((c, LANES), 0), iota((c, LANES), 1) & (c - 1)
    t4, s4 = iota((c, 2 * LANES), 0), iota((c, 2 * LANES), 1) & (c - 1)
    tri = {False: (t1 >= s1).astype(F32).astype(BF16), True: (t1 <= s1).astype(F32).astype(BF16)}
    strict2 = {False: t2 > s2, True: t2 < s2}
    incl4 = {False: t4 >= s4, True: t4 <= s4}
    wide = (c, INVERSE_GROUP * LANES)
    tw, sw = iota(wide, 0), iota(wide, 1) & (c - 1)
    same16 = (tw >> 4) == (sw >> 4)
    same32 = (tw >> 5) == (sw >> 5)
    head_shift = A_HEAD_DIM.bit_length() - 1
    bd_mask = lambda n: (iota((n, n), 0) >> head_shift) == (iota((n, n), 1) >> head_shift)
    bd2_mask, bdw_mask = bd_mask(LANES), bd_mask(wide[1])
    bd2 = lambda x: _tile_block_diag(x, bd2_mask)

    cs_all = [_cumsum_rows(tri[rev], lw) for (_, _, _, lw, _, _, _, rev) in items]
    prep = []
    for (r, v, kk, lw, k, a, s, rev), cs in zip(items, cs_all):
        cs_end = cs[0:1] if rev else cs[c - 1:c]
        b = kk * a
        g_inv = jnp.exp(-cs)
        g_tail = jnp.exp(cs_end - cs)
        kt = kk * jnp.exp(cs - lw)
        rt = r * jnp.exp(cs)
        lhs = jnp.concatenate([kt, rt], axis=0)
        kb_inv = jnp.concatenate([bd2(k * g_inv), bd2(b * g_inv)], axis=0)
        kb_tail = jnp.concatenate([k * g_tail, b * g_tail], axis=0)
        prep.append((lhs, kb_inv, kb_tail, jnp.exp(cs_end)))
    ps = [_mm(pr[0], it[6], _NT) for pr, it in zip(prep, items)]
    gs = [_mm(pr[0], pr[1], _NT) for pr in prep]
    a_kbs = [jnp.where(strict2[it[7]], g[0:c, LANES:2 * LANES], 0.0) for g, it in zip(gs, items)]
    groups = [jnp.concatenate(a_kbs[i:i + INVERSE_GROUP], axis=1) for i in range(0, len(items), INVERSE_GROUP)]
    t_groups = _unit_triangular_inverses(groups, same16, same32, bdw_mask)
    ts = [tg[:, j * LANES:(j + 1) * LANES] for tg in t_groups for j in range(INVERSE_GROUP)]
    ws = [p[0:c] + _mm(jnp.where(strict2[it[7]], g[0:c, 0:LANES], 0.0), bd2(it[1]), _NN)
          for p, g, it in zip(ps, gs, items)]
    us = [_mm(t, bd2(w), _NN) for t, w in zip(ts, ws)]
    ys = [p[c:2 * c] + _mm(jnp.where(incl4[it[7]], g[c:2 * c], 0.0),
                           jnp.concatenate([bd2(it[1]), bd2(-u)], axis=0), _NN)
          for p, g, it, u in zip(ps, gs, items, us)]
    dss = [_mm(jnp.concatenate([it[1], -u], axis=0), pr[2], _TN)
           for it, u, pr in zip(items, us, prep)]
    return [(y, it[6] * pr[3] + jnp.where(bd2_mask, ds, 0.0))
            for y, it, pr, ds in zip(ys, items, prep, dss)]


def _wkv_kernel(rf, vf, kkf, lwf, kf, af, rb, vb, kkb, lwb, kb, ab, s0_ref,
                yf_ref, yb_ref, sfin_ref, st_s):
    n = A_HEAD_DIM
    where = [(q, p) for q in range(SCAN_REQUESTS) for p in range(PAIRS_PER_STEP)]

    @pl.when(pl.program_id(2) == 0)
    def _():
        zero = jnp.zeros((n, n), F32)
        for q, p in where:
            for z in range(2):
                st_s[q, z, p] = jnp.concatenate(
                    [jnp.concatenate([s0_ref[q, z, HEADS_PER_TILE * p], zero], axis=1),
                     jnp.concatenate([zero, s0_ref[q, z, HEADS_PER_TILE * p + 1]], axis=1)], axis=0)

    items = []
    for q, p in where:
        ln = slice(p * LANES, (p + 1) * LANES)
        items.append((rf[q, :, ln], vf[q, :, ln].astype(F32), kkf[q, :, ln], lwf[0, q, :, ln], kf[0, q, :, ln],
                      af[0, q, :, ln], st_s[q, 0, p], False))
        items.append((rb[q, :, ln], vb[q, :, ln].astype(F32), kkb[q, :, ln], lwb[0, q, :, ln], kb[0, q, :, ln],
                      ab[0, q, :, ln], st_s[q, 1, p], True))
    out = _wkv_chunks(items)
    for i, (q, p) in enumerate(where):
        ln = slice(p * LANES, (p + 1) * LANES)
        yf_ref[q, :, ln], st_s[q, 0, p] = out[2 * i]
        yb_ref[q, :, ln], st_s[q, 1, p] = out[2 * i + 1]

    @pl.when(pl.program_id(2) == pl.num_programs(2) - 1)
    def _():
        for q, p in where:
            for z in range(2):
                s = st_s[q, z, p]
                sfin_ref[q, z, HEADS_PER_TILE * p] = s[0:n, 0:n]
                sfin_ref[q, z, HEADS_PER_TILE * p + 1] = s[n:2 * n, n:2 * n]


def wkv7_scan_pallas(r, v, kk, lw, k, a, s0):
    bsz, seq, _ = r.shape
    c = SCAN_CHUNK
    nc = seq // c
    n = A_HEAD_DIM
    wd = PAIRS_PER_STEP * LANES
    nq = SCAN_REQUESTS
    heads = HEADS_PER_TILE * PAIRS_PER_STEP
    shared_f = pl.BlockSpec((nq, c, wd), lambda b, p, i: (b, i, p))
    shared_b = pl.BlockSpec((nq, c, wd), lambda b, p, i: (b, nc - 1 - i, p))
    dir_f = pl.BlockSpec((1, nq, c, wd), lambda b, p, i: (0, b, i, p))
    dir_b = pl.BlockSpec((1, nq, c, wd), lambda b, p, i: (1, b, nc - 1 - i, p))
    st = pl.BlockSpec((nq, 2, heads, n, n), lambda b, p, i: (b, 0, p, 0, 0))
    return pl.pallas_call(
        _wkv_kernel,
        grid=(bsz // SCAN_REQUESTS, N_HEAD_PAIRS // PAIRS_PER_STEP, nc),
        in_specs=[shared_f, shared_f, shared_f, dir_f, dir_f, dir_f,
                  shared_b, shared_b, shared_b, dir_b, dir_b, dir_b, st],
        out_specs=[shared_f, shared_b, st],
        out_shape=[jax.ShapeDtypeStruct((bsz, seq, D_A), F32),
                   jax.ShapeDtypeStruct((bsz, seq, D_A), F32),
                   jax.ShapeDtypeStruct((bsz, 2, A_HEADS, n, n), F32)],
        scratch_shapes=[pltpu.VMEM((nq, 2, PAIRS_PER_STEP, LANES, LANES), F32)],
        compiler_params=pltpu.CompilerParams(
            dimension_semantics=("arbitrary", "arbitrary", "arbitrary")),
        name="wkv7_scan",
    )(r, v, kk, lw, k, a, r, v, kk, lw, k, a, s0)


def grid_posemb(n_tokens, dtype):
    rows = n_tokens // GRID_W
    row = jnp.repeat(jnp.arange(rows, dtype=F32), GRID_W)
    col = jnp.tile(jnp.arange(GRID_W, dtype=F32), rows)
    quarter = D_MODEL // 4
    freq = 1.0 / (POS_BASE ** (jnp.arange(quarter, dtype=F32) / quarter))
    ar = row[:, None] * freq[None, :]
    ac = col[:, None] * freq[None, :]
    return jnp.concatenate([jnp.sin(ar), jnp.cos(ar), jnp.sin(ac), jnp.cos(ac)], axis=-1).astype(dtype)


def _exclusive_prefix(x):
    rows, seq = x.shape
    tri = jnp.where(lax.broadcasted_iota(jnp.int32, (LANES, LANES), 0)
                    <= lax.broadcasted_iota(jnp.int32, (LANES, LANES), 1), 1.0, 0.0).astype(BF16)
    carry = jnp.zeros((rows, 1), F32)
    out = []
    for blk in range(seq // LANES):
        xb = x[:, blk * LANES:(blk + 1) * LANES]
        inc = jnp.dot(xb.astype(BF16), tri, preferred_element_type=F32)
        out.append(inc - xb + carry)
        carry = carry + inc[:, LANES - 1:LANES]
    return jnp.concatenate(out, axis=1)


SELECT_STEPS = 80


def _select_kernel(aff_ref, slot_ref, *, cap):
    bsz, n_exp, seq = aff_ref.shape
    aff = aff_ref[...].reshape(bsz * n_exp, seq)
    count = lambda t: jnp.sum(jnp.where(aff >= t, 1.0, 0.0), axis=1, keepdims=True)
    lo = jnp.min(aff, axis=1, keepdims=True)
    hi = 2.0 * jnp.max(aff, axis=1, keepdims=True) + 1e-30

    def halve(_, bracket):
        lo, hi = bracket
        mid = lo + 0.5 * (hi - lo)
        ok = count(mid) >= cap
        return jnp.where(ok, mid, lo), jnp.where(ok, hi, mid)

    lo, hi = lax.fori_loop(0, SELECT_STEPS, halve, (lo, hi))
    above = jnp.where(aff >= hi, 1.0, 0.0)
    tied = jnp.where(aff >= lo, 1.0, 0.0) - above
    need = cap - jnp.sum(above, axis=1, keepdims=True)
    keep = above + tied * jnp.where(_exclusive_prefix(tied) < need, 1.0, 0.0)
    slot = _exclusive_prefix(keep)
    slot_ref[...] = jnp.where(keep > 0.5, slot, -1.0).astype(jnp.int32).reshape(bsz, n_exp, seq)


def _expert_kernel(slot_ref, aff_ref, h_ref, wg_ref, wu_ref, wd_ref, y_ref, wg_s, wu_s, wd_s, *, cap):
    @pl.when(pl.program_id(1) == 0)
    def _():
        wg_s[...] = wg_ref[0, 0].astype(BF16)
        wu_s[...] = wu_ref[0, 0].astype(BF16)
        wd_s[...] = wd_ref[0, 0].astype(BF16)

    group, seq, _ = h_ref.shape
    slot_iota = lax.broadcasted_iota(jnp.int32, (cap, seq), 0)
    expert = pl.ds(pl.program_id(0), 1)
    xs, vals = [], []
    for rb in range(group):
        hit = slot_iota == slot_ref[rb, expert, :]
        xs.append(jnp.dot(jnp.where(hit, 1.0, 0.0).astype(BF16), h_ref[rb], preferred_element_type=F32))
        vals.append(jnp.sum(jnp.where(hit, aff_ref[rb, expert, :], 0.0), axis=1, keepdims=True))
    xs = jnp.concatenate(xs, axis=0).astype(BF16)
    g = jnp.dot(xs, wg_s[...], preferred_element_type=F32)
    u = jnp.dot(xs, wu_s[...], preferred_element_type=F32)
    hid = (g * jax.nn.sigmoid(g) * u).astype(BF16)
    y = jnp.dot(hid, wd_s[...], preferred_element_type=F32) * jnp.concatenate(vals, axis=0)
    for rb in range(group):
        y_ref[rb, 0] = y[rb * cap:(rb + 1) * cap].astype(y_ref.dtype)


def _combine_kernel(slot_ref, ys_ref, x_ref, gate_ref, fg_ref, o_ref, *, cap, final):
    rows = x_ref.shape[1]
    n_exp = slot_ref.shape[2]
    if cap >= MXU_DEPTH:
        slot_iota = lax.broadcasted_iota(jnp.int32, (rows, cap), 1)
        moe = jnp.zeros(x_ref.shape[1:], F32)
        for e in range(n_exp):
            onehot = jnp.where(slot_iota == slot_ref[0, :, e:e + 1], 1.0, 0.0).astype(BF16)
            moe = moe + jnp.dot(onehot, ys_ref[0, e * cap:(e + 1) * cap], preferred_element_type=F32)
    else:
        lane = lax.broadcasted_iota(jnp.int32, (n_exp, n_exp * cap), 1)
        first = lax.broadcasted_iota(jnp.int32, (n_exp, n_exp * cap), 0) * cap
        spread = jnp.where(jnp.logical_and(lane >= first, lane < first + cap), 1.0, 0.0).astype(BF16)
        slot_wide = jnp.dot(slot_ref[0].astype(F32).astype(BF16), spread, preferred_element_type=F32)
        within = (lane - first)[0:1].astype(F32)
        for e in range(1, n_exp):
            within = jnp.where(lane[0:1] >= e * cap, (lane - first)[e:e + 1].astype(F32), within)
        onehot = jnp.where(slot_wide == within, 1.0, 0.0).astype(BF16)
        moe = jnp.dot(onehot, ys_ref[0], preferred_element_type=F32)
    out = x_ref[0] + gate_ref[0] * moe
    o_ref[0] = _rms(out, fg_ref[...]) if final else out


MOE_VMEM_BYTES = 56 * 1024 * 1024
COMBINE_ROWS = 512
EXPERT_ROWS = 512


def expert_choice_ffn(x, gate, h, aff_t, layer, w_gate, w_up, w_down, final_g, final):
    bsz, seq, d = h.shape
    _, n_exp, _, f = w_gate.shape
    cap = EC_FACTOR * seq // n_exp
    slot = pl.pallas_call(
        functools.partial(_select_kernel, cap=float(cap)),
        out_shape=jax.ShapeDtypeStruct((bsz, n_exp, seq), jnp.int32),
        compiler_params=pltpu.CompilerParams(vmem_limit_bytes=MOE_VMEM_BYTES),
        name="moe_select",
    )(aff_t)
    group = min(bsz, max(1, EXPERT_ROWS // cap))
    per_request = pl.BlockSpec((group, n_exp, seq), lambda e, b: (b, 0, 0))
    weight = lambda k, n: pl.BlockSpec((1, 1, k, n), lambda e, b: (layer, e, 0, 0))
    ys = pl.pallas_call(
        functools.partial(_expert_kernel, cap=cap),
        grid=(n_exp, bsz // group),
        in_specs=[per_request, per_request,
                  pl.BlockSpec((group, seq, d), lambda e, b: (b, 0, 0)),
                  weight(d, f), weight(d, f), weight(f, d)],
        out_specs=pl.BlockSpec((group, 1, cap, d), lambda e, b: (b, e, 0, 0)),
        out_shape=jax.ShapeDtypeStruct((bsz, n_exp, cap, d), BF16),
        scratch_shapes=[pltpu.VMEM((d, f), BF16), pltpu.VMEM((d, f), BF16), pltpu.VMEM((f, d), BF16)],
        compiler_params=pltpu.CompilerParams(dimension_semantics=("arbitrary", "arbitrary"),
                                             vmem_limit_bytes=MOE_VMEM_BYTES),
        name="moe_experts",
    )(slot, aff_t, h, w_gate, w_up, w_down)
    rows = min(COMBINE_ROWS, seq)
    return pl.pallas_call(
        functools.partial(_combine_kernel, cap=cap, final=final),
        grid=(bsz, seq // rows),
        in_specs=[pl.BlockSpec((1, rows, n_exp), lambda b, i: (b, i, 0)),
                  pl.BlockSpec((1, n_exp * cap, d), lambda b, i: (b, 0, 0)),
                  pl.BlockSpec((1, rows, d), lambda b, i: (b, i, 0)),
                  pl.BlockSpec((1, 1, d), lambda b, i: (b, 0, 0)),
                  pl.BlockSpec((1, d), lambda b, i: (0, 0))],
        out_specs=pl.BlockSpec((1, rows, d), lambda b, i: (b, i, 0)),
        out_shape=jax.ShapeDtypeStruct((bsz, seq, d), F32),
        compiler_params=pltpu.CompilerParams(dimension_semantics=("arbitrary", "arbitrary"),
                                             vmem_limit_bytes=MOE_VMEM_BYTES),
        name="moe_combine",
    )(jnp.swapaxes(slot, 1, 2), ys.reshape(bsz, n_exp * cap, d), x, gate, _row(final_g))


def _mod_kernel(c_ref, w_ref, b_ref, o_ref):
    c = c_ref[...]
    w_hi, w_lo = _hi_lo(w_ref[0])
    o_ref[0] = _split_dot(c * jax.nn.sigmoid(c), w_hi, w_lo) + b_ref[0]


def modulation(cvecs, w_mod, b_mod):
    n, d = cvecs.shape
    depth = w_mod.shape[0]
    rows = -(-n // 8) * 8
    cpad = jnp.pad(cvecs, ((0, rows - n), (0, 0)))
    out = pl.pallas_call(
        _mod_kernel,
        grid=(depth, 6),
        in_specs=[pl.BlockSpec((rows, d), lambda l, n_: (0, 0)),
                  pl.BlockSpec((1, d, d), lambda l, n_: (l, 0, n_)),
                  pl.BlockSpec((1, 1, d), lambda l, n_: (l, 0, n_))],
        out_specs=pl.BlockSpec((1, rows, d), lambda l, n_: (l, 0, n_)),
        out_shape=jax.ShapeDtypeStruct((depth, rows, 6 * d), F32),
        compiler_params=pltpu.CompilerParams(dimension_semantics=("arbitrary", "arbitrary")),
        name="modulation",
    )(cpad, w_mod, b_mod.reshape(depth, 1, 6 * d))
    return out[:, :n].reshape(depth, n, 6, d)


TOKEN_TILE = 512
HALO = 16
LAYER_VMEM_BYTES = 48 * 1024 * 1024


def _rms(x, g):
    return x * lax.rsqrt(jnp.mean(x * x, axis=-1, keepdims=True) + RMS_EPS) * g


def _ln(x, g, b):
    mu = jnp.mean(x, axis=-1, keepdims=True)
    xc = x - mu
    return xc * lax.rsqrt(jnp.mean(xc * xc, axis=-1, keepdims=True) + LN_EPS) * g + b


def _split_dot(x, w_hi, w_lo):
    x_hi = x.astype(BF16)
    x_lo = (x - x_hi.astype(F32)).astype(BF16)
    dot = lambda a, b: jnp.dot(a, b, preferred_element_type=F32)
    return dot(x_hi, w_hi) + dot(x_lo, w_hi) + dot(x_hi, w_lo)


def _with_halo(x_ref, prev_ref, next_ref):
    return jnp.concatenate([prev_ref[0], x_ref[0], next_ref[0]], axis=0)


def _inside_mask(rows):
    i, n = pl.program_id(1), pl.num_programs(1)
    r = lax.broadcasted_iota(jnp.int32, (rows + 2 * HALO, 1), 0)
    return jnp.logical_and(jnp.logical_or(i > 0, r >= HALO), jnp.logical_or(i < n - 1, r < rows + HALO))


def _router(x1, mod_ref, nf_ref, wr_hi_ref, wr_lo_ref, br_ref, h2_ref, aff_ref):
    h2 = _rms(x1, nf_ref[...]) * (1.0 + mod_ref[0, 4:5]) + mod_ref[0, 3:4]
    h2_ref[0] = h2.astype(h2_ref.dtype)
    h_hi = h2.astype(BF16)
    h_lo = (h2 - h_hi.astype(F32)).astype(BF16)
    dot = lambda w, a: lax.dot_general(w, a, (_NT, ((), ())), preferred_element_type=F32)
    logits = dot(wr_hi_ref[...], h_hi) + dot(wr_hi_ref[...], h_lo) + dot(wr_lo_ref[...], h_hi) + br_ref[...]
    e = jnp.exp(logits - jnp.max(logits, axis=0, keepdims=True))
    aff_ref[0] = e / jnp.sum(e, axis=0, keepdims=True)


def _odd_kernel(x_ref, xp_ref, xn_ref, mod_ref, nm_ref, nf_ref, win_ref, cw_ref, cb_ref, cg_ref, cbb_ref,
                vg_ref, vb_ref, ws_ref, bs_ref, wout_ref, wr_hi_ref, wr_lo_ref, br_ref,
                x1_ref, h2_ref, aff_ref, glu_s):
    rows = x_ref.shape[1]
    inside = _inside_mask(rows)
    xa = _with_halo(x_ref, xp_ref, xn_ref)
    h = _rms(xa, nm_ref[...]) * (1.0 + mod_ref[0, 1:2]) + mod_ref[0, 0:1]
    h = jnp.where(inside, h, 0.0).astype(BF16)
    pc = jnp.dot(h, win_ref[:, 0:2 * D_C], preferred_element_type=F32)
    glu = pc[:, 0:D_C] * jax.nn.sigmoid(pc[:, D_C:2 * D_C])
    span = glu_s.shape[1]
    for r in range(SUBLANES):
        glu_s[r] = glu[r:r + span]
    acc = jnp.zeros((rows, D_C), F32)
    for j in range(CONV_WIDTH):
        off = HALO - CONV_WIDTH // 2 + j
        base = off // SUBLANES * SUBLANES
        acc = acc + cw_ref[j:j + 1, :] * glu_s[off % SUBLANES, base:base + rows, :]
    o_c = _ln(acc + cb_ref[...], cg_ref[...], cbb_ref[...])
    o_c = o_c * jax.nn.sigmoid(o_c)
    pd = jax.nn.gelu(jnp.dot(h[HALO:HALO + rows], win_ref[:, 2 * D_C:], preferred_element_type=F32))
    u = pd[:, 0:D_D]
    v = _ln(pd[:, D_D:], vg_ref[...], vb_ref[...]).astype(BF16)
    chunks = []
    for ck in range(rows // CHUNK):
        vc = v[ck * CHUNK:(ck + 1) * CHUNK]
        chunks.append(jnp.concatenate(
            [jnp.dot(ws_ref[hd], vc[:, hd * D_HEAD_DIM:(hd + 1) * D_HEAD_DIM], preferred_element_type=F32)
             for hd in range(D_HEADS)], axis=1) + bs_ref[...])
    o_d = u * jnp.concatenate(chunks, axis=0)
    mixed = jnp.dot(jnp.concatenate([o_c, o_d], axis=1).astype(BF16), wout_ref[...], preferred_element_type=F32)
    x1 = x_ref[0] + mod_ref[0, 2:3] * mixed
    x1_ref[0] = x1
    _router(x1, mod_ref, nf_ref, wr_hi_ref, wr_lo_ref, br_ref, h2_ref, aff_ref)


def _row(v):
    return v.reshape(1, -1)


def _hi_lo(w):
    hi = w.astype(BF16)
    return hi, (w - hi.astype(F32)).astype(BF16)


def _tile_specs(rows, seq, d):
    per = rows // HALO
    last = seq // HALO - 1
    return [pl.BlockSpec((1, rows, d), lambda b, i: (b, i, 0)),
            pl.BlockSpec((1, HALO, d), lambda b, i: (b, jnp.maximum(i * per - 1, 0), 0)),
            pl.BlockSpec((1, HALO, d), lambda b, i: (b, jnp.minimum((i + 1) * per, last), 0))]


def _shared_tile_specs(rows, seq, d):
    per = rows // HALO
    last = seq // HALO - 1
    return [pl.BlockSpec((rows, d), lambda b, i: (i, 0)),
            pl.BlockSpec((HALO, d), lambda b, i: (jnp.maximum(i * per - 1, 0), 0)),
            pl.BlockSpec((HALO, d), lambda b, i: (jnp.minimum((i + 1) * per, last), 0))]


def _full(a):
    return pl.BlockSpec(a.shape, lambda b, i: (0,) * a.ndim)


def odd_layer(x, mod, norm_mix, norm_ffn, w_in, conv_w, conv_b, cln_g, cln_b, vln_g, vln_b, w_s, b_s, w_out,
              w_router, b_router):
    bsz, seq, d = x.shape
    rows = min(TOKEN_TILE, seq)
    n_exp = w_router.shape[1]
    wr_hi, wr_lo = _hi_lo(w_router.T)
    bs_full = jnp.repeat(b_s.T, D_HEAD_DIM, axis=1)
    consts = [_row(norm_mix), _row(norm_ffn), w_in.astype(BF16), conv_w, _row(conv_b), _row(cln_g), _row(cln_b),
              _row(vln_g), _row(vln_b), w_s.astype(BF16), bs_full, w_out.astype(BF16), wr_hi, wr_lo,
              b_router.reshape(-1, 1)]
    tile = lambda w: pl.BlockSpec((1, rows, w), lambda b, i: (b, i, 0))
    return pl.pallas_call(
        _odd_kernel,
        grid=(bsz, seq // rows),
        in_specs=_tile_specs(rows, seq, d) + [pl.BlockSpec((1, 6, d), lambda b, i: (b, 0, 0))]
        + [_full(a) for a in consts],
        out_specs=[tile(d), tile(d), pl.BlockSpec((1, n_exp, rows), lambda b, i: (b, 0, i))],
        out_shape=[jax.ShapeDtypeStruct((bsz, seq, d), F32), jax.ShapeDtypeStruct((bsz, seq, d), BF16),
                   jax.ShapeDtypeStruct((bsz, n_exp, seq), F32)],
        scratch_shapes=[pltpu.VMEM((SUBLANES, rows + 2 * HALO - SUBLANES, D_C), F32)],
        compiler_params=pltpu.CompilerParams(dimension_semantics=("arbitrary", "arbitrary"),
                                             vmem_limit_bytes=LAYER_VMEM_BYTES),
        name="odd_layer",
    )(x, x, x, mod, *consts)


EVEN_TILE = 256
DECAY_SCALE = math.exp(-0.5)


def _head_sums(x, ones_bd):
    hi = x.astype(BF16)
    lo = (x - hi.astype(F32)).astype(BF16)
    dot = lambda a: jnp.dot(a, ones_bd, preferred_element_type=F32)
    return jnp.concatenate([dot(hi[:, g * LANES:(g + 1) * LANES]) + dot(lo[:, g * LANES:(g + 1) * LANES])
                            for g in range(x.shape[1] // LANES)], axis=1)


def _head_ones():
    head_shift = A_HEAD_DIM.bit_length() - 1
    same = ((lax.broadcasted_iota(jnp.int32, (LANES, LANES), 0) >> head_shift)
            == (lax.broadcasted_iota(jnp.int32, (LANES, LANES), 1) >> head_shift))
    return jnp.where(same, 1.0, 0.0).astype(BF16)


def _even_pre_kernel(x_ref, xp_ref, xn_ref, pos_ref, posp_ref, posn_ref, mod_ref, nm_ref, win_ref, mu_rkv_ref,
                     down_h_ref, down_dh_ref, w2_ref, w0_ref, a2_ref, a0_ref, g2_ref, kk_ref, ka_ref,
                     rk_ref, x_out, r_out, v_out, kk_out, lw_out, k_out, a_out, gate_out, bonus_out, u_out,
                     proj_s, h_s):
    rows = x_ref.shape[1]
    inside = _inside_mask(rows)
    xa = _with_halo(x_ref, xp_ref, xn_ref) + jnp.concatenate([posp_ref[...], pos_ref[...], posn_ref[...]], axis=0)
    x_out[0] = xa[HALO:HALO + rows]
    h = _rms(xa, nm_ref[...]) * (1.0 + mod_ref[0, 1:2]) + mod_ref[0, 0:1]
    h = jnp.where(inside, h, 0.0)
    h_s[...] = h
    proj = jnp.dot(h.astype(BF16), win_ref[...], preferred_element_type=F32)
    proj_s[...] = proj[:, 0:3 * D_A]
    u_out[0] = proj[HALO:HALO + rows, 3 * D_A:].astype(u_out.dtype)

    def shifted(ref, lo, hi):
        cur = ref[HALO:HALO + rows, lo:hi]
        return cur, 0.5 * (ref[HALO - 1:HALO - 1 + rows, lo:hi] + ref[HALO + 1:HALO + 1 + rows, lo:hi]) - cur

    r, dr = shifted(proj_s, 0, D_A)
    k, dk = shifted(proj_s, D_A, 2 * D_A)
    v, dv = shifted(proj_s, 2 * D_A, 3 * D_A)
    r = r + dr * mu_rkv_ref[0:1]
    k = k + dk * mu_rkv_ref[1:2]
    v = v + dv * mu_rkv_ref[2:3]
    hc, dh = shifted(h_s, 0, D_MODEL)
    dot = lambda a, b: jnp.dot(a, b, preferred_element_type=F32)
    low = dot(hc.astype(BF16), down_h_ref[...]) + dot(dh.astype(BF16), down_dh_ref[...])
    n_w, n_a = w2_ref.shape[0], a2_ref.shape[0]
    w_pre = w0_ref[...] + dot(jnp.tanh(low[:, 0:n_w]).astype(BF16), w2_ref[...])
    icl = jax.nn.sigmoid(a0_ref[...] + dot(low[:, n_w:n_w + n_a].astype(BF16), a2_ref[...]))
    gate_out[0] = dot(jax.nn.sigmoid(low[:, n_w + n_a:]).astype(BF16), g2_ref[...])
    ones_bd = _head_ones()
    kk = k * kk_ref[...]
    kk = kk / jnp.maximum(jnp.sqrt(_head_sums(kk * kk, ones_bd)), 1e-12)
    r_out[0] = r
    v_out[0] = v.astype(v_out.dtype)
    kk_out[0] = kk
    bonus_out[0] = _head_sums(r * k * rk_ref[...], ones_bd) * v
    for z in range(2):
        a_z = icl[:, z * D_A:(z + 1) * D_A]
        lw_out[z, 0] = -DECAY_SCALE * jax.nn.sigmoid(w_pre[:, z * D_A:(z + 1) * D_A])
        a_out[z, 0] = a_z
        k_out[z, 0] = k * (1.0 + (a_z - 1.0) * ka_ref[...])


def _block_diag2(w):
    z = jnp.zeros_like(w[0])
    return jnp.concatenate([jnp.concatenate([w[0], z], axis=1), jnp.concatenate([z, w[1]], axis=1)], axis=0)


def even_pre(x, pos, mod, norm_mix, w_in, mu_rkv, mu_wag, w0, w1, w2, a0, a1, a2, g1, g2, k_k, k_a, r_k):
    bsz, seq, d = x.shape
    rows = min(EVEN_TILE, seq)
    cat = lambda w: jnp.concatenate([w[0], w[1]], axis=1)
    downs = [cat(w1), cat(a1), g1]
    down_h = jnp.concatenate(downs, axis=1)
    down_dh = jnp.concatenate([mu_wag[i][:, None] * w for i, w in enumerate(downs)], axis=1)
    consts = [_row(norm_mix), w_in.astype(BF16), mu_rkv, down_h.astype(BF16), down_dh.astype(BF16),
              _block_diag2(w2).astype(BF16), _row(w0), _block_diag2(a2).astype(BF16), _row(a0),
              g2.astype(BF16), _row(k_k), _row(k_a), _row(r_k)]
    tile = lambda w: pl.BlockSpec((1, rows, w), lambda b, i: (b, i, 0))
    tile2 = pl.BlockSpec((2, 1, rows, D_A), lambda b, i: (0, b, i, 0))
    sds = lambda w, dt=F32: jax.ShapeDtypeStruct((bsz, seq, w), dt)
    sds2 = jax.ShapeDtypeStruct((2, bsz, seq, D_A), F32)
    return pl.pallas_call(
        _even_pre_kernel,
        grid=(bsz, seq // rows),
        in_specs=_tile_specs(rows, seq, d) + _shared_tile_specs(rows, seq, d)
        + [pl.BlockSpec((1, 6, d), lambda b, i: (b, 0, 0))] + [_full(a) for a in consts],
        out_specs=[tile(d), tile(D_A), tile(D_A), tile(D_A), tile2, tile2, tile2, tile(D_A), tile(D_A), tile(D_B)],
        out_shape=[sds(d), sds(D_A), sds(D_A, BF16), sds(D_A), sds2, sds2, sds2, sds(D_A), sds(D_A), sds(D_B, BF16)],
        scratch_shapes=[pltpu.VMEM((rows + 2 * HALO, 3 * D_A), F32), pltpu.VMEM((rows + 2 * HALO, d), F32)],
        compiler_params=pltpu.CompilerParams(dimension_semantics=("arbitrary", "arbitrary"),
                                             vmem_limit_bytes=LAYER_VMEM_BYTES),
        name="even_pre",
    )(x, x, x, pos, pos, pos, mod, *consts)


FOURIER_ROWS = 512
DFT_SPLIT = 64


def _fourier_kernel(u_ref, f64_ref, fl_ref, o_ref, ucs_s):
    seq = u_ref.shape[1]

    @pl.when(pl.program_id(1) == 0)
    def _():
        ucs = jnp.dot(u_ref[0], f64_ref[...], preferred_element_type=F32)
        ucs_s[0:seq] = ucs[:, 0:D_B].astype(BF16)
        ucs_s[seq:2 * seq] = ucs[:, D_B:].astype(BF16)

    scale = 1.0 / math.sqrt(seq * B_GROUP_DIM)
    o_ref[0] = (jnp.dot(fl_ref[...], ucs_s[...], preferred_element_type=F32) * scale).astype(o_ref.dtype)


def _dft_tables(seq):
    def cs(n):
        i = jnp.arange(n, dtype=jnp.int32)
        ang = ((i[:, None] * i[None, :]) % n).astype(F32) * (2.0 * math.pi / n)
        return jnp.cos(ang), jnp.sin(ang)
    c64, s64 = cs(B_GROUP_DIM)
    eye = jnp.eye(B_GROUPS, dtype=F32)
    f64 = jnp.concatenate([jnp.kron(eye, c64), jnp.kron(eye, s64)], axis=1)
    split = min(DFT_SPLIT, seq)
    s = jnp.arange(seq, dtype=jnp.int32)[None, :]
    ang = lambda t: ((t[:, None] * s) % seq).astype(F32) * (2.0 * math.pi / seq)
    ang_a = ang(jnp.arange(seq // split, dtype=jnp.int32) * split)
    ang_b = ang(jnp.arange(split, dtype=jnp.int32))
    ca, sa = jnp.cos(ang_a)[:, None, :], jnp.sin(ang_a)[:, None, :]
    cb, sb = jnp.cos(ang_b)[None, :, :], jnp.sin(ang_b)[None, :, :]
    cl = (ca * cb - sa * sb).reshape(seq, seq)
    sl = (sa * cb + ca * sb).reshape(seq, seq)
    return f64.astype(BF16), jnp.concatenate([cl, -sl], axis=1).astype(BF16)


def fourier_mixer(u):
    bsz, seq, _ = u.shape
    rows = min(FOURIER_ROWS, seq)
    f64, fl = _dft_tables(seq)
    return pl.pallas_call(
        _fourier_kernel,
        grid=(bsz, seq // rows),
        in_specs=[pl.BlockSpec((1, seq, D_B), lambda b, i: (b, 0, 0)),
                  pl.BlockSpec(f64.shape, lambda b, i: (0, 0)),
                  pl.BlockSpec((rows, 2 * seq), lambda b, i: (i, 0))],
        out_specs=pl.BlockSpec((1, rows, D_B), lambda b, i: (b, i, 0)),
        out_shape=jax.ShapeDtypeStruct((bsz, seq, D_B), BF16),
        scratch_shapes=[pltpu.VMEM((2 * seq, D_B), BF16)],
        compiler_params=pltpu.CompilerParams(dimension_semantics=("arbitrary", "arbitrary"),
                                             vmem_limit_bytes=LAYER_VMEM_BYTES),
        name="fourier_mixer",
    )(u, f64, fl)


def _even_post_kernel(yf_ref, yb_ref, bonus_ref, gate_ref, ob_ref, x_ref, mod_ref, gnw_ref, gnb_ref, wout_ref,
                      nf_ref, wr_hi_ref, wr_lo_ref, br_ref, x1_ref, h2_ref, aff_ref):
    ones_bd = _head_ones()
    y = yf_ref[0] + yb_ref[0]
    mu = _head_sums(y, ones_bd) * (1.0 / A_HEAD_DIM)
    yc = y - mu
    var = _head_sums(yc * yc, ones_bd) * (1.0 / A_HEAD_DIM)
    o_a = (yc * lax.rsqrt(var + GN_EPS) * gnw_ref[...] + gnb_ref[...] + bonus_ref[0]) * gate_ref[0]
    mixed = (jnp.dot(o_a.astype(BF16), wout_ref[0:D_A], preferred_element_type=F32)
             + jnp.dot(ob_ref[0], wout_ref[D_A:], preferred_element_type=F32))
    x1 = x_ref[0] + mod_ref[0, 2:3] * mixed
    x1_ref[0] = x1
    _router(x1, mod_ref, nf_ref, wr_hi_ref, wr_lo_ref, br_ref, h2_ref, aff_ref)


def even_post(yf, yb, bonus, gate, o_b, x, mod, gn_w, gn_b, w_out, norm_ffn, w_router, b_router):
    bsz, seq, d = x.shape
    rows = min(TOKEN_TILE, seq)
    n_exp = w_router.shape[1]
    wr_hi, wr_lo = _hi_lo(w_router.T)
    consts = [_row(gn_w), _row(gn_b), w_out.astype(BF16), _row(norm_ffn), wr_hi, wr_lo, b_router.reshape(-1, 1)]
    tile = lambda w: pl.BlockSpec((1, rows, w), lambda b, i: (b, i, 0))
    return pl.pallas_call(
        _even_post_kernel,
        grid=(bsz, seq // rows),
        in_specs=[tile(D_A), tile(D_A), tile(D_A), tile(D_A), tile(D_B), tile(d),
                  pl.BlockSpec((1, 6, d), lambda b, i: (b, 0, 0))] + [_full(a) for a in consts],
        out_specs=[tile(d), tile(d), pl.BlockSpec((1, n_exp, rows), lambda b, i: (b, 0, i))],
        out_shape=[jax.ShapeDtypeStruct((bsz, seq, d), F32), jax.ShapeDtypeStruct((bsz, seq, d), BF16),
                   jax.ShapeDtypeStruct((bsz, n_exp, seq), F32)],
        compiler_params=pltpu.CompilerParams(dimension_semantics=("arbitrary", "arbitrary"),
                                             vmem_limit_bytes=LAYER_VMEM_BYTES),
        name="even_post",
    )(yf, yb, bonus, gate, o_b, x, mod, *consts)


def run_trunk(x, pos, mods, s_init, P):
    states = []
    for l in range(DEPTH):
        j = l // 2
        mod = jnp.broadcast_to(mods[l], (x.shape[0], 6, x.shape[2]))
        if l % 2 == 0:
            x, r, v, kk, lw, kd, a, gate, bonus, u = even_pre(
                x, pos if l == 0 else jnp.zeros_like(pos), mod, P['norm_mix'][l], P['ev_w_in'][j], P['ev_mu_rkv'][j], P['ev_mu_wag'][j],
                P['ev_w0'][j], P['ev_w1'][j], P['ev_w2'][j], P['ev_a0'][j], P['ev_a1'][j], P['ev_a2'][j],
                P['ev_g1'][j], P['ev_g2'][j], P['ev_k_k'][j], P['ev_k_a'][j], P['ev_r_k'][j])
            yf, yb, s_fin = wkv7_scan_pallas(r, v, kk, lw, kd, a, s_init[:, j])
            states.append(s_fin)
            x1, h2, aff_t = even_post(yf, yb, bonus, gate, fourier_mixer(u), x, mod, P['ev_gn_w'][j], P['ev_gn_b'][j],
                                      P['ev_w_out'][j], P['norm_ffn'][l], P['moe_router'][l], P['moe_router_b'][l])
        else:
            x1, h2, aff_t = odd_layer(x, mod, P['norm_mix'][l], P['norm_ffn'][l], P['od_w_in'][j], P['od_conv_w'][j],
                                      P['od_conv_b'][j], P['od_cln_g'][j], P['od_cln_b'][j], P['od_vln_g'][j],
                                      P['od_vln_b'][j], P['od_w_s'][j], P['od_b_s'][j], P['od_w_out'][j],
                                      P['moe_router'][l], P['moe_router_b'][l])
        x = expert_choice_ffn(x1, mod[:, 5:6], h2, aff_t, l, P['moe_w_gate'], P['moe_w_up'], P['moe_w_down'],
                              P['final_norm'], l == DEPTH - 1)
    return x, jnp.stack(states, axis=1)


def kernel(x_prompt, x_sample, state_wkv, c, c_ctx, mod_w, mod_b, norm_mix, norm_ffn, final_norm,
           ev_w_in, ev_w_out, ev_mu_rkv, ev_mu_wag, ev_w0, ev_w1, ev_w2, ev_a0, ev_a1, ev_a2,
           ev_g1, ev_g2, ev_k_k, ev_k_a, ev_r_k, ev_gn_w, ev_gn_b,
           od_w_in, od_w_out, od_conv_w, od_conv_b, od_cln_g, od_cln_b, od_vln_g, od_vln_b,
           od_w_s, od_b_s, moe_router, moe_router_b, moe_w_gate, moe_w_up, moe_w_down):
    P = dict(mod_w=mod_w, mod_b=mod_b, norm_mix=norm_mix, norm_ffn=norm_ffn, final_norm=final_norm,
             ev_w_in=ev_w_in, ev_w_out=ev_w_out, ev_mu_rkv=ev_mu_rkv, ev_mu_wag=ev_mu_wag,
             ev_w0=ev_w0, ev_w1=ev_w1, ev_w2=ev_w2, ev_a0=ev_a0, ev_a1=ev_a1, ev_a2=ev_a2,
             ev_g1=ev_g1, ev_g2=ev_g2, ev_k_k=ev_k_k, ev_k_a=ev_k_a, ev_r_k=ev_r_k,
             ev_gn_w=ev_gn_w, ev_gn_b=ev_gn_b,
             od_w_in=od_w_in, od_w_out=od_w_out, od_conv_w=od_conv_w, od_conv_b=od_conv_b,
             od_cln_g=od_cln_g, od_cln_b=od_cln_b, od_vln_g=od_vln_g, od_vln_b=od_vln_b,
             od_w_s=od_w_s, od_b_s=od_b_s, moe_router=moe_router, moe_router_b=moe_router_b,
             moe_w_gate=moe_w_gate, moe_w_up=moe_w_up, moe_w_down=moe_w_down)
    n_even = state_wkv.shape[1]
    s_zero = jnp.zeros((x_prompt.shape[0], n_even, 2, A_HEADS, A_HEAD_DIM, A_HEAD_DIM), F32)
    mods = modulation(jnp.concatenate([c_ctx[None, :], c], axis=0), mod_w, mod_b)
    no_pos = jnp.zeros(x_prompt.shape[1:], x_prompt.dtype)
    y_prompt, new_state_wkv = run_trunk(x_prompt, no_pos, mods[:, 0:1], s_zero, P)
    y_sample, _ = run_trunk(x_sample, grid_posemb(x_sample.shape[1], x_sample.dtype), mods[:, 1:], state_wkv, P)
    return (y_prompt, y_sample, new_state_wkv)
```

```python
import functools
import math

import jax
import jax.numpy as jnp
from jax import lax
from jax.experimental import pallas as pl
from jax.experimental.pallas import tpu as pltpu

D_MODEL = 1024
DEPTH = 2
GRID_W = 64
POS_BASE = 10000.0
A_HEADS = 12
A_HEAD_DIM = 64
D_A = A_HEADS * A_HEAD_DIM
B_GROUPS = 4
B_GROUP_DIM = 64
D_B = B_GROUPS * B_GROUP_DIM
D_C = 512
CONV_WIDTH = 31
D_HEADS = 4
D_HEAD_DIM = 128
D_D = D_HEADS * D_HEAD_DIM
CHUNK = 128
N_EXPERTS = 16
EC_FACTOR = 2
RMS_EPS = 1e-6
LN_EPS = 1e-5
GN_EPS = 64e-5

LANES = 128
SUBLANES = 8
MXU_DEPTH = 256
SCAN_CHUNK = 64
HEADS_PER_TILE = LANES // A_HEAD_DIM
N_HEAD_PAIRS = A_HEADS // HEADS_PER_TILE
PAIRS_PER_STEP = 6
SCAN_REQUESTS = 2

F32 = jnp.float32
BF16 = jnp.bfloat16


def _mm(a, b, dims):
    return lax.dot_general(a.astype(BF16), b.astype(BF16), (dims, ((), ())),
                           preferred_element_type=F32)


_NN = ((1,), (0,))
_NT = ((1,), (1,))
_TN = ((0,), (0,))


def _cumsum_rows(tri, x):
    x1 = x.astype(BF16)
    r1 = x - x1.astype(F32)
    x2 = r1.astype(BF16)
    x3 = (r1 - x2.astype(F32)).astype(BF16)
    dot = lambda y: lax.dot_general(tri, y, (_NN, ((), ())), preferred_element_type=F32)
    return dot(x1) + dot(x2) + dot(x3)


def _tile_block_diag(x, mask):
    n = x.shape[1] // x.shape[0]
    return jnp.where(mask, jnp.concatenate([x] * n, axis=0), 0.0)


def _unit_triangular_inverses(mats, same16, same32, bd_mask):
    c = mats[0].shape[0]
    bd = lambda x: _tile_block_diag(x, bd_mask)
    lane = lax.broadcasted_iota(jnp.int32, mats[0].shape, 1) & (c - 1)
    eye = (lax.broadcasted_iota(jnp.int32, mats[0].shape, 0) == lane).astype(F32)
    pws = [jnp.where(same16, -a, 0.0) for a in mats]
    ts = [eye + n for n in pws]
    for _ in range(3):
        pws = [_mm(pw, bd(pw), _NN) for pw in pws]
        ts = [t + _mm(t, bd(pw), _NN) for t, pw in zip(ts, pws)]
    in32 = jnp.logical_and(same32, jnp.logical_not(same16))
    tmp = [_mm(t, bd(jnp.where(in32, a, 0.0)), _NN) for t, a in zip(ts, mats)]
    ts = [t - _mm(x, bd(t), _NN) for t, x in zip(ts, tmp)]
    tmp = [_mm(t, bd(jnp.where(same32, 0.0, a)), _NN) for t, a in zip(ts, mats)]
    ts = [t - _mm(x, bd(t), _NN) for t, x in zip(ts, tmp)]
    return ts


INVERSE_GROUP = 2


def _wkv_chunks(items):
    c = items[0][0].shape[0]
    iota = lambda shape, axis: lax.broadcasted_iota(jnp.int32, shape, axis)
    t1, s1 = iota((c, c), 0), iota((c, c), 1)
    t2, s2 = iota((c, LANES), 0), iota((c, LANES), 1) & (c - 1)
    t4, s4 = iota((c, 2 * LANES), 0), iota((c, 2 * LANES), 1) & (c - 1)
    tri = {False: (t1 >= s1).astype(F32).astype(BF16), True: (t1 <= s1).astype(F32).astype(BF16)}
    strict2 = {False: t2 > s2, True: t2 < s2}
    incl4 = {False: t4 >= s4, True: t4 <= s4}
    wide = (c, INVERSE_GROUP * LANES)
    tw, sw = iota(wide, 0), iota(wide, 1) & (c - 1)
    same16 = (tw >> 4) == (sw >> 4)
    same32 = (tw >> 5) == (sw >> 5)
    head_shift = A_HEAD_DIM.bit_length() - 1
    bd_mask = lambda n: (iota((n, n), 0) >> head_shift) == (iota((n, n), 1) >> head_shift)
    bd2_mask, bdw_mask = bd_mask(LANES), bd_mask(wide[1])
    bd2 = lambda x: _tile_block_diag(x, bd2_mask)

    cs_all = [_cumsum_rows(tri[rev], lw) for (_, _, _, lw, _, _, _, rev) in items]
    prep = []
    for (r, v, kk, lw, k, a, s, rev), cs in zip(items, cs_all):
        cs_end = cs[0:1] if rev else cs[c - 1:c]
        b = kk * a
        g_inv = jnp.exp(-cs)
        g_tail = jnp.exp(cs_end - cs)
        kt = kk * jnp.exp(cs - lw)
        rt = r * jnp.exp(cs)
        lhs = jnp.concatenate([kt, rt], axis=0)
        kb_inv = jnp.concatenate([bd2(k * g_inv), bd2(b * g_inv)], axis=0)
        kb_tail = jnp.concatenate([k * g_tail, b * g_tail], axis=0)
        prep.append((lhs, kb_inv, kb_tail, jnp.exp(cs_end)))
    ps = [_mm(pr[0], it[6], _NT) for pr, it in zip(prep, items)]
    gs = [_mm(pr[0], pr[1], _NT) for pr in prep]
    a_kbs = [jnp.where(strict2[it[7]], g[0:c, LANES:2 * LANES], 0.0) for g, it in zip(gs, items)]
    groups = [jnp.concatenate(a_kbs[i:i + INVERSE_GROUP], axis=1) for i in range(0, len(items), INVERSE_GROUP)]
    t_groups = _unit_triangular_inverses(groups, same16, same32, bdw_mask)
    ts = [tg[:, j * LANES:(j + 1) * LANES] for tg in t_groups for j in range(INVERSE_GROUP)]
    ws = [p[0:c] + _mm(jnp.where(strict2[it[7]], g[0:c, 0:LANES], 0.0), bd2(it[1]), _NN)
          for p, g, it in zip(ps, gs, items)]
    us = [_mm(t, bd2(w), _NN) for t, w in zip(ts, ws)]
    ys = [p[c:2 * c] + _mm(jnp.where(incl4[it[7]], g[c:2 * c], 0.0),
                           jnp.concatenate([bd2(it[1]), bd2(-u)], axis=0), _NN)
          for p, g, it, u in zip(ps, gs, items, us)]
    dss = [_mm(jnp.concatenate([it[1], -u], axis=0), pr[2], _TN)
           for it, u, pr in zip(items, us, prep)]
    return [(y, it[6] * pr[3] + jnp.where(bd2_mask, ds, 0.0))
            for y, it, pr, ds in zip(ys, items, prep, dss)]


def _wkv_kernel(rf, vf, kkf, lwf, kf, af, rb, vb, kkb, lwb, kb, ab, s0_ref,
                yf_ref, yb_ref, sfin_ref, st_s):
    n = A_HEAD_DIM
    where = [(q, p) for q in range(SCAN_REQUESTS) for p in range(PAIRS_PER_STEP)]

    @pl.when(pl.program_id(2) == 0)
    def _():
        zero = jnp.zeros((n, n), F32)
        for q, p in where:
            for z in range(2):
                st_s[q, z, p] = jnp.concatenate(
                    [jnp.concatenate([s0_ref[q, z, HEADS_PER_TILE * p], zero], axis=1),
                     jnp.concatenate([zero, s0_ref[q, z, HEADS_PER_TILE * p + 1]], axis=1)], axis=0)

    items = []
    for q, p in where:
        ln = slice(p * LANES, (p + 1) * LANES)
        items.append((rf[q, :, ln], vf[q, :, ln].astype(F32), kkf[q, :, ln], lwf[0, q, :, ln], kf[0, q, :, ln],
                      af[0, q, :, ln], st_s[q, 0, p], False))
        items.append((rb[q, :, ln], vb[q, :, ln].astype(F32), kkb[q, :, ln], lwb[0, q, :, ln], kb[0, q, :, ln],
                      ab[0, q, :, ln], st_s[q, 1, p], True))
    out = _wkv_chunks(items)
    for i, (q, p) in enumerate(where):
        ln = slice(p * LANES, (p + 1) * LANES)
        yf_ref[q, :, ln], st_s[q, 0, p] = out[2 * i]
        yb_ref[q, :, ln], st_s[q, 1, p] = out[2 * i + 1]

    @pl.when(pl.program_id(2) == pl.num_programs(2) - 1)
    def _():
        for q, p in where:
            for z in range(2):
                s = st_s[q, z, p]
                sfin_ref[q, z, HEADS_PER_TILE * p] = s[0:n, 0:n]
                sfin_ref[q, z, HEADS_PER_TILE * p + 1] = s[n:2 * n, n:2 * n]


def wkv7_scan_pallas(r, v, kk, lw, k, a, s0):
    bsz, seq, _ = r.shape
    c = SCAN_CHUNK
    nc = seq // c
    n = A_HEAD_DIM
    wd = PAIRS_PER_STEP * LANES
    nq = SCAN_REQUESTS
    heads = HEADS_PER_TILE * PAIRS_PER_STEP
    shared_f = pl.BlockSpec((nq, c, wd), lambda b, p, i: (b, i, p))
    shared_b = pl.BlockSpec((nq, c, wd), lambda b, p, i: (b, nc - 1 - i, p))
    dir_f = pl.BlockSpec((1, nq, c, wd), lambda b, p, i: (0, b, i, p))
    dir_b = pl.BlockSpec((1, nq, c, wd), lambda b, p, i: (1, b, nc - 1 - i, p))
    st = pl.BlockSpec((nq, 2, heads, n, n), lambda b, p, i: (b, 0, p, 0, 0))
    return pl.pallas_call(
        _wkv_kernel,
        grid=(bsz // SCAN_REQUESTS, N_HEAD_PAIRS // PAIRS_PER_STEP, nc),
        in_specs=[shared_f, shared_f, shared_f, dir_f, dir_f, dir_f,
                  shared_b, shared_b, shared_b, dir_b, dir_b, dir_b, st],
        out_specs=[shared_f, shared_b, st],
        out_shape=[jax.ShapeDtypeStruct((bsz, seq, D_A), F32),
                   jax.ShapeDtypeStruct((bsz, seq, D_A), F32),
                   jax.ShapeDtypeStruct((bsz, 2, A_HEADS, n, n), F32)],
        scratch_shapes=[pltpu.VMEM((nq, 2, PAIRS_PER_STEP, LANES, LANES), F32)],
        compiler_params=pltpu.CompilerParams(
            dimension_semantics=("arbitrary", "arbitrary", "arbitrary")),
        name="wkv7_scan",
    )(r, v, kk, lw, k, a, r, v, kk, lw, k, a, s0)


def grid_posemb(n_tokens, dtype):
    rows = n_tokens // GRID_W
    row = jnp.repeat(jnp.arange(rows, dtype=F32), GRID_W)
    col = jnp.tile(jnp.arange(GRID_W, dtype=F32), rows)
    quarter = D_MODEL // 4
    freq = 1.0 / (POS_BASE ** (jnp.arange(quarter, dtype=F32) / quarter))
    ar = row[:, None] * freq[None, :]
    ac = col[:, None] * freq[None, :]
    return jnp.concatenate([jnp.sin(ar), jnp.cos(ar), jnp.sin(ac), jnp.cos(ac)], axis=-1).astype(dtype)


def _exclusive_prefix(x):
    rows, seq = x.shape
    tri = jnp.where(lax.broadcasted_iota(jnp.int32, (LANES, LANES), 0)
                    <= lax.broadcasted_iota(jnp.int32, (LANES, LANES), 1), 1.0, 0.0).astype(BF16)
    carry = jnp.zeros((rows, 1), F32)
    out = []
    for blk in range(seq // LANES):
        xb = x[:, blk * LANES:(blk + 1) * LANES]
        inc = jnp.dot(xb.astype(BF16), tri, preferred_element_type=F32)
        out.append(inc - xb + carry)
        carry = carry + inc[:, LANES - 1:LANES]
    return jnp.concatenate(out, axis=1)


SELECT_STEPS = 80


def _select_kernel(aff_ref, slot_ref, *, cap):
    bsz, n_exp, seq = aff_ref.shape
    aff = aff_ref[...].reshape(bsz * n_exp, seq)
    count = lambda t: jnp.sum(jnp.where(aff >= t, 1.0, 0.0), axis=1, keepdims=True)
    lo = jnp.min(aff, axis=1, keepdims=True)
    hi = 2.0 * jnp.max(aff, axis=1, keepdims=True) + 1e-30

    def halve(_, bracket):
        lo, hi = bracket
        mid = lo + 0.5 * (hi - lo)
        ok = count(mid) >= cap
        return jnp.where(ok, mid, lo), jnp.where(ok, hi, mid)

    lo, hi = lax.fori_loop(0, SELECT_STEPS, halve, (lo, hi))
    above = jnp.where(aff >= hi, 1.0, 0.0)
    tied = jnp.where(aff >= lo, 1.0, 0.0) - above
    need = cap - jnp.sum(above, axis=1, keepdims=True)
    keep = above + tied * jnp.where(_exclusive_prefix(tied) < need, 1.0, 0.0)
    slot = _exclusive_prefix(keep)
    slot_ref[...] = jnp.where(keep > 0.5, slot, -1.0).astype(jnp.int32).reshape(bsz, n_exp, seq)


def _expert_kernel(slot_ref, aff_ref, h_ref, wg_ref, wu_ref, wd_ref, y_ref, wg_s, wu_s, wd_s, *, cap):
    @pl.when(pl.program_id(1) == 0)
    def _():
        wg_s[...] = wg_ref[0, 0].astype(BF16)
        wu_s[...] = wu_ref[0, 0].astype(BF16)
        wd_s[...] = wd_ref[0, 0].astype(BF16)

    group, seq, _ = h_ref.shape
    slot_iota = lax.broadcasted_iota(jnp.int32, (cap, seq), 0)
    expert = pl.ds(pl.program_id(0), 1)
    xs, vals = [], []
    for rb in range(group):
        hit = slot_iota == slot_ref[rb, expert, :]
        xs.append(jnp.dot(jnp.where(hit, 1.0, 0.0).astype(BF16), h_ref[rb], preferred_element_type=F32))
        vals.append(jnp.sum(jnp.where(hit, aff_ref[rb, expert, :], 0.0), axis=1, keepdims=True))
    xs = jnp.concatenate(xs, axis=0).astype(BF16)
    g = jnp.dot(xs, wg_s[...], preferred_element_type=F32)
    u = jnp.dot(xs, wu_s[...], preferred_element_type=F32)
    hid = (g * jax.nn.sigmoid(g) * u).astype(BF16)
    y = jnp.dot(hid, wd_s[...], preferred_element_type=F32) * jnp.concatenate(vals, axis=0)
    for rb in range(group):
        y_ref[rb, 0] = y[rb * cap:(rb + 1) * cap].astype(y_ref.dtype)


def _combine_kernel(slot_ref, ys_ref, x_ref, gate_ref, fg_ref, o_ref, *, cap, final):
    rows = x_ref.shape[1]
    n_exp = slot_ref.shape[2]
    if cap >= MXU_DEPTH:
        slot_iota = lax.broadcasted_iota(jnp.int32, (rows, cap), 1)
        moe = jnp.zeros(x_ref.shape[1:], F32)
        for e in range(n_exp):
            onehot = jnp.where(slot_iota == slot_ref[0, :, e:e + 1], 1.0, 0.0).astype(BF16)
            moe = moe + jnp.dot(onehot, ys_ref[0, e * cap:(e + 1) * cap], preferred_element_type=F32)
    else:
        lane = lax.broadcasted_iota(jnp.int32, (n_exp, n_exp * cap), 1)
        first = lax.broadcasted_iota(jnp.int32, (n_exp, n_exp * cap), 0) * cap
        spread = jnp.where(jnp.logical_and(lane >= first, lane < first + cap), 1.0, 0.0).astype(BF16)
        slot_wide = jnp.dot(slot_ref[0].astype(F32).astype(BF16), spread, preferred_element_type=F32)
        within = (lane - first)[0:1].astype(F32)
        for e in range(1, n_exp):
            within = jnp.where(lane[0:1] >= e * cap, (lane - first)[e:e + 1].astype(F32), within)
        onehot = jnp.where(slot_wide == within, 1.0, 0.0).astype(BF16)
        moe = jnp.dot(onehot, ys_ref[0], preferred_element_type=F32)
    out = x_ref[0] + gate_ref[0] * moe
    o_ref[0] = _rms(out, fg_ref[...]) if final else out


MOE_VMEM_BYTES = 56 * 1024 * 1024
COMBINE_ROWS = 512
EXPERT_ROWS = 512


def expert_choice_ffn(x, gate, h, aff_t, layer, w_gate, w_up, w_down, final_g, final):
    bsz, seq, d = h.shape
    _, n_exp, _, f = w_gate.shape
    cap = EC_FACTOR * seq // n_exp
    slot = pl.pallas_call(
        functools.partial(_select_kernel, cap=float(cap)),
        out_shape=jax.ShapeDtypeStruct((bsz, n_exp, seq), jnp.int32),
        compiler_params=pltpu.CompilerParams(vmem_limit_bytes=MOE_VMEM_BYTES),
        name="moe_select",
    )(aff_t)
    group = min(bsz, max(1, EXPERT_ROWS // cap))
    per_request = pl.BlockSpec((group, n_exp, seq), lambda e, b: (b, 0, 0))
    weight = lambda k, n: pl.BlockSpec((1, 1, k, n), lambda e, b: (layer, e, 0, 0))
    ys = pl.pallas_call(
        functools.partial(_expert_kernel, cap=cap),
        grid=(n_exp, bsz // group),
        in_specs=[per_request, per_request,
                  pl.BlockSpec((group, seq, d), lambda e, b: (b, 0, 0)),
                  weight(d, f), weight(d, f), weight(f, d)],
        out_specs=pl.BlockSpec((group, 1, cap, d), lambda e, b: (b, e, 0, 0)),
        out_shape=jax.ShapeDtypeStruct((bsz, n_exp, cap, d), BF16),
        scratch_shapes=[pltpu.VMEM((d, f), BF16), pltpu.VMEM((d, f), BF16), pltpu.VMEM((f, d), BF16)],
        compiler_params=pltpu.CompilerParams(dimension_semantics=("arbitrary", "arbitrary"),
                                             vmem_limit_bytes=MOE_VMEM_BYTES),
        name="moe_experts",
    )(slot, aff_t, h, w_gate, w_up, w_down)
    rows = min(COMBINE_ROWS, seq)
    return pl.pallas_call(
        functools.partial(_combine_kernel, cap=cap, final=final),
        grid=(bsz, seq // rows),
        in_specs=[pl.BlockSpec((1, rows, n_exp), lambda b, i: (b, i, 0)),
                  pl.BlockSpec((1, n_exp * cap, d), lambda b, i: (b, 0, 0)),
                  pl.BlockSpec((1, rows, d), lambda b, i: (b, i, 0)),
                  pl.BlockSpec((1, 1, d), lambda b, i: (b, 0, 0)),
                  pl.BlockSpec((1, d), lambda b, i: (0, 0))],
        out_specs=pl.BlockSpec((1, rows, d), lambda b, i: (b, i, 0)),
        out_shape=jax.ShapeDtypeStruct((bsz, seq, d), F32),
        compiler_params=pltpu.CompilerParams(dimension_semantics=("arbitrary", "arbitrary"),
                                             vmem_limit_bytes=MOE_VMEM_BYTES),
        name="moe_combine",
    )(jnp.swapaxes(slot, 1, 2), ys.reshape(bsz, n_exp * cap, d), x, gate, _row(final_g))


def _mod_kernel(c_ref, w_ref, b_ref, o_ref):
    c = c_ref[...]
    w_hi, w_lo = _hi_lo(w_ref[0])
    o_ref[0] = _split_dot(c * jax.nn.sigmoid(c), w_hi, w_lo) + b_ref[0]


def modulation(cvecs, w_mod, b_mod):
    n, d = cvecs.shape
    depth = w_mod.shape[0]
    rows = -(-n // 8) * 8
    cpad = jnp.pad(cvecs, ((0, rows - n), (0, 0)))
    out = pl.pallas_call(
        _mod_kernel,
        grid=(depth, 6),
        in_specs=[pl.BlockSpec((rows, d), lambda l, n_: (0, 0)),
                  pl.BlockSpec((1, d, d), lambda l, n_: (l, 0, n_)),
                  pl.BlockSpec((1, 1, d), lambda l, n_: (l, 0, n_))],
        out_specs=pl.BlockSpec((1, rows, d), lambda l, n_: (l, 0, n_)),
        out_shape=jax.ShapeDtypeStruct((depth, rows, 6 * d), F32),
        compiler_params=pltpu.CompilerParams(dimension_semantics=("arbitrary", "arbitrary")),
        name="modulation",
    )(cpad, w_mod, b_mod.reshape(depth, 1, 6 * d))
    return out[:, :n].reshape(depth, n, 6, d)


TOKEN_TILE = 512
HALO = 16
LAYER_VMEM_BYTES = 48 * 1024 * 1024


def _rms(x, g):
    return x * lax.rsqrt(jnp.mean(x * x, axis=-1, keepdims=True) + RMS_EPS) * g


def _ln(x, g, b):
    mu = jnp.mean(x, axis=-1, keepdims=True)
    xc = x - mu
    return xc * lax.rsqrt(jnp.mean(xc * xc, axis=-1, keepdims=True) + LN_EPS) * g + b


def _split_dot(x, w_hi, w_lo):
    x_hi = x.astype(BF16)
    x_lo = (x - x_hi.astype(F32)).astype(BF16)
    dot = lambda a, b: jnp.dot(a, b, preferred_element_type=F32)
    return dot(x_hi, w_hi) + dot(x_lo, w_hi) + dot(x_hi, w_lo)


def _with_halo(x_ref, prev_ref, next_ref):
    return jnp.concatenate([prev_ref[0], x_ref[0], next_ref[0]], axis=0)


def _inside_mask(rows):
    i, n = pl.program_id(1), pl.num_programs(1)
    r = lax.broadcasted_iota(jnp.int32, (rows + 2 * HALO, 1), 0)
    return jnp.logical_and(jnp.logical_or(i > 0, r >= HALO), jnp.logical_or(i < n - 1, r < rows + HALO))


def _router(x1, mod_ref, nf_ref, wr_hi_ref, wr_lo_ref, br_ref, h2_ref, aff_ref):
    h2 = _rms(x1, nf_ref[...]) * (1.0 + mod_ref[0, 4:5]) + mod_ref[0, 3:4]
    h2_ref[0] = h2.astype(h2_ref.dtype)
    h_hi = h2.astype(BF16)
    h_lo = (h2 - h_hi.astype(F32)).astype(BF16)
    dot = lambda w, a: lax.dot_general(w, a, (_NT, ((), ())), preferred_element_type=F32)
    logits = dot(wr_hi_ref[...], h_hi) + dot(wr_hi_ref[...], h_lo) + dot(wr_lo_ref[...], h_hi) + br_ref[...]
    e = jnp.exp(logits - jnp.max(logits, axis=0, keepdims=True))
    aff_ref[0] = e / jnp.sum(e, axis=0, keepdims=True)


def _odd_kernel(x_ref, xp_ref, xn_ref, mod_ref, nm_ref, nf_ref, win_ref, cw_ref, cb_ref, cg_ref, cbb_ref,
                vg_ref, vb_ref, ws_ref, bs_ref, wout_ref, wr_hi_ref, wr_lo_ref, br_ref,
                x1_ref, h2_ref, aff_ref, glu_s):
    rows = x_ref.shape[1]
    inside = _inside_mask(rows)
    xa = _with_halo(x_ref, xp_ref, xn_ref)
    h = _rms(xa, nm_ref[...]) * (1.0 + mod_ref[0, 1:2]) + mod_ref[0, 0:1]
    h = jnp.where(inside, h, 0.0).astype(BF16)
    pc = jnp.dot(h, win_ref[:, 0:2 * D_C], preferred_element_type=F32)
    glu = pc[:, 0:D_C] * jax.nn.sigmoid(pc[:, D_C:2 * D_C])
    span = glu_s.shape[1]
    for r in range(SUBLANES):
        glu_s[r] = glu[r:r + span]
    acc = jnp.zeros((rows, D_C), F32)
    for j in range(CONV_WIDTH):
        off = HALO - CONV_WIDTH // 2 + j
        base = off // SUBLANES * SUBLANES
        acc = acc + cw_ref[j:j + 1, :] * glu_s[off % SUBLANES, base:base + rows, :]
    o_c = _ln(acc + cb_ref[...], cg_ref[...], cbb_ref[...])
    o_c = o_c * jax.nn.sigmoid(o_c)
    pd = jax.nn.gelu(jnp.dot(h[HALO:HALO + rows], win_ref[:, 2 * D_C:], preferred_element_type=F32))
    u = pd[:, 0:D_D]
    v = _ln(pd[:, D_D:], vg_ref[...], vb_ref[...]).astype(BF16)
    chunks = []
    for ck in range(rows // CHUNK):
        vc = v[ck * CHUNK:(ck + 1) * CHUNK]
        chunks.append(jnp.concatenate(
            [jnp.dot(ws_ref[hd], vc[:, hd * D_HEAD_DIM:(hd + 1) * D_HEAD_DIM], preferred_element_type=F32)
             for hd in range(D_HEADS)], axis=1) + bs_ref[...])
    o_d = u * jnp.concatenate(chunks, axis=0)
    mixed = jnp.dot(jnp.concatenate([o_c, o_d], axis=1).astype(BF16), wout_ref[...], preferred_element_type=F32)
    x1 = x_ref[0] + mod_ref[0, 2:3] * mixed
    x1_ref[0] = x1
    _router(x1, mod_ref, nf_ref, wr_hi_ref, wr_lo_ref, br_ref, h2_ref, aff_ref)


def _row(v):
    return v.reshape(1, -1)


def _hi_lo(w):
    hi = w.astype(BF16)
    return hi, (w - hi.astype(F32)).astype(BF16)


def _tile_specs(rows, seq, d):
    per = rows // HALO
    last = seq // HALO - 1
    return [pl.BlockSpec((1, rows, d), lambda b, i: (b, i, 0)),
            pl.BlockSpec((1, HALO, d), lambda b, i: (b, jnp.maximum(i * per - 1, 0), 0)),
            pl.BlockSpec((1, HALO, d), lambda b, i: (b, jnp.minimum((i + 1) * per, last), 0))]


def _shared_tile_specs(rows, seq, d):
    per = rows // HALO
    last = seq // HALO - 1
    return [pl.BlockSpec((rows, d), lambda b, i: (i, 0)),
            pl.BlockSpec((HALO, d), lambda b, i: (jnp.maximum(i * per - 1, 0), 0)),
            pl.BlockSpec((HALO, d), lambda b, i: (jnp.minimum((i + 1) * per, last), 0))]


def _full(a):
    return pl.BlockSpec(a.shape, lambda b, i: (0,) * a.ndim)


def odd_layer(x, mod, norm_mix, norm_ffn, w_in, conv_w, conv_b, cln_g, cln_b, vln_g, vln_b, w_s, b_s, w_out,
              w_router, b_router):
    bsz, seq, d = x.shape
    rows = min(TOKEN_TILE, seq)
    n_exp = w_router.shape[1]
    wr_hi, wr_lo = _hi_lo(w_router.T)
    bs_full = jnp.repeat(b_s.T, D_HEAD_DIM, axis=1)
    consts = [_row(norm_mix), _row(norm_ffn), w_in.astype(BF16), conv_w, _row(conv_b), _row(cln_g), _row(cln_b),
              _row(vln_g), _row(vln_b), w_s.astype(BF16), bs_full, w_out.astype(BF16), wr_hi, wr_lo,
              b_router.reshape(-1, 1)]
    tile = lambda w: pl.BlockSpec((1, rows, w), lambda b, i: (b, i, 0))
    return pl.pallas_call(
        _odd_kernel,
        grid=(bsz, seq // rows),
        in_specs=_tile_specs(rows, seq, d) + [pl.BlockSpec((1, 6, d), lambda b, i: (b, 0, 0))]
        + [_full(a) for a in consts],
        out_specs=[tile(d), tile(d), pl.BlockSpec((1, n_exp, rows), lambda b, i: (b, 0, i))],
        out_shape=[jax.ShapeDtypeStruct((bsz, seq, d), F32), jax.ShapeDtypeStruct((bsz, seq, d), BF16),
                   jax.ShapeDtypeStruct((bsz, n_exp, seq), F32)],
        scratch_shapes=[pltpu.VMEM((SUBLANES, rows + 2 * HALO - SUBLANES, D_C), F32)],
        compiler_params=pltpu.CompilerParams(dimension_semantics=("arbitrary", "arbitrary"),
                                             vmem_limit_bytes=LAYER_VMEM_BYTES),
        name="odd_layer",
    )(x, x, x, mod, *consts)


EVEN_TILE = 256
DECAY_SCALE = math.exp(-0.5)


def _head_sums(x, ones_bd):
    xb = x.astype(BF16)
    return jnp.concatenate([jnp.dot(xb[:, g * LANES:(g + 1) * LANES], ones_bd, preferred_element_type=F32)
                            for g in range(x.shape[1] // LANES)], axis=1)


def _head_ones():
    head_shift = A_HEAD_DIM.bit_length() - 1
    same = ((lax.broadcasted_iota(jnp.int32, (LANES, LANES), 0) >> head_shift)
            == (lax.broadcasted_iota(jnp.int32, (LANES, LANES), 1) >> head_shift))
    return jnp.where(same, 1.0, 0.0).astype(BF16)


def _even_pre_kernel(x_ref, xp_ref, xn_ref, pos_ref, posp_ref, posn_ref, mod_ref, nm_ref, win_ref, mu_rkv_ref,
                     down_h_ref, down_dh_ref, w2_ref, w0_ref, a2_ref, a0_ref, g2_ref, kk_ref, ka_ref,
                     rk_ref, x_out, r_out, v_out, kk_out, lw_out, k_out, a_out, gate_out, bonus_out, u_out,
                     proj_s, h_s):
    rows = x_ref.shape[1]
    inside = _inside_mask(rows)
    xa = _with_halo(x_ref, xp_ref, xn_ref) + jnp.concatenate([posp_ref[...], pos_ref[...], posn_ref[...]], axis=0)
    x_out[0] = xa[HALO:HALO + rows]
    h = _rms(xa, nm_ref[...]) * (1.0 + mod_ref[0, 1:2]) + mod_ref[0, 0:1]
    h = jnp.where(inside, h, 0.0)
    h_s[...] = h
    proj = jnp.dot(h.astype(BF16), win_ref[...], preferred_element_type=F32)
    proj_s[...] = proj[:, 0:3 * D_A]
    u_out[0] = proj[HALO:HALO + rows, 3 * D_A:].astype(u_out.dtype)

    def shifted(ref, lo, hi):
        cur = ref[HALO:HALO + rows, lo:hi]
        return cur, 0.5 * (ref[HALO - 1:HALO - 1 + rows, lo:hi] + ref[HALO + 1:HALO + 1 + rows, lo:hi]) - cur

    r, dr = shifted(proj_s, 0, D_A)
    k, dk = shifted(proj_s, D_A, 2 * D_A)
    v, dv = shifted(proj_s, 2 * D_A, 3 * D_A)
    r = r + dr * mu_rkv_ref[0:1]
    k = k + dk * mu_rkv_ref[1:2]
    v = v + dv * mu_rkv_ref[2:3]
    hc, dh = shifted(h_s, 0, D_MODEL)
    dot = lambda a, b: jnp.dot(a, b, preferred_element_type=F32)
    low = dot(hc.astype(BF16), down_h_ref[...]) + dot(dh.astype(BF16), down_dh_ref[...])
    n_w, n_a = w2_ref.shape[0], a2_ref.shape[0]
    w_pre = w0_ref[...] + dot(jnp.tanh(low[:, 0:n_w]).astype(BF16), w2_ref[...])
    icl = jax.nn.sigmoid(a0_ref[...] + dot(low[:, n_w:n_w + n_a].astype(BF16), a2_ref[...]))
    gate_out[0] = dot(jax.nn.sigmoid(low[:, n_w + n_a:]).astype(BF16), g2_ref[...])
    ones_bd = _head_ones()
    kk = k * kk_ref[...]
    kk = kk / jnp.maximum(jnp.sqrt(_head_sums(kk * kk, ones_bd)), 1e-12)
    r_out[0] = r
    v_out[0] = v.astype(v_out.dtype)
    kk_out[0] = kk
    bonus_out[0] = _head_sums(r * k * rk_ref[...], ones_bd) * v
    for z in range(2):
        a_z = icl[:, z * D_A:(z + 1) * D_A]
        lw_out[z, 0] = -DECAY_SCALE * jax.nn.sigmoid(w_pre[:, z * D_A:(z + 1) * D_A])
        a_out[z, 0] = a_z
        k_out[z, 0] = k * (1.0 + (a_z - 1.0) * ka_ref[...])


def _block_diag2(w):
    z = jnp.zeros_like(w[0])
    return jnp.concatenate([jnp.concatenate([w[0], z], axis=1), jnp.concatenate([z, w[1]], axis=1)], axis=0)


def even_pre(x, pos, mod, norm_mix, w_in, mu_rkv, mu_wag, w0, w1, w2, a0, a1, a2, g1, g2, k_k, k_a, r_k):
    bsz, seq, d = x.shape
    rows = min(EVEN_TILE, seq)
    cat = lambda w: jnp.concatenate([w[0], w[1]], axis=1)
    downs = [cat(w1), cat(a1), g1]
    down_h = jnp.concatenate(downs, axis=1)
    down_dh = jnp.concatenate([mu_wag[i][:, None] * w for i, w in enumerate(downs)], axis=1)
    consts = [_row(norm_mix), w_in.astype(BF16), mu_rkv, down_h.astype(BF16), down_dh.astype(BF16),
              _block_diag2(w2).astype(BF16), _row(w0), _block_diag2(a2).astype(BF16), _row(a0),
              g2.astype(BF16), _row(k_k), _row(k_a), _row(r_k)]
    tile = lambda w: pl.BlockSpec((1, rows, w), lambda b, i: (b, i, 0))
    tile2 = pl.BlockSpec((2, 1, rows, D_A), lambda b, i: (0, b, i, 0))
    sds = lambda w, dt=F32: jax.ShapeDtypeStruct((bsz, seq, w), dt)
    sds2 = jax.ShapeDtypeStruct((2, bsz, seq, D_A), F32)
    return pl.pallas_call(
        _even_pre_kernel,
        grid=(bsz, seq // rows),
        in_specs=_tile_specs(rows, seq, d) + _shared_tile_specs(rows, seq, d)
        + [pl.BlockSpec((1, 6, d), lambda b, i: (b, 0, 0))] + [_full(a) for a in consts],
        out_specs=[tile(d), tile(D_A), tile(D_A), tile(D_A), tile2, tile2, tile2, tile(D_A), tile(D_A), tile(D_B)],
        out_shape=[sds(d), sds(D_A), sds(D_A, BF16), sds(D_A), sds2, sds2, sds2, sds(D_A), sds(D_A), sds(D_B, BF16)],
        scratch_shapes=[pltpu.VMEM((rows + 2 * HALO, 3 * D_A), F32), pltpu.VMEM((rows + 2 * HALO, d), F32)],
        compiler_params=pltpu.CompilerParams(dimension_semantics=("arbitrary", "arbitrary"),
                                             vmem_limit_bytes=LAYER_VMEM_BYTES),
        name="even_pre",
    )(x, x, x, pos, pos, pos, mod, *consts)


FOURIER_ROWS = 1024
DFT_SPLIT = 64


def _fourier_kernel(u_ref, f64_ref, fl_ref, o_ref, ucs_s):
    seq = u_ref.shape[1]

    @pl.when(pl.program_id(1) == 0)
    def _():
        ucs = jnp.dot(u_ref[0], f64_ref[...], preferred_element_type=F32)
        ucs_s[0:seq] = ucs[:, 0:D_B].astype(BF16)
        ucs_s[seq:2 * seq] = ucs[:, D_B:].astype(BF16)

    scale = 1.0 / math.sqrt(seq * B_GROUP_DIM)
    o_ref[0] = (jnp.dot(fl_ref[...], ucs_s[...], preferred_element_type=F32) * scale).astype(o_ref.dtype)


def _dft_tables(seq):
    def cs(n):
        i = jnp.arange(n, dtype=jnp.int32)
        ang = ((i[:, None] * i[None, :]) % n).astype(F32) * (2.0 * math.pi / n)
        return jnp.cos(ang), jnp.sin(ang)
    c64, s64 = cs(B_GROUP_DIM)
    eye = jnp.eye(B_GROUPS, dtype=F32)
    f64 = jnp.concatenate([jnp.kron(eye, c64), jnp.kron(eye, s64)], axis=1)
    split = min(DFT_SPLIT, seq)
    s = jnp.arange(seq, dtype=jnp.int32)[None, :]
    ang = lambda t: ((t[:, None] * s) % seq).astype(F32) * (2.0 * math.pi / seq)
    ang_a = ang(jnp.arange(seq // split, dtype=jnp.int32) * split)
    ang_b = ang(jnp.arange(split, dtype=jnp.int32))
    ca, sa = jnp.cos(ang_a)[:, None, :], jnp.sin(ang_a)[:, None, :]
    cb, sb = jnp.cos(ang_b)[None, :, :], jnp.sin(ang_b)[None, :, :]
    cl = (ca * cb - sa * sb).reshape(seq, seq)
    sl = (sa * cb + ca * sb).reshape(seq, seq)
    return f64.astype(BF16), jnp.concatenate([cl, -sl], axis=1).astype(BF16)


def fourier_mixer(u):
    bsz, seq, _ = u.shape
    rows = min(FOURIER_ROWS, seq)
    f64, fl = _dft_tables(seq)
    return pl.pallas_call(
        _fourier_kernel,
        grid=(bsz, seq // rows),
        in_specs=[pl.BlockSpec((1, seq, D_B), lambda b, i: (b, 0, 0)),
                  pl.BlockSpec(f64.shape, lambda b, i: (0, 0)),
                  pl.BlockSpec((rows, 2 * seq), lambda b, i: (i, 0))],
        out_specs=pl.BlockSpec((1, rows, D_B), lambda b, i: (b, i, 0)),
        out_shape=jax.ShapeDtypeStruct((bsz, seq, D_B), BF16),
        scratch_shapes=[pltpu.VMEM((2 * seq, D_B), BF16)],
        compiler_params=pltpu.CompilerParams(dimension_semantics=("arbitrary", "arbitrary"),
                                             vmem_limit_bytes=LAYER_VMEM_BYTES),
        name="fourier_mixer",
    )(u, f64, fl)


def _even_post_kernel(yf_ref, yb_ref, bonus_ref, gate_ref, ob_ref, x_ref, mod_ref, gnw_ref, gnb_ref, wout_ref,
                      nf_ref, wr_hi_ref, wr_lo_ref, br_ref, x1_ref, h2_ref, aff_ref):
    ones_bd = _head_ones()
    y = yf_ref[0] + yb_ref[0]
    mu = _head_sums(y, ones_bd) * (1.0 / A_HEAD_DIM)
    yc = y - mu
    var = _head_sums(yc * yc, ones_bd) * (1.0 / A_HEAD_DIM)
    o_a = (yc * lax.rsqrt(var + GN_EPS) * gnw_ref[...] + gnb_ref[...] + bonus_ref[0]) * gate_ref[0]
    mixed = (jnp.dot(o_a.astype(BF16), wout_ref[0:D_A], preferred_element_type=F32)
             + jnp.dot(ob_ref[0], wout_ref[D_A:], preferred_element_type=F32))
    x1 = x_ref[0] + mod_ref[0, 2:3] * mixed
    x1_ref[0] = x1
    _router(x1, mod_ref, nf_ref, wr_hi_ref, wr_lo_ref, br_ref, h2_ref, aff_ref)


def even_post(yf, yb, bonus, gate, o_b, x, mod, gn_w, gn_b, w_out, norm_ffn, w_router, b_router):
    bsz, seq, d = x.shape
    rows = min(TOKEN_TILE, seq)
    n_exp = w_router.shape[1]
    wr_hi, wr_lo = _hi_lo(w_router.T)
    consts = [_row(gn_w), _row(gn_b), w_out.astype(BF16), _row(norm_ffn), wr_hi, wr_lo, b_router.reshape(-1, 1)]
    tile = lambda w: pl.BlockSpec((1, rows, w), lambda b, i: (b, i, 0))
    return pl.pallas_call(
        _even_post_kernel,
        grid=(bsz, seq // rows),
        in_specs=[tile(D_A), tile(D_A), tile(D_A), tile(D_A), tile(D_B), tile(d),
                  pl.BlockSpec((1, 6, d), lambda b, i: (b, 0, 0))] + [_full(a) for a in consts],
        out_specs=[tile(d), tile(d), pl.BlockSpec((1, n_exp, rows), lambda b, i: (b, 0, i))],
        out_shape=[jax.ShapeDtypeStruct((bsz, seq, d), F32), jax.ShapeDtypeStruct((bsz, seq, d), BF16),
                   jax.ShapeDtypeStruct((bsz, n_exp, seq), F32)],
        compiler_params=pltpu.CompilerParams(dimension_semantics=("arbitrary", "arbitrary"),
                                             vmem_limit_bytes=LAYER_VMEM_BYTES),
        name="even_post",
    )(yf, yb, bonus, gate, o_b, x, mod, *consts)


def run_trunk(x, pos, mods, s_init, P):
    states = []
    for l in range(DEPTH):
        j = l // 2
        mod = jnp.broadcast_to(mods[l], (x.shape[0], 6, x.shape[2]))
        if l % 2 == 0:
            x, r, v, kk, lw, kd, a, gate, bonus, u = even_pre(
                x, pos if l == 0 else jnp.zeros_like(pos), mod, P['norm_mix'][l], P['ev_w_in'][j], P['ev_mu_rkv'][j], P['ev_mu_wag'][j],
                P['ev_w0'][j], P['ev_w1'][j], P['ev_w2'][j], P['ev_a0'][j], P['ev_a1'][j], P['ev_a2'][j],
                P['ev_g1'][j], P['ev_g2'][j], P['ev_k_k'][j], P['ev_k_a'][j], P['ev_r_k'][j])
            yf, yb, s_fin = wkv7_scan_pallas(r, v, kk, lw, kd, a, s_init[:, j])
            states.append(s_fin)
            x1, h2, aff_t = even_post(yf, yb, bonus, gate, fourier_mixer(u), x, mod, P['ev_gn_w'][j], P['ev_gn_b'][j],
                                      P['ev_w_out'][j], P['norm_ffn'][l], P['moe_router'][l], P['moe_router_b'][l])
        else:
            x1, h2, aff_t = odd_layer(x, mod, P['norm_mix'][l], P['norm_ffn'][l], P['od_w_in'][j], P['od_conv_w'][j],
                                      P['od_conv_b'][j], P['od_cln_g'][j], P['od_cln_b'][j], P['od_vln_g'][j],
                                      P['od_vln_b'][j], P['od_w_s'][j], P['od_b_s'][j], P['od_w_out'][j],
                                      P['moe_router'][l], P['moe_router_b'][l])
        x = expert_choice_ffn(x1, mod[:, 5:6], h2, aff_t, l, P['moe_w_gate'], P['moe_w_up'], P['moe_w_down'],
                              P['final_norm'], l == DEPTH - 1)
    return x, jnp.stack(states, axis=1)


def kernel(x_prompt, x_sample, state_wkv, c, c_ctx, mod_w, mod_b, norm_mix, norm_ffn, final_norm,
           ev_w_in, ev_w_out, ev_mu_rkv, ev_mu_wag, ev_w0, ev_w1, ev_w2, ev_a0, ev_a1, ev_a2,
           ev_g1, ev_g2, ev_k_k, ev_k_a, ev_r_k, ev_gn_w, ev_gn_b,
           od_w_in, od_w_out, od_conv_w, od_conv_b, od_cln_g, od_cln_b, od_vln_g, od_vln_b,
           od_w_s, od_b_s, moe_router, moe_router_b, moe_w_gate, moe_w_up, moe_w_down):
    P = dict(mod_w=mod_w, mod_b=mod_b, norm_mix=norm_mix, norm_ffn=norm_ffn, final_norm=final_norm,
             ev_w_in=ev_w_in, ev_w_out=ev_w_out, ev_mu_rkv=ev_mu_rkv, ev_mu_wag=ev_mu_wag,
             ev_w0=ev_w0, ev_w1=ev_w1, ev_w2=ev_w2, ev_a0=ev_a0, ev_a1=ev_a1, ev_a2=ev_a2,
             ev_g1=ev_g1, ev_g2=ev_g2, ev_k_k=ev_k_k, ev_k_a=ev_k_a, ev_r_k=ev_r_k,
             ev_gn_w=ev_gn_w, ev_gn_b=ev_gn_b,
             od_w_in=od_w_in, od_w_out=od_w_out, od_conv_w=od_conv_w, od_conv_b=od_conv_b,
             od_cln_g=od_cln_g, od_cln_b=od_cln_b, od_vln_g=od_vln_g, od_vln_b=od_vln_b,
             od_w_s=od_w_s, od_b_s=od_b_s, moe_router=moe_router, moe_router_b=moe_router_b,
             moe_w_gate=moe_w_gate, moe_w_up=moe_w_up, moe_w_down=moe_w_down)
    n_even = state_wkv.shape[1]
    s_zero = jnp.zeros((x_prompt.shape[0], n_even, 2, A_HEADS, A_HEAD_DIM, A_HEAD_DIM), F32)
    mods = modulation(jnp.concatenate([c_ctx[None, :], c], axis=0), mod_w, mod_b)
    no_pos = jnp.zeros(x_prompt.shape[1:], x_prompt.dtype)
    y_prompt, new_state_wkv = run_trunk(x_prompt, no_pos, mods[:, 0:1], s_zero, P)
    y_sample, _ = run_trunk(x_sample, grid_posemb(x_sample.shape[1], x_sample.dtype), mods[:, 1:], state_wkv, P)
    return (y_prompt, y_sample, new_state_wkv)
```

```python
import functools
import math

import jax
import jax.numpy as jnp
from jax import lax
from jax.experimental import pallas as pl
from jax.experimental.pallas import tpu as pltpu

D_MODEL = 1024
DEPTH = 2
GRID_W = 64
POS_BASE = 10000.0
A_HEADS = 12
A_HEAD_DIM = 64
D_A = A_HEADS * A_HEAD_DIM
B_GROUPS = 4
B_GROUP_DIM = 64
D_B = B_GROUPS * B_GROUP_DIM
D_C = 512
CONV_WIDTH = 31
D_HEADS = 4
D_HEAD_DIM = 128
D_D = D_HEADS * D_HEAD_DIM
CHUNK = 128
N_EXPERTS = 16
EC_FACTOR = 2
RMS_EPS = 1e-6
LN_EPS = 1e-5
GN_EPS = 64e-5

LANES = 128
SUBLANES = 8
MXU_DEPTH = 256
SCAN_CHUNK = 64
HEADS_PER_TILE = LANES // A_HEAD_DIM
N_HEAD_PAIRS = A_HEADS // HEADS_PER_TILE
PAIRS_PER_STEP = 6
SCAN_REQUESTS = 2

F32 = jnp.float32
BF16 = jnp.bfloat16


def _mm(a, b, dims):
    a_hi, b_hi = a.astype(BF16), b.astype(BF16)
    a_lo, b_lo = (a - a_hi.astype(F32)).astype(BF16), (b - b_hi.astype(F32)).astype(BF16)
    dot = lambda x, y: lax.dot_general(x, y, (dims, ((), ())), preferred_element_type=F32)
    return dot(a_hi, b_hi) + dot(a_hi, b_lo) + dot(a_lo, b_hi)


_NN = ((1,), (0,))
_NT = ((1,), (1,))
_TN = ((0,), (0,))


def _cumsum_rows(tri, x):
    x1 = x.astype(BF16)
    r1 = x - x1.astype(F32)
    x2 = r1.astype(BF16)
    x3 = (r1 - x2.astype(F32)).astype(BF16)
    dot = lambda y: lax.dot_general(tri, y, (_NN, ((), ())), preferred_element_type=F32)
    return dot(x1) + dot(x2) + dot(x3)


def _tile_block_diag(x, mask):
    n = x.shape[1] // x.shape[0]
    return jnp.where(mask, jnp.concatenate([x] * n, axis=0), 0.0)


def _unit_triangular_inverses(mats, same16, same32, bd_mask):
    c = mats[0].shape[0]
    bd = lambda x: _tile_block_diag(x, bd_mask)
    lane = lax.broadcasted_iota(jnp.int32, mats[0].shape, 1) & (c - 1)
    eye = (lax.broadcasted_iota(jnp.int32, mats[0].shape, 0) == lane).astype(F32)
    pws = [jnp.where(same16, -a, 0.0) for a in mats]
    ts = [eye + n for n in pws]
    for _ in range(3):
        pws = [_mm(pw, bd(pw), _NN) for pw in pws]
        ts = [t + _mm(t, bd(pw), _NN) for t, pw in zip(ts, pws)]
    in32 = jnp.logical_and(same32, jnp.logical_not(same16))
    tmp = [_mm(t, bd(jnp.where(in32, a, 0.0)), _NN) for t, a in zip(ts, mats)]
    ts = [t - _mm(x, bd(t), _NN) for t, x in zip(ts, tmp)]
    tmp = [_mm(t, bd(jnp.where(same32, 0.0, a)), _NN) for t, a in zip(ts, mats)]
    ts = [t - _mm(x, bd(t), _NN) for t, x in zip(ts, tmp)]
    return ts


INVERSE_GROUP = 2


def _wkv_chunks(items):
    c = items[0][0].shape[0]
    iota = lambda shape, axis: lax.broadcasted_iota(jnp.int32, shape, axis)
    t1, s1 = iota((c, c), 0), iota((c, c), 1)
    t2, s2 = iota((c, LANES), 0), iota((c, LANES), 1) & (c - 1)
    t4, s4 = iota((c, 2 * LANES), 0), iota((c, 2 * LANES), 1) & (c - 1)
    tri = {False: (t1 >= s1).astype(F32).astype(BF16), True: (t1 <= s1).astype(F32).astype(BF16)}
    strict2 = {False: t2 > s2, True: t2 < s2}
    incl4 = {False: t4 >= s4, True: t4 <= s4}
    wide = (c, INVERSE_GROUP * LANES)
    tw, sw = iota(wide, 0), iota(wide, 1) & (c - 1)
    same16 = (tw >> 4) == (sw >> 4)
    same32 = (tw >> 5) == (sw >> 5)
    head_shift = A_HEAD_DIM.bit_length() - 1
    bd_mask = lambda n: (iota((n, n), 0) >> head_shift) == (iota((n, n), 1) >> head_shift)
    bd2_mask, bdw_mask = bd_mask(LANES), bd_mask(wide[1])
    bd2 = lambda x: _tile_block_diag(x, bd2_mask)

    cs_all = [_cumsum_rows(tri[rev], lw) for (_, _, _, lw, _, _, _, rev) in items]
    prep = []
    for (r, v, kk, lw, k, a, s, rev), cs in zip(items, cs_all):
        cs_end = cs[0:1] if rev else cs[c - 1:c]
        b = kk * a
        g_inv = jnp.exp(-cs)
        g_tail = jnp.exp(cs_end - cs)
        kt = kk * jnp.exp(cs - lw)
        rt = r * jnp.exp(cs)
        lhs = jnp.concatenate([kt, rt], axis=0)
        kb_inv = jnp.concatenate([bd2(k * g_inv), bd2(b * g_inv)], axis=0)
        kb_tail = jnp.concatenate([k * g_tail, b * g_tail], axis=0)
        prep.append((lhs, kb_inv, kb_tail, jnp.exp(cs_end)))
    ps = [_mm(pr[0], it[6], _NT) for pr, it in zip(prep, items)]
    gs = [_mm(pr[0], pr[1], _NT) for pr in prep]
    a_kbs = [jnp.where(strict2[it[7]], g[0:c, LANES:2 * LANES], 0.0) for g, it in zip(gs, items)]
    groups = [jnp.concatenate(a_kbs[i:i + INVERSE_GROUP], axis=1) for i in range(0, len(items), INVERSE_GROUP)]
    t_groups = _unit_triangular_inverses(groups, same16, same32, bdw_mask)
    ts = [tg[:, j * LANES:(j + 1) * LANES] for tg in t_groups for j in range(INVERSE_GROUP)]
    ws = [p[0:c] + _mm(jnp.where(strict2[it[7]], g[0:c, 0:LANES], 0.0), bd2(it[1]), _NN)
          for p, g, it in zip(ps, gs, items)]
    us = [_mm(t, bd2(w), _NN) for t, w in zip(ts, ws)]
    ys = [p[c:2 * c] + _mm(jnp.where(incl4[it[7]], g[c:2 * c], 0.0),
                           jnp.concatenate([bd2(it[1]), bd2(-u)], axis=0), _NN)
          for p, g, it, u in zip(ps, gs, items, us)]
    dss = [_mm(jnp.concatenate([it[1], -u], axis=0), pr[2], _TN)
           for it, u, pr in zip(items, us, prep)]
    return [(y, it[6] * pr[3] + jnp.where(bd2_mask, ds, 0.0))
            for y, it, pr, ds in zip(ys, items, prep, dss)]


def _wkv_kernel(rf, vf, kkf, lwf, kf, af, rb, vb, kkb, lwb, kb, ab, s0_ref,
                yf_ref, yb_ref, sfin_ref, st_s):
    n = A_HEAD_DIM
    where = [(q, p) for q in range(SCAN_REQUESTS) for p in range(PAIRS_PER_STEP)]

    @pl.when(pl.program_id(2) == 0)
    def _():
        zero = jnp.zeros((n, n), F32)
        for q, p in where:
            for z in range(2):
                st_s[q, z, p] = jnp.concatenate(
                    [jnp.concatenate([s0_ref[q, z, HEADS_PER_TILE * p], zero], axis=1),
                     jnp.concatenate([zero, s0_ref[q, z, HEADS_PER_TILE * p + 1]], axis=1)], axis=0)

    items = []
    for q, p in where:
        ln = slice(p * LANES, (p + 1) * LANES)
        items.append((rf[q, :, ln], vf[q, :, ln].astype(F32), kkf[q, :, ln], lwf[0, q, :, ln], kf[0, q, :, ln],
                      af[0, q, :, ln], st_s[q, 0, p], False))
        items.append((rb[q, :, ln], vb[q, :, ln].astype(F32), kkb[q, :, ln], lwb[0, q, :, ln], kb[0, q, :, ln],
                      ab[0, q, :, ln], st_s[q, 1, p], True))
    out = _wkv_chunks(items)
    for i, (q, p) in enumerate(where):
        ln = slice(p * LANES, (p + 1) * LANES)
        yf_ref[q, :, ln], st_s[q, 0, p] = out[2 * i]
        yb_ref[q, :, ln], st_s[q, 1, p] = out[2 * i + 1]

    @pl.when(pl.program_id(2) == pl.num_programs(2) - 1)
    def _():
        for q, p in where:
            for z in range(2):
                s = st_s[q, z, p]
                sfin_ref[q, z, HEADS_PER_TILE * p] = s[0:n, 0:n]
                sfin_ref[q, z, HEADS_PER_TILE * p + 1] = s[n:2 * n, n:2 * n]


def wkv7_scan_pallas(r, v, kk, lw, k, a, s0):
    bsz, seq, _ = r.shape
    c = SCAN_CHUNK
    nc = seq // c
    n = A_HEAD_DIM
    wd = PAIRS_PER_STEP * LANES
    nq = SCAN_REQUESTS
    heads = HEADS_PER_TILE * PAIRS_PER_STEP
    shared_f = pl.BlockSpec((nq, c, wd), lambda b, p, i: (b, i, p))
    shared_b = pl.BlockSpec((nq, c, wd), lambda b, p, i: (b, nc - 1 - i, p))
    dir_f = pl.BlockSpec((1, nq, c, wd), lambda b, p, i: (0, b, i, p))
    dir_b = pl.BlockSpec((1, nq, c, wd), lambda b, p, i: (1, b, nc - 1 - i, p))
    st = pl.BlockSpec((nq, 2, heads, n, n), lambda b, p, i: (b, 0, p, 0, 0))
    return pl.pallas_call(
        _wkv_kernel,
        grid=(bsz // SCAN_REQUESTS, N_HEAD_PAIRS // PAIRS_PER_STEP, nc),
        in_specs=[shared_f, shared_f, shared_f, dir_f, dir_f, dir_f,
                  shared_b, shared_b, shared_b, dir_b, dir_b, dir_b, st],
        out_specs=[shared_f, shared_b, st],
        out_shape=[jax.ShapeDtypeStruct((bsz, seq, D_A), F32),
                   jax.ShapeDtypeStruct((bsz, seq, D_A), F32),
                   jax.ShapeDtypeStruct((bsz, 2, A_HEADS, n, n), F32)],
        scratch_shapes=[pltpu.VMEM((nq, 2, PAIRS_PER_STEP, LANES, LANES), F32)],
        compiler_params=pltpu.CompilerParams(
            dimension_semantics=("arbitrary", "arbitrary", "arbitrary")),
        name="wkv7_scan",
    )(r, v, kk, lw, k, a, r, v, kk, lw, k, a, s0)


def grid_posemb(n_tokens, dtype):
    rows = n_tokens // GRID_W
    row = jnp.repeat(jnp.arange(rows, dtype=F32), GRID_W)
    col = jnp.tile(jnp.arange(GRID_W, dtype=F32), rows)
    quarter = D_MODEL // 4
    freq = 1.0 / (POS_BASE ** (jnp.arange(quarter, dtype=F32) / quarter))
    ar = row[:, None] * freq[None, :]
    ac = col[:, None] * freq[None, :]
    return jnp.concatenate([jnp.sin(ar), jnp.cos(ar), jnp.sin(ac), jnp.cos(ac)], axis=-1).astype(dtype)


def _exclusive_prefix(x):
    rows, seq = x.shape
    tri = jnp.where(lax.broadcasted_iota(jnp.int32, (LANES, LANES), 0)
                    <= lax.broadcasted_iota(jnp.int32, (LANES, LANES), 1), 1.0, 0.0).astype(BF16)
    carry = jnp.zeros((rows, 1), F32)
    out = []
    for blk in range(seq // LANES):
        xb = x[:, blk * LANES:(blk + 1) * LANES]
        inc = jnp.dot(xb.astype(BF16), tri, preferred_element_type=F32)
        out.append(inc - xb + carry)
        carry = carry + inc[:, LANES - 1:LANES]
    return jnp.concatenate(out, axis=1)


SELECT_STEPS = 80


def _select_kernel(aff_ref, slot_ref, *, cap):
    bsz, n_exp, seq = aff_ref.shape
    aff = aff_ref[...].reshape(bsz * n_exp, seq)
    count = lambda t: jnp.sum(jnp.where(aff >= t, 1.0, 0.0), axis=1, keepdims=True)
    lo = jnp.min(aff, axis=1, keepdims=True)
    hi = 2.0 * jnp.max(aff, axis=1, keepdims=True) + 1e-30

    def halve(_, bracket):
        lo, hi = bracket
        mid = lo + 0.5 * (hi - lo)
        ok = count(mid) >= cap
        return jnp.where(ok, mid, lo), jnp.where(ok, hi, mid)

    lo, hi = lax.fori_loop(0, SELECT_STEPS, halve, (lo, hi))
    above = jnp.where(aff >= hi, 1.0, 0.0)
    tied = jnp.where(aff >= lo, 1.0, 0.0) - above
    need = cap - jnp.sum(above, axis=1, keepdims=True)
    keep = above + tied * jnp.where(_exclusive_prefix(tied) < need, 1.0, 0.0)
    slot = _exclusive_prefix(keep)
    slot_ref[...] = jnp.where(keep > 0.5, slot, -1.0).astype(jnp.int32).reshape(bsz, n_exp, seq)


def _expert_kernel(slot_ref, aff_ref, h_ref, wg_ref, wu_ref, wd_ref, y_ref, wg_s, wu_s, wd_s, *, cap):
    @pl.when(pl.program_id(1) == 0)
    def _():
        wg_s[...] = wg_ref[0, 0].astype(BF16)
        wu_s[...] = wu_ref[0, 0].astype(BF16)
        wd_s[...] = wd_ref[0, 0].astype(BF16)

    group, seq, _ = h_ref.shape
    slot_iota = lax.broadcasted_iota(jnp.int32, (cap, seq), 0)
    expert = pl.ds(pl.program_id(0), 1)
    xs, vals = [], []
    for rb in range(group):
        hit = slot_iota == slot_ref[rb, expert, :]
        xs.append(jnp.dot(jnp.where(hit, 1.0, 0.0).astype(BF16), h_ref[rb], preferred_element_type=F32))
        vals.append(jnp.sum(jnp.where(hit, aff_ref[rb, expert, :], 0.0), axis=1, keepdims=True))
    xs = jnp.concatenate(xs, axis=0).astype(BF16)
    g = jnp.dot(xs, wg_s[...], preferred_element_type=F32)
    u = jnp.dot(xs, wu_s[...], preferred_element_type=F32)
    hid = (g * jax.nn.sigmoid(g) * u).astype(BF16)
    y = jnp.dot(hid, wd_s[...], preferred_element_type=F32) * jnp.concatenate(vals, axis=0)
    for rb in range(group):
        y_ref[rb, 0] = y[rb * cap:(rb + 1) * cap].astype(y_ref.dtype)


def _combine_kernel(slot_ref, ys_ref, x_ref, gate_ref, fg_ref, o_ref, *, cap, final):
    rows = x_ref.shape[1]
    n_exp = slot_ref.shape[2]
    if cap >= MXU_DEPTH:
        slot_iota = lax.broadcasted_iota(jnp.int32, (rows, cap), 1)
        moe = jnp.zeros(x_ref.shape[1:], F32)
        for e in range(n_exp):
            onehot = jnp.where(slot_iota == slot_ref[0, :, e:e + 1], 1.0, 0.0).astype(BF16)
            moe = moe + jnp.dot(onehot, ys_ref[0, e * cap:(e + 1) * cap], preferred_element_type=F32)
    else:
        lane = lax.broadcasted_iota(jnp.int32, (n_exp, n_exp * cap), 1)
        first = lax.broadcasted_iota(jnp.int32, (n_exp, n_exp * cap), 0) * cap
        spread = jnp.where(jnp.logical_and(lane >= first, lane < first + cap), 1.0, 0.0).astype(BF16)
        slot_wide = jnp.dot(slot_ref[0].astype(F32).astype(BF16), spread, preferred_element_type=F32)
        within = (lane - first)[0:1].astype(F32)
        for e in range(1, n_exp):
            within = jnp.where(lane[0:1] >= e * cap, (lane - first)[e:e + 1].astype(F32), within)
        onehot = jnp.where(slot_wide == within, 1.0, 0.0).astype(BF16)
        moe = jnp.dot(onehot, ys_ref[0], preferred_element_type=F32)
    out = x_ref[0] + gate_ref[0] * moe
    o_ref[0] = _rms(out, fg_ref[...]) if final else out


MOE_VMEM_BYTES = 56 * 1024 * 1024
COMBINE_ROWS = 512
EXPERT_ROWS = 512


def expert_choice_ffn(x, gate, h, aff_t, layer, w_gate, w_up, w_down, final_g, final):
    bsz, seq, d = h.shape
    _, n_exp, _, f = w_gate.shape
    cap = EC_FACTOR * seq // n_exp
    slot = pl.pallas_call(
        functools.partial(_select_kernel, cap=float(cap)),
        out_shape=jax.ShapeDtypeStruct((bsz, n_exp, seq), jnp.int32),
        compiler_params=pltpu.CompilerParams(vmem_limit_bytes=MOE_VMEM_BYTES),
        name="moe_select",
    )(aff_t)
    group = min(bsz, max(1, EXPERT_ROWS // cap))
    per_request = pl.BlockSpec((group, n_exp, seq), lambda e, b: (b, 0, 0))
    weight = lambda k, n: pl.BlockSpec((1, 1, k, n), lambda e, b: (layer, e, 0, 0))
    ys = pl.pallas_call(
        functools.partial(_expert_kernel, cap=cap),
        grid=(n_exp, bsz // group),
        in_specs=[per_request, per_request,
                  pl.BlockSpec((group, seq, d), lambda e, b: (b, 0, 0)),
                  weight(d, f), weight(d, f), weight(f, d)],
        out_specs=pl.BlockSpec((group, 1, cap, d), lambda e, b: (b, e, 0, 0)),
        out_shape=jax.ShapeDtypeStruct((bsz, n_exp, cap, d), BF16),
        scratch_shapes=[pltpu.VMEM((d, f), BF16), pltpu.VMEM((d, f), BF16), pltpu.VMEM((f, d), BF16)],
        compiler_params=pltpu.CompilerParams(dimension_semantics=("arbitrary", "arbitrary"),
                                             vmem_limit_bytes=MOE_VMEM_BYTES),
        name="moe_experts",
    )(slot, aff_t, h, w_gate, w_up, w_down)
    rows = min(COMBINE_ROWS, seq)
    return pl.pallas_call(
        functools.partial(_combine_kernel, cap=cap, final=final),
        grid=(bsz, seq // rows),
        in_specs=[pl.BlockSpec((1, rows, n_exp), lambda b, i: (b, i, 0)),
                  pl.BlockSpec((1, n_exp * cap, d), lambda b, i: (b, 0, 0)),
                  pl.BlockSpec((1, rows, d), lambda b, i: (b, i, 0)),
                  pl.BlockSpec((1, 1, d), lambda b, i: (b, 0, 0)),
                  pl.BlockSpec((1, d), lambda b, i: (0, 0))],
        out_specs=pl.BlockSpec((1, rows, d), lambda b, i: (b, i, 0)),
        out_shape=jax.ShapeDtypeStruct((bsz, seq, d), F32),
        compiler_params=pltpu.CompilerParams(dimension_semantics=("arbitrary", "arbitrary"),
                                             vmem_limit_bytes=MOE_VMEM_BYTES),
        name="moe_combine",
    )(jnp.swapaxes(slot, 1, 2), ys.reshape(bsz, n_exp * cap, d), x, gate, _row(final_g))


def _mod_kernel(c_ref, w_ref, b_ref, o_ref):
    c = c_ref[...]
    w_hi, w_lo = _hi_lo(w_ref[0])
    o_ref[0] = _split_dot(c * jax.nn.sigmoid(c), w_hi, w_lo) + b_ref[0]


def modulation(cvecs, w_mod, b_mod):
    n, d = cvecs.shape
    depth = w_mod.shape[0]
    rows = -(-n // 8) * 8
    cpad = jnp.pad(cvecs, ((0, rows - n), (0, 0)))
    out = pl.pallas_call(
        _mod_kernel,
        grid=(depth, 6),
        in_specs=[pl.BlockSpec((rows, d), lambda l, n_: (0, 0)),
                  pl.BlockSpec((1, d, d), lambda l, n_: (l, 0, n_)),
                  pl.BlockSpec((1, 1, d), lambda l, n_: (l, 0, n_))],
        out_specs=pl.BlockSpec((1, rows, d), lambda l, n_: (l, 0, n_)),
        out_shape=jax.ShapeDtypeStruct((depth, rows, 6 * d), F32),
        compiler_params=pltpu.CompilerParams(dimension_semantics=("arbitrary", "arbitrary")),
        name="modulation",
    )(cpad, w_mod, b_mod.reshape(depth, 1, 6 * d))
    return out[:, :n].reshape(depth, n, 6, d)


TOKEN_TILE = 512
HALO = 16
LAYER_VMEM_BYTES = 48 * 1024 * 1024


def _rms(x, g):
    return x * lax.rsqrt(jnp.mean(x * x, axis=-1, keepdims=True) + RMS_EPS) * g


def _ln(x, g, b):
    mu = jnp.mean(x, axis=-1, keepdims=True)
    xc = x - mu
    return xc * lax.rsqrt(jnp.mean(xc * xc, axis=-1, keepdims=True) + LN_EPS) * g + b


def _split_dot(x, w_hi, w_lo):
    x_hi = x.astype(BF16)
    x_lo = (x - x_hi.astype(F32)).astype(BF16)
    dot = lambda a, b: jnp.dot(a, b, preferred_element_type=F32)
    return dot(x_hi, w_hi) + dot(x_lo, w_hi) + dot(x_hi, w_lo)


def _with_halo(x_ref, prev_ref, next_ref):
    return jnp.concatenate([prev_ref[0], x_ref[0], next_ref[0]], axis=0)


def _inside_mask(rows):
    i, n = pl.program_id(1), pl.num_programs(1)
    r = lax.broadcasted_iota(jnp.int32, (rows + 2 * HALO, 1), 0)
    return jnp.logical_and(jnp.logical_or(i > 0, r >= HALO), jnp.logical_or(i < n - 1, r < rows + HALO))


def _router(x1, mod_ref, nf_ref, wr_hi_ref, wr_lo_ref, br_ref, h2_ref, aff_ref):
    h2 = _rms(x1, nf_ref[...]) * (1.0 + mod_ref[0, 4:5]) + mod_ref[0, 3:4]
    h2_ref[0] = h2.astype(h2_ref.dtype)
    h_hi = h2.astype(BF16)
    h_lo = (h2 - h_hi.astype(F32)).astype(BF16)
    dot = lambda w, a: lax.dot_general(w, a, (_NT, ((), ())), preferred_element_type=F32)
    logits = dot(wr_hi_ref[...], h_hi) + dot(wr_hi_ref[...], h_lo) + dot(wr_lo_ref[...], h_hi) + br_ref[...]
    e = jnp.exp(logits - jnp.max(logits, axis=0, keepdims=True))
    aff_ref[0] = e / jnp.sum(e, axis=0, keepdims=True)


def _odd_kernel(x_ref, xp_ref, xn_ref, mod_ref, nm_ref, nf_ref, win_ref, cw_ref, cb_ref, cg_ref, cbb_ref,
                vg_ref, vb_ref, ws_ref, bs_ref, wout_ref, wr_hi_ref, wr_lo_ref, br_ref,
                x1_ref, h2_ref, aff_ref, glu_s):
    rows = x_ref.shape[1]
    inside = _inside_mask(rows)
    xa = _with_halo(x_ref, xp_ref, xn_ref)
    h = _rms(xa, nm_ref[...]) * (1.0 + mod_ref[0, 1:2]) + mod_ref[0, 0:1]
    h = jnp.where(inside, h, 0.0).astype(BF16)
    pc = jnp.dot(h, win_ref[:, 0:2 * D_C], preferred_element_type=F32)
    glu = pc[:, 0:D_C] * jax.nn.sigmoid(pc[:, D_C:2 * D_C])
    span = glu_s.shape[1]
    for r in range(SUBLANES):
        glu_s[r] = glu[r:r + span]
    acc = jnp.zeros((rows, D_C), F32)
    for j in range(CONV_WIDTH):
        off = HALO - CONV_WIDTH // 2 + j
        base = off // SUBLANES * SUBLANES
        acc = acc + cw_ref[j:j + 1, :] * glu_s[off % SUBLANES, base:base + rows, :]
    o_c = _ln(acc + cb_ref[...], cg_ref[...], cbb_ref[...])
    o_c = o_c * jax.nn.sigmoid(o_c)
    pd = jax.nn.gelu(jnp.dot(h[HALO:HALO + rows], win_ref[:, 2 * D_C:], preferred_element_type=F32))
    u = pd[:, 0:D_D]
    v = _ln(pd[:, D_D:], vg_ref[...], vb_ref[...]).astype(BF16)
    chunks = []
    for ck in range(rows // CHUNK):
        vc = v[ck * CHUNK:(ck + 1) * CHUNK]
        chunks.append(jnp.concatenate(
            [jnp.dot(ws_ref[hd], vc[:, hd * D_HEAD_DIM:(hd + 1) * D_HEAD_DIM], preferred_element_type=F32)
             for hd in range(D_HEADS)], axis=1) + bs_ref[...])
    o_d = u * jnp.concatenate(chunks, axis=0)
    mixed = jnp.dot(jnp.concatenate([o_c, o_d], axis=1).astype(BF16), wout_ref[...], preferred_element_type=F32)
    x1 = x_ref[0] + mod_ref[0, 2:3] * mixed
    x1_ref[0] = x1
    _router(x1, mod_ref, nf_ref, wr_hi_ref, wr_lo_ref, br_ref, h2_ref, aff_ref)


def _row(v):
    return v.reshape(1, -1)


def _hi_lo(w):
    hi = w.astype(BF16)
    return hi, (w - hi.astype(F32)).astype(BF16)


def _tile_specs(rows, seq, d):
    per = rows // HALO
    last = seq // HALO - 1
    return [pl.BlockSpec((1, rows, d), lambda b, i: (b, i, 0)),
            pl.BlockSpec((1, HALO, d), lambda b, i: (b, jnp.maximum(i * per - 1, 0), 0)),
            pl.BlockSpec((1, HALO, d), lambda b, i: (b, jnp.minimum((i + 1) * per, last), 0))]


def _shared_tile_specs(rows, seq, d):
    per = rows // HALO
    last = seq // HALO - 1
    return [pl.BlockSpec((rows, d), lambda b, i: (i, 0)),
            pl.BlockSpec((HALO, d), lambda b, i: (jnp.maximum(i * per - 1, 0), 0)),
            pl.BlockSpec((HALO, d), lambda b, i: (jnp.minimum((i + 1) * per, last), 0))]


def _full(a):
    return pl.BlockSpec(a.shape, lambda b, i: (0,) * a.ndim)


def odd_layer(x, mod, norm_mix, norm_ffn, w_in, conv_w, conv_b, cln_g, cln_b, vln_g, vln_b, w_s, b_s, w_out,
              w_router, b_router):
    bsz, seq, d = x.shape
    rows = min(TOKEN_TILE, seq)
    n_exp = w_router.shape[1]
    wr_hi, wr_lo = _hi_lo(w_router.T)
    bs_full = jnp.repeat(b_s.T, D_HEAD_DIM, axis=1)
    consts = [_row(norm_mix), _row(norm_ffn), w_in.astype(BF16), conv_w, _row(conv_b), _row(cln_g), _row(cln_b),
              _row(vln_g), _row(vln_b), w_s.astype(BF16), bs_full, w_out.astype(BF16), wr_hi, wr_lo,
              b_router.reshape(-1, 1)]
    tile = lambda w: pl.BlockSpec((1, rows, w), lambda b, i: (b, i, 0))
    return pl.pallas_call(
        _odd_kernel,
        grid=(bsz, seq // rows),
        in_specs=_tile_specs(rows, seq, d) + [pl.BlockSpec((1, 6, d), lambda b, i: (b, 0, 0))]
        + [_full(a) for a in consts],
        out_specs=[tile(d), tile(d), pl.BlockSpec((1, n_exp, rows), lambda b, i: (b, 0, i))],
        out_shape=[jax.ShapeDtypeStruct((bsz, seq, d), F32), jax.ShapeDtypeStruct((bsz, seq, d), BF16),
                   jax.ShapeDtypeStruct((bsz, n_exp, seq), F32)],
        scratch_shapes=[pltpu.VMEM((SUBLANES, rows + 2 * HALO - SUBLANES, D_C), F32)],
        compiler_params=pltpu.CompilerParams(dimension_semantics=("arbitrary", "arbitrary"),
                                             vmem_limit_bytes=LAYER_VMEM_BYTES),
        name="odd_layer",
    )(x, x, x, mod, *consts)


EVEN_TILE = 256
DECAY_SCALE = math.exp(-0.5)


def _head_sums(x, ones_bd):
    hi = x.astype(BF16)
    lo = (x - hi.astype(F32)).astype(BF16)
    dot = lambda a: jnp.dot(a, ones_bd, preferred_element_type=F32)
    return jnp.concatenate([dot(hi[:, g * LANES:(g + 1) * LANES]) + dot(lo[:, g * LANES:(g + 1) * LANES])
                            for g in range(x.shape[1] // LANES)], axis=1)


def _head_ones():
    head_shift = A_HEAD_DIM.bit_length() - 1
    same = ((lax.broadcasted_iota(jnp.int32, (LANES, LANES), 0) >> head_shift)
            == (lax.broadcasted_iota(jnp.int32, (LANES, LANES), 1) >> head_shift))
    return jnp.where(same, 1.0, 0.0).astype(BF16)


def _even_pre_kernel(x_ref, xp_ref, xn_ref, pos_ref, posp_ref, posn_ref, mod_ref, nm_ref, win_ref, mu_rkv_ref,
                     down_h_ref, down_dh_ref, w2_ref, w0_ref, a2_ref, a0_ref, g2_ref, kk_ref, ka_ref,
                     rk_ref, x_out, r_out, v_out, kk_out, lw_out, k_out, a_out, gate_out, bonus_out, u_out,
                     proj_s, h_s):
    rows = x_ref.shape[1]
    inside = _inside_mask(rows)
    xa = _with_halo(x_ref, xp_ref, xn_ref) + jnp.concatenate([posp_ref[...], pos_ref[...], posn_ref[...]], axis=0)
    x_out[0] = xa[HALO:HALO + rows]
    h = _rms(xa, nm_ref[...]) * (1.0 + mod_ref[0, 1:2]) + mod_ref[0, 0:1]
    h = jnp.where(inside, h, 0.0)
    h_s[...] = h
    proj = jnp.dot(h.astype(BF16), win_ref[...], preferred_element_type=F32)
    proj_s[...] = proj[:, 0:3 * D_A]
    u_out[0] = proj[HALO:HALO + rows, 3 * D_A:].astype(u_out.dtype)

    def shifted(ref, lo, hi):
        cur = ref[HALO:HALO + rows, lo:hi]
        return cur, 0.5 * (ref[HALO - 1:HALO - 1 + rows, lo:hi] + ref[HALO + 1:HALO + 1 + rows, lo:hi]) - cur

    r, dr = shifted(proj_s, 0, D_A)
    k, dk = shifted(proj_s, D_A, 2 * D_A)
    v, dv = shifted(proj_s, 2 * D_A, 3 * D_A)
    r = r + dr * mu_rkv_ref[0:1]
    k = k + dk * mu_rkv_ref[1:2]
    v = v + dv * mu_rkv_ref[2:3]
    hc, dh = shifted(h_s, 0, D_MODEL)
    dot = lambda a, b: jnp.dot(a, b, preferred_element_type=F32)
    low = dot(hc.astype(BF16), down_h_ref[...]) + dot(dh.astype(BF16), down_dh_ref[...])
    n_w, n_a = w2_ref.shape[0], a2_ref.shape[0]
    w_pre = w0_ref[...] + dot(jnp.tanh(low[:, 0:n_w]).astype(BF16), w2_ref[...])
    icl = jax.nn.sigmoid(a0_ref[...] + dot(low[:, n_w:n_w + n_a].astype(BF16), a2_ref[...]))
    gate_out[0] = dot(jax.nn.sigmoid(low[:, n_w + n_a:]).astype(BF16), g2_ref[...])
    ones_bd = _head_ones()
    kk = k * kk_ref[...]
    kk = kk / jnp.maximum(jnp.sqrt(_head_sums(kk * kk, ones_bd)), 1e-12)
    r_out[0] = r
    v_out[0] = v.astype(v_out.dtype)
    kk_out[0] = kk
    bonus_out[0] = _head_sums(r * k * rk_ref[...], ones_bd) * v
    for z in range(2):
        a_z = icl[:, z * D_A:(z + 1) * D_A]
        lw_out[z, 0] = -DECAY_SCALE * jax.nn.sigmoid(w_pre[:, z * D_A:(z + 1) * D_A])
        a_out[z, 0] = a_z
        k_out[z, 0] = k * (1.0 + (a_z - 1.0) * ka_ref[...])


def _block_diag2(w):
    z = jnp.zeros_like(w[0])
    return jnp.concatenate([jnp.concatenate([w[0], z], axis=1), jnp.concatenate([z, w[1]], axis=1)], axis=0)


def even_pre(x, pos, mod, norm_mix, w_in, mu_rkv, mu_wag, w0, w1, w2, a0, a1, a2, g1, g2, k_k, k_a, r_k):
    bsz, seq, d = x.shape
    rows = min(EVEN_TILE, seq)
    cat = lambda w: jnp.concatenate([w[0], w[1]], axis=1)
    downs = [cat(w1), cat(a1), g1]
    down_h = jnp.concatenate(downs, axis=1)
    down_dh = jnp.concatenate([mu_wag[i][:, None] * w for i, w in enumerate(downs)], axis=1)
    consts = [_row(norm_mix), w_in.astype(BF16), mu_rkv, down_h.astype(BF16), down_dh.astype(BF16),
              _block_diag2(w2).astype(BF16), _row(w0), _block_diag2(a2).astype(BF16), _row(a0),
              g2.astype(BF16), _row(k_k), _row(k_a), _row(r_k)]
    tile = lambda w: pl.BlockSpec((1, rows, w), lambda b, i: (b, i, 0))
    tile2 = pl.BlockSpec((2, 1, rows, D_A), lambda b, i: (0, b, i, 0))
    sds = lambda w, dt=F32: jax.ShapeDtypeStruct((bsz, seq, w), dt)
    sds2 = jax.ShapeDtypeStruct((2, bsz, seq, D_A), F32)
    return pl.pallas_call(
        _even_pre_kernel,
        grid=(bsz, seq // rows),
        in_specs=_tile_specs(rows, seq, d) + _shared_tile_specs(rows, seq, d)
        + [pl.BlockSpec((1, 6, d), lambda b, i: (b, 0, 0))] + [_full(a) for a in consts],
        out_specs=[tile(d), tile(D_A), tile(D_A), tile(D_A), tile2, tile2, tile2, tile(D_A), tile(D_A), tile(D_B)],
        out_shape=[sds(d), sds(D_A), sds(D_A, BF16), sds(D_A), sds2, sds2, sds2, sds(D_A), sds(D_A), sds(D_B, BF16)],
        scratch_shapes=[pltpu.VMEM((rows + 2 * HALO, 3 * D_A), F32), pltpu.VMEM((rows + 2 * HALO, d), F32)],
        compiler_params=pltpu.CompilerParams(dimension_semantics=("arbitrary", "arbitrary"),
                                             vmem_limit_bytes=LAYER_VMEM_BYTES),
        name="even_pre",
    )(x, x, x, pos, pos, pos, mod, *consts)


FOURIER_ROWS = 512
DFT_SPLIT = 64


def _fourier_kernel(u_ref, f64_ref, fl_ref, o_ref, ucs_s):
    seq = u_ref.shape[1]

    @pl.when(pl.program_id(1) == 0)
    def _():
        ucs = jnp.dot(u_ref[0], f64_ref[...], preferred_element_type=F32)
        ucs_s[0:seq] = ucs[:, 0:D_B].astype(BF16)
        ucs_s[seq:2 * seq] = ucs[:, D_B:].astype(BF16)

    scale = 1.0 / math.sqrt(seq * B_GROUP_DIM)
    o_ref[0] = (jnp.dot(fl_ref[...], ucs_s[...], preferred_element_type=F32) * scale).astype(o_ref.dtype)


def _dft_tables(seq):
    def cs(n):
        i = jnp.arange(n, dtype=jnp.int32)
        ang = ((i[:, None] * i[None, :]) % n).astype(F32) * (2.0 * math.pi / n)
        return jnp.cos(ang), jnp.sin(ang)
    c64, s64 = cs(B_GROUP_DIM)
    eye = jnp.eye(B_GROUPS, dtype=F32)
    f64 = jnp.concatenate([jnp.kron(eye, c64), jnp.kron(eye, s64)], axis=1)
    split = min(DFT_SPLIT, seq)
    s = jnp.arange(seq, dtype=jnp.int32)[None, :]
    ang = lambda t: ((t[:, None] * s) % seq).astype(F32) * (2.0 * math.pi / seq)
    ang_a = ang(jnp.arange(seq // split, dtype=jnp.int32) * split)
    ang_b = ang(jnp.arange(split, dtype=jnp.int32))
    ca, sa = jnp.cos(ang_a)[:, None, :], jnp.sin(ang_a)[:, None, :]
    cb, sb = jnp.cos(ang_b)[None, :, :], jnp.sin(ang_b)[None, :, :]
    cl = (ca * cb - sa * sb).reshape(seq, seq)
    sl = (sa * cb + ca * sb).reshape(seq, seq)
    return f64.astype(BF16), jnp.concatenate([cl, -sl], axis=1).astype(BF16)


def fourier_mixer(u):
    bsz, seq, _ = u.shape
    rows = min(FOURIER_ROWS, seq)
    f64, fl = _dft_tables(seq)
    return pl.pallas_call(
        _fourier_kernel,
        grid=(bsz, seq // rows),
        in_specs=[pl.BlockSpec((1, seq, D_B), lambda b, i: (b, 0, 0)),
                  pl.BlockSpec(f64.shape, lambda b, i: (0, 0)),
                  pl.BlockSpec((rows, 2 * seq), lambda b, i: (i, 0))],
        out_specs=pl.BlockSpec((1, rows, D_B), lambda b, i: (b, i, 0)),
        out_shape=jax.ShapeDtypeStruct((bsz, seq, D_B), BF16),
        scratch_shapes=[pltpu.VMEM((2 * seq, D_B), BF16)],
        compiler_params=pltpu.CompilerParams(dimension_semantics=("arbitrary", "arbitrary"),
                                             vmem_limit_bytes=LAYER_VMEM_BYTES),
        name="fourier_mixer",
    )(u, f64, fl)


def _even_post_kernel(yf_ref, yb_ref, bonus_ref, gate_ref, ob_ref, x_ref, mod_ref, gnw_ref, gnb_ref, wout_ref,
                      nf_ref, wr_hi_ref, wr_lo_ref, br_ref, x1_ref, h2_ref, aff_ref):
    ones_bd = _head_ones()
    y = yf_ref[0] + yb_ref[0]
    mu = _head_sums(y, ones_bd) * (1.0 / A_HEAD_DIM)
    yc = y - mu
    var = _head_sums(yc * yc, ones_bd) * (1.0 / A_HEAD_DIM)
    o_a = (yc * lax.rsqrt(var + GN_EPS) * gnw_ref[...] + gnb_ref[...] + bonus_ref[0]) * gate_ref[0]
    mixed = (jnp.dot(o_a.astype(BF16), wout_ref[0:D_A], preferred_element_type=F32)
             + jnp.dot(ob_ref[0], wout_ref[D_A:], preferred_element_type=F32))
    x1 = x_ref[0] + mod_ref[0, 2:3] * mixed
    x1_ref[0] = x1
    _router(x1, mod_ref, nf_ref, wr_hi_ref, wr_lo_ref, br_ref, h2_ref, aff_ref)


def even_post(yf, yb, bonus, gate, o_b, x, mod, gn_w, gn_b, w_out, norm_ffn, w_router, b_router):
    bsz, seq, d = x.shape
    rows = min(TOKEN_TILE, seq)
    n_exp = w_router.shape[1]
    wr_hi, wr_lo = _hi_lo(w_router.T)
    consts = [_row(gn_w), _row(gn_b), w_out.astype(BF16), _row(norm_ffn), wr_hi, wr_lo, b_router.reshape(-1, 1)]
    tile = lambda w: pl.BlockSpec((1, rows, w), lambda b, i: (b, i, 0))
    return pl.pallas_call(
        _even_post_kernel,
        grid=(bsz, seq // rows),
        in_specs=[tile(D_A), tile(D_A), tile(D_A), tile(D_A), tile(D_B), tile(d),
                  pl.BlockSpec((1, 6, d), lambda b, i: (b, 0, 0))] + [_full(a) for a in consts],
        out_specs=[tile(d), tile(d), pl.BlockSpec((1, n_exp, rows), lambda b, i: (b, 0, i))],
        out_shape=[jax.ShapeDtypeStruct((bsz, seq, d), F32), jax.ShapeDtypeStruct((bsz, seq, d), BF16),
                   jax.ShapeDtypeStruct((bsz, n_exp, seq), F32)],
        compiler_params=pltpu.CompilerParams(dimension_semantics=("arbitrary", "arbitrary"),
                                             vmem_limit_bytes=LAYER_VMEM_BYTES),
        name="even_post",
    )(yf, yb, bonus, gate, o_b, x, mod, *consts)


def run_trunk(x, pos, mods, s_init, P):
    states = []
    for l in range(DEPTH):
        j = l // 2
        mod = jnp.broadcast_to(mods[l], (x.shape[0], 6, x.shape[2]))
        if l % 2 == 0:
            x, r, v, kk, lw, kd, a, gate, bonus, u = even_pre(
                x, pos if l == 0 else jnp.zeros_like(pos), mod, P['norm_mix'][l], P['ev_w_in'][j], P['ev_mu_rkv'][j], P['ev_mu_wag'][j],
                P['ev_w0'][j], P['ev_w1'][j], P['ev_w2'][j], P['ev_a0'][j], P['ev_a1'][j], P['ev_a2'][j],
                P['ev_g1'][j], P['ev_g2'][j], P['ev_k_k'][j], P['ev_k_a'][j], P['ev_r_k'][j])
            yf, yb, s_fin = wkv7_scan_pallas(r, v, kk, lw, kd, a, s_init[:, j])
            states.append(s_fin)
            x1, h2, aff_t = even_post(yf, yb, bonus, gate, fourier_mixer(u), x, mod, P['ev_gn_w'][j], P['ev_gn_b'][j],
                                      P['ev_w_out'][j], P['norm_ffn'][l], P['moe_router'][l], P['moe_router_b'][l])
        else:
            x1, h2, aff_t = odd_layer(x, mod, P['norm_mix'][l], P['norm_ffn'][l], P['od_w_in'][j], P['od_conv_w'][j],
                                      P['od_conv_b'][j], P['od_cln_g'][j], P['od_cln_b'][j], P['od_vln_g'][j],
                                      P['od_vln_b'][j], P['od_w_s'][j], P['od_b_s'][j], P['od_w_out'][j],
                                      P['moe_router'][l], P['moe_router_b'][l])
        x = expert_choice_ffn(x1, mod[:, 5:6], h2, aff_t, l, P['moe_w_gate'], P['moe_w_up'], P['moe_w_down'],
                              P['final_norm'], l == DEPTH - 1)
    return x, jnp.stack(states, axis=1)


def kernel(x_prompt, x_sample, state_wkv, c, c_ctx, mod_w, mod_b, norm_mix, norm_ffn, final_norm,
           ev_w_in, ev_w_out, ev_mu_rkv, ev_mu_wag, ev_w0, ev_w1, ev_w2, ev_a0, ev_a1, ev_a2,
           ev_g1, ev_g2, ev_k_k, ev_k_a, ev_r_k, ev_gn_w, ev_gn_b,
           od_w_in, od_w_out, od_conv_w, od_conv_b, od_cln_g, od_cln_b, od_vln_g, od_vln_b,
           od_w_s, od_b_s, moe_router, moe_router_b, moe_w_gate, moe_w_up, moe_w_down):
    P = dict(mod_w=mod_w, mod_b=mod_b, norm_mix=norm_mix, norm_ffn=norm_ffn, final_norm=final_norm,
             ev_w_in=ev_w_in, ev_w_out=ev_w_out, ev_mu_rkv=ev_mu_rkv, ev_mu_wag=ev_mu_wag,
             ev_w0=ev_w0, ev_w1=ev_w1, ev_w2=ev_w2, ev_a0=ev_a0, ev_a1=ev_a1, ev_a2=ev_a2,
             ev_g1=ev_g1, ev_g2=ev_g2, ev_k_k=ev_k_k, ev_k_a=ev_k_a, ev_r_k=ev_r_k,
             ev_gn_w=ev_gn_w, ev_gn_b=ev_gn_b,
             od_w_in=od_w_in, od_w_out=od_w_out, od_conv_w=od_conv_w, od_conv_b=od_conv_b,
             od_cln_g=od_cln_g, od_cln_b=od_cln_b, od_vln_g=od_vln_g, od_vln_b=od_vln_b,
             od_w_s=od_w_s, od_b_s=od_b_s, moe_router=moe_router, moe_router_b=moe_router_b,
             moe_w_gate=moe_w_gate, moe_w_up=moe_w_up, moe_w_down=moe_w_down)
    n_even = state_wkv.shape[1]
    s_zero = jnp.zeros((x_prompt.shape[0], n_even, 2, A_HEADS, A_HEAD_DIM, A_HEAD_DIM), F32)
    mods = modulation(jnp.concatenate([c_ctx[None, :], c], axis=0), mod_w, mod_b)
    no_pos = jnp.zeros(x_prompt.shape[1:], x_prompt.dtype)
    y_prompt, new_state_wkv = run_trunk(x_prompt, no_pos, mods[:, 0:1], s_zero, P)
    y_sample, _ = run_trunk(x_sample, grid_posemb(x_sample.shape[1], x_sample.dtype), mods[:, 1:], state_wkv, P)
    return (y_prompt, y_sample, new_state_wkv)
```
